```python
import jax, jax.numpy as jnp
from jax import lax
import numpy as np

D_MODEL = 2048
BATCH = 2
SEQ = 16384
DEPTH = 2

HEAD_DIM = 128
A_GROUPS = ((128, 1), (512, 4), (2048, 16))
A_HEADS_PER_GROUP = 4
A_HEADS = A_HEADS_PER_GROUP * len(A_GROUPS)
A_WIDTH = A_HEADS * HEAD_DIM
A_OUT = A_HEADS_PER_GROUP * HEAD_DIM
B_HEADS = 8
B_WIDTH = B_HEADS * HEAD_DIM
IDX_HEADS = 16
IDX_DIM = 64
IDX_TOPK = 256
BLOCK = 128
MLP_HIDDEN = 4 * D_MODEL
ROPE_THETA = 10000.0
EPS = 1e-6
N_MOD = 6
N_IN = 3 * A_WIDTH + 3 * B_WIDTH + IDX_HEADS * IDX_DIM + IDX_DIM + IDX_HEADS

kernel_name = "hybrid_dilated_dsa_adaln_block"


def rms_norm(x, gain=None):
    xf = x.astype(jnp.float32)
    y = xf * lax.rsqrt(jnp.mean(xf * xf, axis=-1, keepdims=True) + EPS)
    if gain is not None:
        y = y * gain.astype(jnp.float32)
    return y.astype(x.dtype)


def rope(x, positions):
    d = x.shape[-1]
    half = d // 2
    inv_freq = jnp.power(ROPE_THETA, -jnp.arange(half, dtype=jnp.float32) * 2.0 / d)
    ang = positions.astype(jnp.float32)[..., None] * inv_freq
    cos = jnp.cos(ang)[:, :, None, :]
    sin = jnp.sin(ang)[:, :, None, :]
    xf = x.astype(jnp.float32)
    x1, x2 = xf[..., :half], xf[..., half:]
    return jnp.concatenate([x1 * cos - x2 * sin, x2 * cos + x1 * sin], axis=-1).astype(x.dtype)


def dilated_attention(q, k, v, window, dilation):
    b, s, h, d = q.shape
    r = dilation
    span = window // dilation
    blk = BLOCK
    chunk = r * blk
    sp = -(-s // chunk) * chunk
    m = sp // r
    nb = m // blk

    def to_blocks(t):
        t = jnp.pad(t, ((0, 0), (0, sp - s), (0, 0), (0, 0)))
        t = t.reshape(b, m, r, h, d).transpose(0, 2, 1, 3, 4)
        return t.reshape(b, r, nb, blk, h, d)

    def with_prev(t):
        prev = jnp.pad(t, ((0, 0), (0, 0), (1, 0), (0, 0), (0, 0), (0, 0)))[:, :, :-1]
        return jnp.concatenate([prev, t], axis=3)

    qb = to_blocks(q)
    kk = with_prev(to_blocks(k))
    vv = with_prev(to_blocks(v))
    scores = jnp.einsum('bpnqhd,bpnkhd->bpnhqk', qb, kk).astype(jnp.float32) * (d ** -0.5)
    qi = jnp.arange(blk)[:, None] + blk
    ki = jnp.arange(2 * blk)[None, :]
    dist = qi - ki
    key_idx = jnp.arange(nb)[:, None, None] * blk + ki - blk
    valid = (dist >= 0) & (dist <= span) & (key_idx >= 0)
    scores = jnp.where(valid[:, None], scores, -jnp.inf)
    mx = jnp.max(scores, axis=-1, keepdims=True)
    e = jnp.exp(scores - mx)
    den = jnp.sum(e, axis=-1)
    lse = jnp.swapaxes(mx[..., 0] + jnp.log(den), -1, -2)
    out = jnp.einsum('bpnhqk,bpnkhd->bpnqhd', e.astype(vv.dtype), vv).astype(jnp.float32)
    out = out / jnp.swapaxes(den, -1, -2)[..., None]

    def from_blocks(t):
        t = t.reshape((b, r, m) + t.shape[4:])
        t = jnp.swapaxes(t, 1, 2)
        return t.reshape((b, sp) + t.shape[3:])[:, :s]

    return from_blocks(out), from_blocks(lse)


def dsa_attention(q, k, v, q_idx, k_idx, w_idx):
    b, s, h, d = q.shape
    topk = min(IDX_TOPK, s // 4)
    nb = s // BLOCK
    key_pos = jnp.arange(s)
    gather = jax.vmap(lambda t, i: t[i])

    def one_block(i):
        start = i * BLOCK
        qi = lax.dynamic_slice_in_dim(q_idx, start, BLOCK, axis=1)
        wi = lax.dynamic_slice_in_dim(w_idx, start, BLOCK, axis=1)
        qb = lax.dynamic_slice_in_dim(q, start, BLOCK, axis=1)
        qpos = start + jnp.arange(BLOCK)
        logits = jnp.einsum('bqhd,bsd->bqhs', qi, k_idx).astype(jnp.float32) * (IDX_DIM ** -0.5)
        score = jnp.einsum('bqh,bqhs->bqs', wi.astype(jnp.float32), jax.nn.relu(logits))
        causal = key_pos[None, :] <= qpos[:, None]
        score = jnp.where(causal[None], score, -jnp.inf)
        _, sel = lax.top_k(score, topk)
        ks = gather(k, sel)
        vs = gather(v, sel)
        att = jnp.einsum('bqhd,bqkhd->bhqk', qb, ks).astype(jnp.float32) * (d ** -0.5)
        ok = sel <= qpos[None, :, None]
        att = jnp.where(ok[:, None], att, -jnp.inf)
        p = jax.nn.softmax(att, axis=-1)
        return jnp.einsum('bhqk,bqkhd->bqhd', p.astype(vs.dtype), vs)

    out = lax.map(one_block, jnp.arange(nb))
    return jnp.moveaxis(out, 0, 1).reshape(b, s, h, d)


def token_mixer(h, positions, w_in, a_q_gain, a_k_gain, b_q_gain, b_k_gain, idx_k_gain,
                w_gate, b_gate, w_proj_a, w_proj_b, w_out):
    b, s, _ = h.shape
    z = h @ w_in
    sizes = (A_WIDTH, A_WIDTH, A_WIDTH, B_WIDTH, B_WIDTH, B_WIDTH,
             IDX_HEADS * IDX_DIM, IDX_DIM, IDX_HEADS)
    cuts = []
    acc = 0
    for sz in sizes[:-1]:
        acc += sz
        cuts.append(acc)
    aq, ak, av, bq, bk, bv, iq, ik, iw = jnp.split(z, cuts, axis=-1)

    aq = rope(rms_norm(aq.reshape(b, s, A_HEADS, HEAD_DIM), a_q_gain), positions)
    ak = rope(rms_norm(ak.reshape(b, s, A_HEADS, HEAD_DIM), a_k_gain), positions)
    av = av.reshape(b, s, A_HEADS, HEAD_DIM)
    outs, lses = [], []
    for g, (win, dil) in enumerate(A_GROUPS):
        sl = slice(g * A_HEADS_PER_GROUP, (g + 1) * A_HEADS_PER_GROUP)
        o, lse = dilated_attention(aq[:, :, sl], ak[:, :, sl], av[:, :, sl], win, dil)
        outs.append(o)
        lses.append(lse)
    wts = jax.nn.softmax(jnp.stack(lses), axis=0)
    o_a = jnp.sum(wts[..., None] * jnp.stack(outs), axis=0).astype(h.dtype).reshape(b, s, A_OUT)

    bq = rope(rms_norm(bq.reshape(b, s, B_HEADS, HEAD_DIM), b_q_gain), positions)
    bk = rope(rms_norm(bk.reshape(b, s, B_HEADS, HEAD_DIM), b_k_gain), positions)
    bv = bv.reshape(b, s, B_HEADS, HEAD_DIM)
    iq = rope(iq.reshape(b, s, IDX_HEADS, IDX_DIM), positions)
    ik = rope(rms_norm(ik, idx_k_gain)[:, :, None, :], positions)[:, :, 0, :]
    iw = iw * (IDX_HEADS ** -0.5)
    o_b = dsa_attention(bq, bk, bv, iq, ik, iw).reshape(b, s, B_WIDTH)

    gates = jax.nn.sigmoid((h @ w_gate + b_gate).astype(jnp.float32)).astype(h.dtype)
    g_a, g_b = jnp.split(gates, 2, axis=-1)
    merged = g_a * (o_a @ w_proj_a) + g_b * (o_b @ w_proj_b)
    return merged @ w_out


def setup_inputs(seed: int = 0) -> dict:
    key = jax.random.key(seed)
    ks = jax.random.split(key, 20)
    f32 = jnp.float32

    def nrm(k, shape, scale):
        return jax.random.normal(k, shape, f32) * scale

    def gain(k, n):
        return 1.0 + 0.02 * jax.random.normal(k, (DEPTH, n), f32)

    x = jax.random.normal(ks[0], (BATCH, SEQ, D_MODEL), f32)
    c = jax.random.normal(ks[1], (BATCH, D_MODEL), f32)
    offset = jax.random.randint(ks[2], (BATCH, 1), 0, 1024, dtype=jnp.int32)
    positions = (jnp.arange(SEQ, dtype=jnp.int32)[None, :] + offset).astype(jnp.int32)
    return {
        "x": x,
        "c": c,
        "positions": positions,
        "w_ada": nrm(ks[3], (DEPTH, D_MODEL, N_MOD * D_MODEL), D_MODEL ** -0.5),
        "b_ada": nrm(ks[4], (DEPTH, N_MOD * D_MODEL), 0.01),
        "w_in": nrm(ks[5], (DEPTH, D_MODEL, N_IN), D_MODEL ** -0.5),
        "a_q_gain": gain(ks[6], HEAD_DIM),
        "a_k_gain": gain(ks[7], HEAD_DIM),
        "b_q_gain": gain(ks[8], HEAD_DIM),
        "b_k_gain": gain(ks[9], HEAD_DIM),
        "idx_k_gain": gain(ks[10], IDX_DIM),
        "w_gate": nrm(ks[11], (DEPTH, D_MODEL, 2 * D_MODEL), D_MODEL ** -0.5),
        "b_gate": nrm(ks[12], (DEPTH, 2 * D_MODEL), 0.01),
        "w_proj_a": nrm(ks[13], (DEPTH, A_OUT, D_MODEL), A_OUT ** -0.5),
        "w_proj_b": nrm(ks[14], (DEPTH, B_WIDTH, D_MODEL), B_WIDTH ** -0.5),
        "w_out": nrm(ks[15], (DEPTH, D_MODEL, D_MODEL), D_MODEL ** -0.5),
        "w_up": nrm(ks[16], (DEPTH, D_MODEL, MLP_HIDDEN), D_MODEL ** -0.5),
        "w_down": nrm(ks[17], (DEPTH, MLP_HIDDEN, D_MODEL), MLP_HIDDEN ** -0.5),
    }


def reference(x, c, positions, w_ada, b_ada, w_in, a_q_gain, a_k_gain, b_q_gain, b_k_gain,
              idx_k_gain, w_gate, b_gate, w_proj_a, w_proj_b, w_out, w_up, w_down):
    c_act = jax.nn.silu(c)
    for l in range(DEPTH):
        mod = c_act @ w_ada[l] + b_ada[l]
        sh1, sc1, g1, sh2, sc2, g2 = jnp.split(mod, N_MOD, axis=-1)
        h = rms_norm(x) * (1.0 + sc1[:, None]) + sh1[:, None]
        mix = token_mixer(h, positions, w_in[l], a_q_gain[l], a_k_gain[l], b_q_gain[l],
                          b_k_gain[l], idx_k_gain[l], w_gate[l], b_gate[l],
                          w_proj_a[l], w_proj_b[l], w_out[l])
        x = x + g1[:, None] * mix
        h = rms_norm(x) * (1.0 + sc2[:, None]) + sh2[:, None]
        ffn = jnp.square(jax.nn.relu(h @ w_up[l])) @ w_down[l]
        x = x + g2[:, None] * ffn
    return x
```

```python
import functools

import jax
import jax.numpy as jnp
from jax import lax
from jax.experimental import pallas as pl
from jax.experimental.pallas import tpu as pltpu

F32 = jnp.float32
BF16 = jnp.bfloat16

HEAD_DIM = 128
LANES = 128
A_GROUPS = ((128, 1), (512, 4), (2048, 16))
A_HEADS_PER_GROUP = 4
A_GROUP_WIDTH = A_HEADS_PER_GROUP * HEAD_DIM
A_WIDTH = len(A_GROUPS) * A_GROUP_WIDTH
B_HEADS = 8
B_WIDTH = B_HEADS * HEAD_DIM
IDX_HEADS = 16
IDX_DIM = 64
IDX_WIDTH = IDX_HEADS * IDX_DIM
IDX_TOPK = 256
ROPE_THETA = 10000.0
EPS = 1e-6
N_MOD = 6
NEG = -1e30

COL_TILE = 512
Z_WIDTH = IDX_WIDTH + 3 * B_WIDTH + 3 * A_WIDTH
Z_TILES = Z_WIDTH // COL_TILE
TILE_IQ, TILE_BQ, TILE_BK, TILE_BV, TILE_AQ, TILE_AK, TILE_AV = 0, 2, 4, 6, 8, 11, 14
EPI_ROPE64, EPI_QK, EPI_PLAIN = 0, 1, 2
TILE_EPILOGUE = (EPI_ROPE64,) * 2 + (EPI_QK,) * 4 + (EPI_PLAIN,) * 2 + (EPI_QK,) * 6 + (EPI_PLAIN,) * 3

SEL_TQ = 128
SEL_TK = 512
VMEM_LIMIT = 52 * 1024 * 1024


def _params(*sem):
    return pltpu.CompilerParams(dimension_semantics=sem, vmem_limit_bytes=VMEM_LIMIT)


def _pick(n, pref):
    t = pref
    while n % t:
        t //= 2
    return t


def _rms(x, width):
    return x * lax.rsqrt(jnp.sum(x * x, axis=-1, keepdims=True) * (1.0 / width) + EPS)


def _swap_half64(y):
    lane = lax.broadcasted_iota(jnp.int32, y.shape, 1)
    return jnp.where((lane & 63) < 32, pltpu.roll(y, 96, 1), pltpu.roll(y, 32, 1))


def _rope_tables_kernel(pos_ref, f128_ref, g128_ref, f64_ref, g64_ref, c128_ref, s128_ref, c64_ref, s64_ref):
    pos = pos_ref[...]
    a = pos * f128_ref[...]
    c128_ref[...] = jnp.cos(a)
    s128_ref[...] = jnp.sin(a) * g128_ref[...]
    a = pos * f64_ref[...]
    c64_ref[...] = jnp.cos(a)
    s64_ref[...] = jnp.sin(a) * g64_ref[...]


def _rope_tables(positions):
    n = positions.size
    pos = positions.reshape(n, 1).astype(F32)

    def freq(d):
        half = d // 2
        inv = jnp.power(ROPE_THETA, -jnp.arange(half, dtype=F32) * 2.0 / d)
        f = jnp.tile(jnp.concatenate([inv, inv]), LANES // d)
        g = jnp.tile(jnp.concatenate([-jnp.ones((half,), F32), jnp.ones((half,), F32)]), LANES // d)
        return f.reshape(1, LANES), g.reshape(1, LANES)

    f128, g128 = freq(HEAD_DIM)
    f64, g64 = freq(IDX_DIM)
    tm = _pick(n, 1024)
    row = pl.BlockSpec((1, LANES), lambda i: (0, 0))
    tab = pl.BlockSpec((tm, LANES), lambda i: (i, 0))
    return pl.pallas_call(
        _rope_tables_kernel,
        grid=(n // tm,),
        in_specs=[pl.BlockSpec((tm, 1), lambda i: (i, 0)), row, row, row, row],
        out_specs=[tab, tab, tab, tab],
        out_shape=[jax.ShapeDtypeStruct((n, LANES), F32)] * 4,
        compiler_params=_params("parallel"),
        name="rope_tables",
    )(pos, f128, g128, f64, g64)


def _ada_kernel(c_ref, w_ref, b_ref, o_ref):
    c = c_ref[...]
    act = (c * jax.nn.sigmoid(c)).astype(BF16)
    o_ref[...] = jnp.dot(act, w_ref[...].astype(BF16), preferred_element_type=F32) + b_ref[...]


def _ada(c, w_ada, b_ada):
    depth, d, n6 = w_ada.shape
    b = c.shape[0]
    rows = 8
    c_pad = jnp.zeros((rows, d), F32).at[:b].set(c)
    tn = _pick(n6, 1024)
    out = pl.pallas_call(
        _ada_kernel,
        grid=(depth, n6 // tn),
        in_specs=[
            pl.BlockSpec((rows, d), lambda l, j: (0, 0)),
            pl.BlockSpec((None, d, tn), lambda l, j: (l, 0, j)),
            pl.BlockSpec((None, 1, tn), lambda l, j: (l, 0, j)),
        ],
        out_specs=pl.BlockSpec((None, rows, tn), lambda l, j: (l, 0, j)),
        out_shape=jax.ShapeDtypeStruct((depth, rows, n6), F32),
        compiler_params=_params("parallel", "parallel"),
        name="adaln",
    )(c_pad, w_ada, b_ada.reshape(depth, 1, n6))
    return out[:, :b].reshape(depth, b, N_MOD, 1, d)


def _mod_spec(d, which, tiles_per_batch):
    return pl.BlockSpec((None, None, 1, d), lambda i, *_: (i // tiles_per_batch, which, 0, 0))


def _normmod_kernel(x_ref, sc_ref, sh_ref, o_ref):
    x = x_ref[...]
    y = _rms(x, x.shape[-1])
    o_ref[...] = (y * (1.0 + sc_ref[...]) + sh_ref[...]).astype(o_ref.dtype)


def _normmod(x, mod, seq, which_scale, which_shift):
    n, d = x.shape
    tm = _pick(seq, 512)
    tpb = seq // tm
    return pl.pallas_call(
        _normmod_kernel,
        grid=(n // tm,),
        in_specs=[pl.BlockSpec((tm, d), lambda i: (i, 0)), _mod_spec(d, which_scale, tpb), _mod_spec(d, which_shift, tpb)],
        out_specs=pl.BlockSpec((tm, d), lambda i: (i, 0)),
        out_shape=jax.ShapeDtypeStruct((n, d), BF16),
        compiler_params=_params("parallel"),
        name="normmod",
    )(x, mod, mod)


def _inproj_kernel(h_ref, w_ref, g_ref, c128_ref, s128_ref, c64_ref, s64_ref, o_ref, z_ref):
    j = pl.program_id(1)
    z_ref[...] = jnp.dot(h_ref[...], w_ref[...], preferred_element_type=F32)

    def tiles_of(kind):
        pred = None
        for t, k in enumerate(TILE_EPILOGUE):
            if k == kind:
                pred = (j == t) if pred is None else (pred | (j == t))
        return pred

    @pl.when(tiles_of(EPI_QK))
    def _():
        for hh in range(COL_TILE // HEAD_DIM):
            sl = slice(hh * HEAD_DIM, (hh + 1) * HEAD_DIM)
            y = _rms(z_ref[:, sl], HEAD_DIM) * g_ref[:, sl]
            y = y * c128_ref[...] + pltpu.roll(y, HEAD_DIM // 2, 1) * s128_ref[...]
            o_ref[:, sl] = y.astype(o_ref.dtype)

    @pl.when(tiles_of(EPI_PLAIN))
    def _():
        o_ref[...] = z_ref[...].astype(o_ref.dtype)

    @pl.when(tiles_of(EPI_ROPE64))
    def _():
        for hh in range(COL_TILE // LANES):
            sl = slice(hh * LANES, (hh + 1) * LANES)
            y = z_ref[:, sl]
            y = y * c64_ref[...] + _swap_half64(y) * s64_ref[...]
            o_ref[:, sl] = y.astype(o_ref.dtype)


def _inproj(h, w_main, gain_cols, tabs):
    n, d = h.shape
    tm = _pick(n, 512)
    tab = pl.BlockSpec((tm, LANES), lambda i, j: (i, 0))
    return pl.pallas_call(
        _inproj_kernel,
        grid=(n // tm, Z_TILES),
        in_specs=[
            pl.BlockSpec((tm, d), lambda i, j: (i, 0)),
            pl.BlockSpec((d, COL_TILE), lambda i, j: (0, j)),
            pl.BlockSpec((1, COL_TILE), lambda i, j: (0, j)),
            tab, tab, tab, tab,
        ],
        out_specs=pl.BlockSpec((tm, COL_TILE), lambda i, j: (i, j)),
        out_shape=jax.ShapeDtypeStruct((n, Z_WIDTH), BF16),
        scratch_shapes=[pltpu.VMEM((tm, COL_TILE), F32)],
        compiler_params=_params("parallel", "arbitrary"),
        name="inproj",
    )(h, w_main, gain_cols, *tabs)


def _idxproj_kernel(h_ref, w_ref, g_ref, c64_ref, s64_ref, ik_ref, iw_ref):
    z = jnp.dot(h_ref[...], w_ref[...], preferred_element_type=F32)
    lane = lax.broadcasted_iota(jnp.int32, z.shape, 1)
    is_k = lane < IDX_DIM
    zk = jnp.where(is_k, z, 0.0)
    y = _rms(zk, IDX_DIM) * g_ref[...]
    y = y * c64_ref[...] + _swap_half64(y) * s64_ref[...]
    ik_ref[...] = y[:, :IDX_DIM].astype(ik_ref.dtype)
    iw_ref[...] = z[:, IDX_DIM:IDX_DIM + IDX_HEADS] * (IDX_HEADS ** -0.5 * IDX_DIM ** -0.5)


def _idxproj(h, w_idx, gain_row, c64, s64):
    n, d = h.shape
    tm = _pick(n, 512)
    tab = pl.BlockSpec((tm, LANES), lambda i: (i, 0))
    return pl.pallas_call(
        _idxproj_kernel,
        grid=(n // tm,),
        in_specs=[
            pl.BlockSpec((tm, d), lambda i: (i, 0)),
            pl.BlockSpec((d, LANES), lambda i: (0, 0)),
            pl.BlockSpec((1, LANES), lambda i: (0, 0)),
            tab, tab,
        ],
        out_specs=[pl.BlockSpec((tm, IDX_DIM), lambda i: (i, 0)), pl.BlockSpec((tm, IDX_HEADS), lambda i: (i, 0))],
        out_shape=[jax.ShapeDtypeStruct((n, IDX_DIM), BF16), jax.ShapeDtypeStruct((n, IDX_HEADS), F32)],
        compiler_params=_params("parallel"),
        name="idxproj",
    )(h, w_idx, gain_row, c64, s64)


def _dilated_kernel(q_ref, kc_ref, kp_ref, vc_ref, vp_ref, o_ref, lse_ref, *, tq):
    i = pl.program_id(2)
    blk = LANES
    scale = HEAD_DIM ** -0.5
    row = lax.broadcasted_iota(jnp.int32, (blk, blk), 0)
    col = lax.broadcasted_iota(jnp.int32, (blk, blk), 1)
    cur_ok = col <= row
    prev_ok = col >= row
    lane = lax.broadcasted_iota(jnp.int32, (blk, LANES), 1)
    nt = (((1,), (1,)), ((), ()))
    for j in range(tq // blk):
        rows = slice(j * blk, (j + 1) * blk)
        lse_tile = jnp.zeros((blk, LANES), F32)
        for hh in range(A_HEADS_PER_GROUP):
            cols = slice(hh * HEAD_DIM, (hh + 1) * HEAD_DIM)
            q = q_ref[rows, cols]
            kc = kc_ref[rows, cols]
            vc = vc_ref[rows, cols]
            if j == 0:
                kp, vp = kp_ref[:, cols], vp_ref[:, cols]
                p_ok = prev_ok & (i > 0)
            else:
                prows = slice((j - 1) * blk, j * blk)
                kp, vp = kc_ref[prows, cols], vc_ref[prows, cols]
                p_ok = prev_ok
            s_c = lax.dot_general(q, kc, nt, preferred_element_type=F32) * scale
            s_p = lax.dot_general(q, kp, nt, preferred_element_type=F32) * scale
            s_c = jnp.where(cur_ok, s_c, -jnp.inf)
            s_p = jnp.where(p_ok, s_p, -jnp.inf)
            m = jnp.maximum(jnp.max(s_c, axis=1, keepdims=True), jnp.max(s_p, axis=1, keepdims=True))
            e_c = jnp.exp(s_c - m)
            e_p = jnp.exp(s_p - m)
            den = jnp.sum(e_c, axis=1, keepdims=True) + jnp.sum(e_p, axis=1, keepdims=True)
            acc = jnp.dot(e_c.astype(BF16), vc, preferred_element_type=F32)
            acc = acc + jnp.dot(e_p.astype(BF16), vp, preferred_element_type=F32)
            o_ref[rows, cols] = acc / den
            lse_tile = jnp.where(lane == hh, m + jnp.log(den), lse_tile)
        lse_ref[rows, :] = lse_tile


def _dilated(z3, group, window, dilation):
    b, s, _ = z3.shape
    r = dilation
    assert window // dilation == LANES and s % (r * LANES) == 0
    m = s // r
    zr = z3.reshape(b, m, r * Z_WIDTH)
    tq = _pick(m, 512)
    nsub = tq // LANES

    def cur(tile):
        return pl.BlockSpec((None, tq, COL_TILE), lambda bb, p, i: (bb, i, p * Z_TILES + tile + group))

    def prev(tile):
        return pl.BlockSpec((None, LANES, COL_TILE),
                            lambda bb, p, i: (bb, jnp.maximum(i * nsub - 1, 0), p * Z_TILES + tile + group))

    out, lse = pl.pallas_call(
        functools.partial(_dilated_kernel, tq=tq),
        grid=(b, r, m // tq),
        in_specs=[cur(TILE_AQ), cur(TILE_AK), prev(TILE_AK), cur(TILE_AV), prev(TILE_AV)],
        out_specs=[
            pl.BlockSpec((None, tq, A_GROUP_WIDTH), lambda bb, p, i: (bb, i, p)),
            pl.BlockSpec((None, tq, LANES), lambda bb, p, i: (bb, i, p)),
        ],
        out_shape=[jax.ShapeDtypeStruct((b, m, r * A_GROUP_WIDTH), F32), jax.ShapeDtypeStruct((b, m, r * LANES), F32)],
        compiler_params=_params("parallel", "parallel", "parallel"),
        name=f"dilated_r{r}",
    )(zr, zr, zr, zr, zr)
    return out.reshape(b * s, A_GROUP_WIDTH), lse.reshape(b * s, LANES)


KEY_NEG_INF = -(2 ** 31) + 0x7FFFFF
KEY_POS_INF = 0x7F800000


def _key_to_float(key):
    bits = jnp.where(key >= 0, key, key ^ 0x7FFFFFFF)
    return lax.bitcast_convert_type(bits, F32)


def _select_kernel(iq_ref, w_ref, kt_ref, bias_ref, qs_ref, wb_ref, sc_ref, *, topk):
    i = pl.program_id(1)
    tq, tk = SEL_TQ, SEL_TK
    nk = sc_ref.shape[0]
    nkb = (i * tq + tq + tk - 1) // tk
    rep = tk // LANES

    for h in range(IDX_HEADS):
        qs_ref[h * tq:(h + 1) * tq, :] = iq_ref[:, h * IDX_DIM:(h + 1) * IDX_DIM]
        wb_ref[h] = jnp.broadcast_to(w_ref[:, h:h + 1], (tq, LANES))

    qpos = i * tq + lax.broadcasted_iota(jnp.int32, (tq, tk), 0)
    kpos = lax.broadcasted_iota(jnp.int32, (tq, tk), 1)

    def score_block(kb, carry):
        logits = jnp.dot(qs_ref[...], kt_ref[kb], preferred_element_type=F32)
        acc = jnp.zeros((tq, tk), F32)
        for h in range(IDX_HEADS):
            w_full = jnp.concatenate([wb_ref[h]] * rep, axis=1)
            acc = acc + w_full * jnp.maximum(logits[h * tq:(h + 1) * tq], 0.0)
        sc_ref[kb] = jnp.where(kpos + kb * tk <= qpos, acc, -jnp.inf)
        return carry

    lax.fori_loop(0, nkb, score_block, 0)

    def count_ge(thr):
        t_full = jnp.concatenate([jnp.broadcast_to(thr, (tq, LANES))] * rep, axis=1)

        def body(kb, acc):
            ind = jnp.where(sc_ref[kb] >= t_full, 1.0, 0.0)
            for c in range(rep):
                acc = acc + ind[:, c * LANES:(c + 1) * LANES]
            return acc

        acc = lax.fori_loop(0, nkb, body, jnp.zeros((tq, LANES), F32))
        return jnp.sum(acc, axis=1, keepdims=True)

    def bisect(_, carry):
        lo, hi = carry
        mid = (lo & hi) + ((lo ^ hi) >> 1)
        ge = count_ge(_key_to_float(mid)) >= float(topk)
        return jnp.where(ge, mid, lo), jnp.where(ge, hi, mid)

    lo0 = jnp.full((tq, 1), KEY_NEG_INF, jnp.int32)
    hi0 = jnp.full((tq, 1), KEY_POS_INF + 1, jnp.int32)
    lo, _ = lax.fori_loop(0, 32, bisect, (lo0, hi0))
    t_full = jnp.concatenate([jnp.broadcast_to(_key_to_float(lo), (tq, LANES))] * rep, axis=1)

    def write_block(kb, carry):
        picked = jnp.where(sc_ref[kb] >= t_full, 0.0, NEG)
        bias_ref[kb] = jnp.where(kpos + kb * tk <= qpos, picked, NEG).astype(bias_ref.dtype)
        return carry

    lax.fori_loop(0, nkb, write_block, 0)

    def fill_block(kb, carry):
        bias_ref[kb] = jnp.full((tq, tk), NEG, bias_ref.dtype)
        return carry

    lax.fori_loop(nkb, nk, fill_block, 0)


def _select(z3, iw, ikt, topk):
    b, s, _ = z3.shape
    tq, tk = SEL_TQ, SEL_TK
    nq, nk = s // tq, s // tk
    return pl.pallas_call(
        functools.partial(_select_kernel, topk=topk),
        grid=(b, nq),
        in_specs=[
            pl.BlockSpec((None, tq, IDX_WIDTH), lambda bb, i: (bb, i, TILE_IQ * COL_TILE // IDX_WIDTH)),
            pl.BlockSpec((None, tq, IDX_HEADS), lambda bb, i: (bb, i, 0)),
            pl.BlockSpec((None, nk, IDX_DIM, tk), lambda bb, i: (bb, 0, 0, 0)),
        ],
        out_specs=pl.BlockSpec((None, None, nk, tq, tk), lambda bb, i: (bb, i, 0, 0, 0)),
        out_shape=jax.ShapeDtypeStruct((b, nq, nk, tq, tk), BF16),
        scratch_shapes=[
            pltpu.VMEM((IDX_HEADS * tq, IDX_DIM), BF16),
            pltpu.VMEM((IDX_HEADS, tq, LANES), F32),
            pltpu.VMEM((nk, tq, tk), F32),
        ],
        compiler_params=_params("parallel", "parallel"),
        name="dsa_select",
    )(z3, iw, ikt)


def _dsa_kernel(q_ref, k_ref, v_ref, b_ref, o_ref, m_ref, l_ref, acc_ref, *, tq, tk):
    i = pl.program_id(1)
    kb = pl.program_id(2)
    last = (i * tq + tq - 1) // tk
    scale = HEAD_DIM ** -0.5
    nt = (((1,), (1,)), ((), ()))

    @pl.when(kb == 0)
    def _():
        m_ref[...] = jnp.full(m_ref.shape, NEG, F32)
        l_ref[...] = jnp.zeros(l_ref.shape, F32)
        acc_ref[...] = jnp.zeros(acc_ref.shape, F32)

    @pl.when(kb <= last)
    def _():
        bias = b_ref[...].reshape(tq, tk).astype(F32)
        for h in range(B_HEADS):
            cols = slice(h * HEAD_DIM, (h + 1) * HEAD_DIM)
            s = lax.dot_general(q_ref[:, cols], k_ref[:, cols], nt, preferred_element_type=F32) * scale + bias
            m_prev = m_ref[h]
            m_new = jnp.maximum(m_prev, jnp.max(s, axis=1, keepdims=True))
            alpha = jnp.exp(m_prev - m_new)
            p = jnp.exp(s - m_new)
            l_ref[h] = alpha * l_ref[h] + jnp.sum(p, axis=1, keepdims=True)
            acc_ref[:, cols] = alpha * acc_ref[:, cols] + jnp.dot(p.astype(BF16), v_ref[:, cols],
                                                                  preferred_element_type=F32)
            m_ref[h] = m_new

    @pl.when(kb == last)
    def _():
        for h in range(B_HEADS):
            cols = slice(h * HEAD_DIM, (h + 1) * HEAD_DIM)
            o_ref[:, cols] = (acc_ref[:, cols] / l_ref[h]).astype(o_ref.dtype)


def _dsa_attention(z3, bias5):
    b, s, _ = z3.shape
    tk = SEL_TK
    tq = _pick(s, 512)
    sub = tq // SEL_TQ
    wide = B_WIDTH // COL_TILE

    def last(i):
        return (i * tq + tq - 1) // tk

    return pl.pallas_call(
        functools.partial(_dsa_kernel, tq=tq, tk=tk),
        grid=(b, s // tq, s // tk),
        in_specs=[
            pl.BlockSpec((None, tq, B_WIDTH), lambda bb, i, kb: (bb, i, TILE_BQ // wide)),
            pl.BlockSpec((None, tk, B_WIDTH), lambda bb, i, kb: (bb, jnp.minimum(kb, last(i)), TILE_BK // wide)),
            pl.BlockSpec((None, tk, B_WIDTH), lambda bb, i, kb: (bb, jnp.minimum(kb, last(i)), TILE_BV // wide)),
            pl.BlockSpec((None, sub, None, SEL_TQ, tk), lambda bb, i, kb: (bb, i, jnp.minimum(kb, last(i)), 0, 0)),
        ],
        out_specs=pl.BlockSpec((None, tq, B_WIDTH), lambda bb, i, kb: (bb, i, 0)),
        out_shape=jax.ShapeDtypeStruct((b, s, B_WIDTH), BF16),
        scratch_shapes=[
            pltpu.VMEM((B_HEADS, tq, 1), F32),
            pltpu.VMEM((B_HEADS, tq, 1), F32),
            pltpu.VMEM((tq, B_WIDTH), F32),
        ],
        compiler_params=_params("parallel", "parallel", "arbitrary"),
        name="dsa_attention",
    )(z3, z3, z3, bias5)


def _merge_kernel(o1_ref, o2_ref, o3_ref, l1_ref, l2_ref, l3_ref, ob_ref, h_ref,
                  wga_ref, wgb_ref, bga_ref, bgb_ref, wpa_ref, wpb_ref, out_ref, oa_ref):
    @pl.when(pl.program_id(1) == 0)
    def _():
        l1, l2, l3 = l1_ref[...], l2_ref[...], l3_ref[...]
        mx = jnp.maximum(jnp.maximum(l1, l2), l3)
        e1, e2, e3 = jnp.exp(l1 - mx), jnp.exp(l2 - mx), jnp.exp(l3 - mx)
        tot = e1 + e2 + e3
        w1, w2, w3 = e1 / tot, e2 / tot, e3 / tot
        for hh in range(A_HEADS_PER_GROUP):
            cols = slice(hh * HEAD_DIM, (hh + 1) * HEAD_DIM)
            oa = (w1[:, hh:hh + 1] * o1_ref[:, cols] + w2[:, hh:hh + 1] * o2_ref[:, cols]
                  + w3[:, hh:hh + 1] * o3_ref[:, cols])
            oa_ref[:, cols] = oa.astype(oa_ref.dtype)

    h = h_ref[...]
    ga = jax.nn.sigmoid(jnp.dot(h, wga_ref[...], preferred_element_type=F32) + bga_ref[...])
    gb = jax.nn.sigmoid(jnp.dot(h, wgb_ref[...], preferred_element_type=F32) + bgb_ref[...])
    pa = jnp.dot(oa_ref[...], wpa_ref[...], preferred_element_type=F32)
    pb = jnp.dot(ob_ref[...], wpb_ref[...], preferred_element_type=F32)
    out_ref[...] = (ga * pa + gb * pb).astype(out_ref.dtype)


def _merge(outs, lses, o_b, h, w_gate, b_gate, w_proj_a, w_proj_b):
    n, d = h.shape
    tm = _pick(n, 512)
    tn = _pick(d, COL_TILE)
    nj = d // tn
    row = lambda width: pl.BlockSpec((tm, width), lambda i, j: (i, 0))
    return pl.pallas_call(
        _merge_kernel,
        grid=(n // tm, nj),
        in_specs=[
            row(A_GROUP_WIDTH), row(A_GROUP_WIDTH), row(A_GROUP_WIDTH), row(LANES), row(LANES), row(LANES),
            row(B_WIDTH), row(d),
            pl.BlockSpec((d, tn), lambda i, j: (0, j)),
            pl.BlockSpec((d, tn), lambda i, j: (0, nj + j)),
            pl.BlockSpec((1, tn), lambda i, j: (0, j)),
            pl.BlockSpec((1, tn), lambda i, j: (0, nj + j)),
            pl.BlockSpec((A_GROUP_WIDTH, tn), lambda i, j: (0, j)),
            pl.BlockSpec((B_WIDTH, tn), lambda i, j: (0, j)),
        ],
        out_specs=pl.BlockSpec((tm, tn), lambda i, j: (i, j)),
        out_shape=jax.ShapeDtypeStruct((n, d), BF16),
        scratch_shapes=[pltpu.VMEM((tm, A_GROUP_WIDTH), BF16)],
        compiler_params=_params("parallel", "arbitrary"),
        name="gated_merge",
    )(*outs, *lses, o_b, h, w_gate, w_gate, b_gate, b_gate, w_proj_a, w_proj_b)


def _outproj_kernel(mg_ref, w_ref, x_ref, g_ref, sc_ref, sh_ref, xo_ref, ho_ref):
    mix = jnp.dot(mg_ref[...], w_ref[...], preferred_element_type=F32)
    x = x_ref[...] + g_ref[...] * mix
    xo_ref[...] = x
    ho_ref[...] = (_rms(x, x.shape[-1]) * (1.0 + sc_ref[...]) + sh_ref[...]).astype(ho_ref.dtype)


def _outproj(merged, w_out, x, mod, seq):
    n, d = x.shape
    tm = _pick(seq, 256)
    tpb = seq // tm
    row = pl.BlockSpec((tm, d), lambda i: (i, 0))
    return pl.pallas_call(
        _outproj_kernel,
        grid=(n // tm,),
        in_specs=[row, pl.BlockSpec((d, d), lambda i: (0, 0)), row,
                  _mod_spec(d, 2, tpb), _mod_spec(d, 4, tpb), _mod_spec(d, 3, tpb)],
        out_specs=[row, row],
        out_shape=[jax.ShapeDtypeStruct((n, d), F32), jax.ShapeDtypeStruct((n, d), BF16)],
        compiler_params=_params("parallel"),
        name="outproj",
    )(merged, w_out, x, mod, mod, mod)


def _ffn_kernel(h_ref, wu_ref, wd_ref, x_ref, g_ref, o_ref, acc_ref):
    c = pl.program_id(1)

    @pl.when(c == 0)
    def _():
        acc_ref[...] = jnp.zeros(acc_ref.shape, F32)

    u = jnp.maximum(jnp.dot(h_ref[...], wu_ref[...], preferred_element_type=F32), 0.0)
    acc_ref[...] += jnp.dot((u * u).astype(BF16), wd_ref[...], preferred_element_type=F32)

    @pl.when(c == pl.num_programs(1) - 1)
    def _():
        o_ref[...] = x_ref[...] + g_ref[...] * acc_ref[...]


def _ffn(h2, w_up, w_down, x, mod, seq):
    n, d = x.shape
    hidden = w_up.shape[1]
    tm = _pick(seq, 512)
    tc = _pick(hidden, 512)
    tpb = seq // tm
    row = pl.BlockSpec((tm, d), lambda i, c: (i, 0))
    return pl.pallas_call(
        _ffn_kernel,
        grid=(n // tm, hidden // tc),
        in_specs=[row, pl.BlockSpec((d, tc), lambda i, c: (0, c)), pl.BlockSpec((tc, d), lambda i, c: (c, 0)),
                  row, _mod_spec(d, 5, tpb)],
        out_specs=row,
        out_shape=jax.ShapeDtypeStruct((n, d), F32),
        scratch_shapes=[pltpu.VMEM((tm, d), F32)],
        compiler_params=_params("parallel", "arbitrary"),
        name="ffn",
    )(h2, w_up, w_down, x, mod)


def _pack_in_weights(w_in, a_q_gain, a_k_gain, b_q_gain, b_k_gain, idx_k_gain):
    d = w_in.shape[0]
    sizes = (A_WIDTH, A_WIDTH, A_WIDTH, B_WIDTH, B_WIDTH, B_WIDTH, IDX_WIDTH, IDX_DIM, IDX_HEADS)
    parts, off = [], 0
    for sz in sizes:
        parts.append(w_in[:, off:off + sz])
        off += sz
    aq, ak, av, bq, bk, bv, iq, ik, iw = parts
    w_main = jnp.concatenate([iq, bq, bk, bv, aq, ak, av], axis=1).astype(BF16)
    w_idx = jnp.concatenate([ik, iw, jnp.zeros((d, LANES - IDX_DIM - IDX_HEADS), w_in.dtype)], axis=1).astype(BF16)
    ones = lambda width: jnp.ones((width,), F32)
    gain_cols = jnp.concatenate([
        ones(IDX_WIDTH), jnp.tile(b_q_gain, B_HEADS), jnp.tile(b_k_gain, B_HEADS), ones(B_WIDTH),
        jnp.tile(a_q_gain, A_WIDTH // HEAD_DIM), jnp.tile(a_k_gain, A_WIDTH // HEAD_DIM), ones(A_WIDTH),
    ]).reshape(1, Z_WIDTH)
    idx_gain_row = jnp.concatenate([idx_k_gain, ones(LANES - IDX_DIM)]).reshape(1, LANES)
    return w_main, w_idx, gain_cols, idx_gain_row


def kernel(x, c, positions, w_ada, b_ada, w_in, a_q_gain, a_k_gain, b_q_gain, b_k_gain, idx_k_gain,
           w_gate, b_gate, w_proj_a, w_proj_b, w_out, w_up, w_down):
    b, s, d = x.shape
    depth = w_ada.shape[0]
    n = b * s
    topk = min(IDX_TOPK, s // 4)
    assert s % SEL_TK == 0 and SEL_TK >= topk and d % COL_TILE == 0

    tabs = _rope_tables(positions)
    c128, s128, c64, s64 = tabs
    mods = _ada(c, w_ada, b_ada)
    xf = x.reshape(n, d)

    for l in range(depth):
        mod = mods[l]
        w_main, w_idx, gain_cols, idx_gain_row = _pack_in_weights(
            w_in[l], a_q_gain[l], a_k_gain[l], b_q_gain[l], b_k_gain[l], idx_k_gain[l])

        h = _normmod(xf, mod, s, 1, 0)
        z = _inproj(h, w_main, gain_cols, tabs)
        ik, iw = _idxproj(h, w_idx, idx_gain_row, c64, s64)
        z3 = z.reshape(b, s, Z_WIDTH)

        a_outs, a_lses = [], []
        for g, (window, dilation) in enumerate(A_GROUPS):
            o, lse = _dilated(z3, g, window, dilation)
            a_outs.append(o)
            a_lses.append(lse)

        ikt = ik.reshape(b, s // SEL_TK, SEL_TK, IDX_DIM).transpose(0, 1, 3, 2)
        bias5 = _select(z3, iw.reshape(b, s, IDX_HEADS), ikt, topk)
        o_b = _dsa_attention(z3, bias5).reshape(n, B_WIDTH)

        merged = _merge(a_outs, a_lses, o_b, h, w_gate[l].astype(BF16), b_gate[l].reshape(1, 2 * d),
                        w_proj_a[l].astype(BF16), w_proj_b[l].astype(BF16))
        xf, h2 = _outproj(merged, w_out[l].astype(BF16), xf, mod, s)
        xf = _ffn(h2, w_up[l].astype(BF16), w_down[l].astype(BF16), xf, mod, s)

    return xf.reshape(b, s, d)
```

```python
import functools

import jax
import jax.numpy as jnp
from jax import lax
from jax.experimental import pallas as pl
from jax.experimental.pallas import tpu as pltpu

F32 = jnp.float32
BF16 = jnp.bfloat16

HEAD_DIM = 128
LANES = 128
A_GROUPS = ((128, 1), (512, 4), (2048, 16))
A_HEADS_PER_GROUP = 4
A_GROUP_WIDTH = A_HEADS_PER_GROUP * HEAD_DIM
A_WIDTH = len(A_GROUPS) * A_GROUP_WIDTH
B_HEADS = 8
B_WIDTH = B_HEADS * HEAD_DIM
IDX_HEADS = 16
IDX_DIM = 64
IDX_WIDTH = IDX_HEADS * IDX_DIM
IDX_TOPK = 256
ROPE_THETA = 10000.0
EPS = 1e-6
N_MOD = 6
NEG = -1e30
LOG2_E = 1.4426950408889634
DSA_Q_SCALE = HEAD_DIM ** -0.5 * LOG2_E

COL_TILE = 512
Z_WIDTH = IDX_WIDTH + 3 * B_WIDTH
Z_TILES = Z_WIDTH // COL_TILE
TILE_IQ, TILE_BQ, TILE_BK, TILE_BV = 0, 2, 4, 6
EPI_ROPE64, EPI_QK, EPI_PLAIN = 0, 1, 2
TILE_EPILOGUE = (EPI_ROPE64,) * 2 + (EPI_QK,) * 4 + (EPI_PLAIN,) * 2
A_TILE_EPILOGUE = (EPI_QK, EPI_QK, EPI_PLAIN)
A_PACK_WIDTH = 3 * A_GROUP_WIDTH

SEL_TQ = 128
SEL_TK = 512
VMEM_LIMIT = 52 * 1024 * 1024


def _params(*sem):
    return pltpu.CompilerParams(dimension_semantics=sem, vmem_limit_bytes=VMEM_LIMIT)


def _pick(n, pref):
    t = pref
    while n % t:
        t //= 2
    return t


def _rms(x, width):
    return x * lax.rsqrt(jnp.sum(x * x, axis=-1, keepdims=True) * (1.0 / width) + EPS)


def _swap_half64(y):
    lane = lax.broadcasted_iota(jnp.int32, y.shape, 1)
    return jnp.where((lane & 63) < 32, pltpu.roll(y, 96, 1), pltpu.roll(y, 32, 1))


def _rope_tables_kernel(pos_ref, f128_ref, g128_ref, f64_ref, g64_ref, c128_ref, s128_ref, c64_ref, s64_ref):
    pos = pos_ref[...]
    a = pos * f128_ref[...]
    c128_ref[...] = jnp.cos(a)
    s128_ref[...] = jnp.sin(a) * g128_ref[...]
    a = pos * f64_ref[...]
    c64_ref[...] = jnp.cos(a)
    s64_ref[...] = jnp.sin(a) * g64_ref[...]


def _rope_tables(positions):
    n = positions.size
    pos = positions.reshape(n, 1).astype(F32)

    def freq(d):
        half = d // 2
        inv = jnp.power(ROPE_THETA, -jnp.arange(half, dtype=F32) * 2.0 / d)
        f = jnp.tile(jnp.concatenate([inv, inv]), LANES // d)
        g = jnp.tile(jnp.concatenate([-jnp.ones((half,), F32), jnp.ones((half,), F32)]), LANES // d)
        return f.reshape(1, LANES), g.reshape(1, LANES)

    f128, g128 = freq(HEAD_DIM)
    f64, g64 = freq(IDX_DIM)
    tm = _pick(n, 1024)
    row = pl.BlockSpec((1, LANES), lambda i: (0, 0))
    tab = pl.BlockSpec((tm, LANES), lambda i: (i, 0))
    return pl.pallas_call(
        _rope_tables_kernel,
        grid=(n // tm,),
        in_specs=[pl.BlockSpec((tm, 1), lambda i: (i, 0)), row, row, row, row],
        out_specs=[tab, tab, tab, tab],
        out_shape=[jax.ShapeDtypeStruct((n, LANES), F32)] * 4,
        compiler_params=_params("parallel"),
        name="rope_tables",
    )(pos, f128, g128, f64, g64)


def _ada_kernel(c_ref, w_ref, b_ref, o_ref):
    c = c_ref[...]
    act = (c * jax.nn.sigmoid(c)).astype(BF16)
    o_ref[...] = jnp.dot(act, w_ref[...].astype(BF16), preferred_element_type=F32) + b_ref[...]


def _ada(c, w_ada, b_ada):
    depth, d, n6 = w_ada.shape
    b = c.shape[0]
    rows = 8
    c_pad = jnp.zeros((rows, d), F32).at[:b].set(c)
    tn = _pick(n6, 1024)
    out = pl.pallas_call(
        _ada_kernel,
        grid=(depth, n6 // tn),
        in_specs=[
            pl.BlockSpec((rows, d), lambda l, j: (0, 0)),
            pl.BlockSpec((None, d, tn), lambda l, j: (l, 0, j)),
            pl.BlockSpec((None, 1, tn), lambda l, j: (l, 0, j)),
        ],
        out_specs=pl.BlockSpec((None, rows, tn), lambda l, j: (l, 0, j)),
        out_shape=jax.ShapeDtypeStruct((depth, rows, n6), F32),
        compiler_params=_params("parallel", "parallel"),
        name="adaln",
    )(c_pad, w_ada, b_ada.reshape(depth, 1, n6))
    return out[:, :b].reshape(depth, b, N_MOD, 1, d)


def _mod_spec(d, which, tiles_per_batch):
    return pl.BlockSpec((None, None, 1, d), lambda i, *_: (i // tiles_per_batch, which, 0, 0))


def _normmod_kernel(x_ref, sc_ref, sh_ref, o_ref):
    x = x_ref[...]
    y = _rms(x, x.shape[-1])
    o_ref[...] = (y * (1.0 + sc_ref[...]) + sh_ref[...]).astype(o_ref.dtype)


def _normmod(x, mod, seq, which_scale, which_shift):
    n, d = x.shape
    tm = _pick(seq, 512)
    tpb = seq // tm
    return pl.pallas_call(
        _normmod_kernel,
        grid=(n // tm,),
        in_specs=[pl.BlockSpec((tm, d), lambda i: (i, 0)), _mod_spec(d, which_scale, tpb), _mod_spec(d, which_shift, tpb)],
        out_specs=pl.BlockSpec((tm, d), lambda i: (i, 0)),
        out_shape=jax.ShapeDtypeStruct((n, d), BF16),
        compiler_params=_params("parallel"),
        name="normmod",
    )(x, mod, mod)


def _proj_kernel(h_ref, w_ref, g_ref, c128_ref, s128_ref, c64_ref, s64_ref, o_ref, z_ref, *, epilogues, streams):
    j = pl.program_id(1)
    slabs = COL_TILE // LANES
    z = jnp.dot(h_ref[...], w_ref[...], preferred_element_type=F32)
    for c in range(slabs):
        z_ref[c] = z[:, c * LANES:(c + 1) * LANES]

    def tiles_of(kind):
        pred = None
        for t, k in enumerate(epilogues):
            if k == kind:
                pred = (j == t) if pred is None else (pred | (j == t))
        return pred

    if EPI_QK in epilogues:
        @pl.when(tiles_of(EPI_QK))
        def _():
            for c in range(slabs):
                y = _rms(z_ref[c], HEAD_DIM) * g_ref[:, c * LANES:(c + 1) * LANES]
                z_ref[c] = y * c128_ref[...] + pltpu.roll(y, HEAD_DIM // 2, 1) * s128_ref[...]

    if EPI_ROPE64 in epilogues:
        @pl.when(tiles_of(EPI_ROPE64))
        def _():
            for c in range(slabs):
                y = z_ref[c]
                z_ref[c] = y * c64_ref[...] + _swap_half64(y) * s64_ref[...]

    per = z_ref.shape[1] // streams
    for c in range(slabs):
        cols = slice(c * LANES, (c + 1) * LANES)
        if streams == 1:
            o_ref[..., cols] = z_ref[c].astype(o_ref.dtype).reshape(o_ref.shape[:-1] + (LANES,))
        else:
            for p in range(streams):
                o_ref[p, :, cols] = z_ref[c, pl.ds(p, per, stride=streams), :].astype(o_ref.dtype)


def _proj_call(h, w, gain_cols, tabs, epilogues, out_spec, out_shape, tm, streams, name):
    d = h.shape[1]
    tab = pl.BlockSpec((tm, LANES), lambda i, j: (i, 0))
    return pl.pallas_call(
        functools.partial(_proj_kernel, epilogues=epilogues, streams=streams),
        grid=(h.shape[0] // tm, len(epilogues)),
        in_specs=[
            pl.BlockSpec((tm, d), lambda i, j: (i, 0)),
            pl.BlockSpec((d, COL_TILE), lambda i, j: (0, j)),
            pl.BlockSpec((1, COL_TILE), lambda i, j: (0, j)),
            tab, tab, tab, tab,
        ],
        out_specs=out_spec,
        out_shape=out_shape,
        scratch_shapes=[pltpu.VMEM((COL_TILE // LANES, tm, LANES), F32)],
        compiler_params=_params("parallel", "arbitrary"),
        name=name,
    )(h, w, gain_cols, *tabs)


def _inproj(h, w_main, gain_cols, tabs):
    n = h.shape[0]
    tm = _pick(n, 512)
    return _proj_call(h, w_main, gain_cols, tabs, TILE_EPILOGUE,
                      pl.BlockSpec((tm, COL_TILE), lambda i, j: (i, j)),
                      jax.ShapeDtypeStruct((n, Z_WIDTH), BF16), tm, 1, "inproj")


def _aproj(h, w_group, gain_cols, tabs, batch, seq, dilation):
    r = dilation
    tm = _pick(seq, 512)
    tpb = seq // tm
    assert tm % (r * 16) == 0
    return _proj_call(h, w_group, gain_cols, tabs, A_TILE_EPILOGUE,
                      pl.BlockSpec((None, r, tm // r, COL_TILE), lambda i, j: (i // tpb, 0, i % tpb, j)),
                      jax.ShapeDtypeStruct((batch, r, seq // r, A_PACK_WIDTH), BF16), tm, r, f"aproj_r{r}")


def _idxproj_kernel(h_ref, w_ref, g_ref, c64_ref, s64_ref, ik_ref, iw_ref):
    z = jnp.dot(h_ref[...], w_ref[...], preferred_element_type=F32)
    lane = lax.broadcasted_iota(jnp.int32, z.shape, 1)
    is_k = lane < IDX_DIM
    zk = jnp.where(is_k, z, 0.0)
    y = _rms(zk, IDX_DIM) * g_ref[...]
    y = y * c64_ref[...] + _swap_half64(y) * s64_ref[...]
    ik_ref[...] = y[:, :IDX_DIM].astype(ik_ref.dtype)
    iw_ref[...] = z[:, IDX_DIM:IDX_DIM + IDX_HEADS] * (IDX_HEADS ** -0.5 * IDX_DIM ** -0.5)


def _idxproj(h, w_idx, gain_row, c64, s64):
    n, d = h.shape
    tm = _pick(n, 512)
    tab = pl.BlockSpec((tm, LANES), lambda i: (i, 0))
    return pl.pallas_call(
        _idxproj_kernel,
        grid=(n // tm,),
        in_specs=[
            pl.BlockSpec((tm, d), lambda i: (i, 0)),
            pl.BlockSpec((d, LANES), lambda i: (0, 0)),
            pl.BlockSpec((1, LANES), lambda i: (0, 0)),
            tab, tab,
        ],
        out_specs=[pl.BlockSpec((tm, IDX_DIM), lambda i: (i, 0)), pl.BlockSpec((tm, IDX_HEADS), lambda i: (i, 0))],
        out_shape=[jax.ShapeDtypeStruct((n, IDX_DIM), BF16), jax.ShapeDtypeStruct((n, IDX_HEADS), F32)],
        compiler_params=_params("parallel"),
        name="idxproj",
    )(h, w_idx, gain_row, c64, s64)


def _dilated_kernel(q_ref, kc_ref, kp_ref, vc_ref, vp_ref, o_ref, lse_ref, *, tq):
    i = pl.program_id(2)
    blk = LANES
    scale = HEAD_DIM ** -0.5
    row = lax.broadcasted_iota(jnp.int32, (blk, blk), 0)
    col = lax.broadcasted_iota(jnp.int32, (blk, blk), 1)
    cur_ok = col <= row
    prev_ok = col >= row
    lane = lax.broadcasted_iota(jnp.int32, (blk, LANES), 1)
    nt = (((1,), (1,)), ((), ()))
    for j in range(tq // blk):
        rows = slice(j * blk, (j + 1) * blk)
        lse_tile = jnp.zeros((blk, LANES), F32)
        for hh in range(A_HEADS_PER_GROUP):
            cols = slice(hh * HEAD_DIM, (hh + 1) * HEAD_DIM)
            q = q_ref[rows, cols]
            kc = kc_ref[rows, cols]
            vc = vc_ref[rows, cols]
            if j == 0:
                kp, vp = kp_ref[:, cols], vp_ref[:, cols]
                p_ok = prev_ok & (i > 0)
            else:
                prows = slice((j - 1) * blk, j * blk)
                kp, vp = kc_ref[prows, cols], vc_ref[prows, cols]
                p_ok = prev_ok
            s_c = lax.dot_general(q, kc, nt, preferred_element_type=F32) * scale
            s_p = lax.dot_general(q, kp, nt, preferred_element_type=F32) * scale
            s_c = jnp.where(cur_ok, s_c, -jnp.inf)
            s_p = jnp.where(p_ok, s_p, -jnp.inf)
            m = jnp.maximum(jnp.max(s_c, axis=1, keepdims=True), jnp.max(s_p, axis=1, keepdims=True))
            e_c = jnp.exp(s_c - m)
            e_p = jnp.exp(s_p - m)
            den = jnp.sum(e_c, axis=1, keepdims=True) + jnp.sum(e_p, axis=1, keepdims=True)
            acc = jnp.dot(e_c.astype(BF16), vc, preferred_element_type=F32)
            acc = acc + jnp.dot(e_p.astype(BF16), vp, preferred_element_type=F32)
            o_ref[rows, cols] = acc / den
            lse_tile = jnp.where(lane == hh, m + jnp.log(den), lse_tile)
        lse_ref[rows, :] = lse_tile


def _dilated(qkv, window):
    b, r, m, _ = qkv.shape
    assert window // r == LANES and m % LANES == 0
    tq = _pick(m, 512)
    nsub = tq // LANES

    def cur(tile):
        return pl.BlockSpec((None, None, tq, COL_TILE), lambda bb, p, i: (bb, p, i, tile))

    def prev(tile):
        return pl.BlockSpec((None, None, LANES, COL_TILE),
                            lambda bb, p, i: (bb, p, jnp.maximum(i * nsub - 1, 0), tile))

    return pl.pallas_call(
        functools.partial(_dilated_kernel, tq=tq),
        grid=(b, r, m // tq),
        in_specs=[cur(0), cur(1), prev(1), cur(2), prev(2)],
        out_specs=[
            pl.BlockSpec((None, None, tq, A_GROUP_WIDTH), lambda bb, p, i: (bb, p, i, 0)),
            pl.BlockSpec((None, None, tq, LANES), lambda bb, p, i: (bb, p, i, 0)),
        ],
        out_shape=[jax.ShapeDtypeStruct((b, r, m, A_GROUP_WIDTH), F32), jax.ShapeDtypeStruct((b, r, m, LANES), F32)],
        compiler_params=_params("parallel", "parallel", "parallel"),
        name=f"dilated_r{r}",
    )(qkv, qkv, qkv, qkv, qkv)


KEY_NEG_INF = -(2 ** 31) + 0x7FFFFF
KEY_POS_INF = 0x7F800000


def _key_to_float(key):
    bits = jnp.where(key >= 0, key, key ^ 0x7FFFFFFF)
    return lax.bitcast_convert_type(bits, F32)


def _float_to_key(x):
    bits = lax.bitcast_convert_type(x, jnp.int32)
    return jnp.where(bits >= 0, bits, bits ^ 0x7FFFFFFF)


def _select_kernel(iq_ref, w_ref, kt_ref, bias_ref, qs_ref, wb_ref, sc_ref, hi_ref, lo_ref, *, topk):
    i = pl.program_id(1)
    tq, tk = SEL_TQ, SEL_TK
    nk = sc_ref.shape[0]
    nkb = (i * tq + tq + tk - 1) // tk
    rep = tk // LANES

    for h in range(IDX_HEADS):
        qs_ref[h * tq:(h + 1) * tq, :] = iq_ref[:, h * IDX_DIM:(h + 1) * IDX_DIM]
        wb_ref[h] = jnp.broadcast_to(w_ref[:, h:h + 1], (tq, LANES))

    qpos = i * tq + lax.broadcasted_iota(jnp.int32, (tq, tk), 0)
    kpos = lax.broadcasted_iota(jnp.int32, (tq, tk), 1)

    def score_block(kb, carry):
        logits = jnp.dot(qs_ref[...], kt_ref[kb], preferred_element_type=F32)
        acc = jnp.zeros((tq, tk), F32)
        for h in range(IDX_HEADS):
            w_full = jnp.concatenate([wb_ref[h]] * rep, axis=1)
            acc = acc + w_full * jnp.maximum(logits[h * tq:(h + 1) * tq], 0.0)
        masked = jnp.where(kpos + kb * tk <= qpos, acc, -jnp.inf)
        sc_ref[kb] = masked
        hi_ref[kb] = (_float_to_key(masked) >> 16).astype(jnp.int16)
        return carry

    lax.fori_loop(0, nkb, score_block, 0)

    def bisect16(ref, need):
        def count_ge(t):
            t_full = jnp.concatenate([jnp.broadcast_to(t.astype(jnp.int16), (tq, LANES))] * rep, axis=1)

            def body(kb, acc):
                ind = jnp.where(ref[kb] >= t_full, jnp.int16(1), jnp.int16(0))
                for c in range(rep):
                    acc = acc + ind[:, c * LANES:(c + 1) * LANES]
                return acc

            acc = lax.fori_loop(0, nkb, body, jnp.zeros((tq, LANES), jnp.int16))
            return jnp.sum(acc.astype(jnp.int32), axis=1, keepdims=True)

        def step(_, carry):
            lo, hi, above = carry
            mid = (lo + hi) >> 1
            cnt = count_ge(mid)
            ge = cnt >= need
            return jnp.where(ge, mid, lo), jnp.where(ge, hi, mid), jnp.where(ge, above, cnt)

        lo0 = jnp.full((tq, 1), -(2 ** 15), jnp.int32)
        hi0 = jnp.full((tq, 1), 2 ** 15, jnp.int32)
        lo, _, above = lax.fori_loop(0, 16, step, (lo0, hi0, jnp.zeros((tq, 1), jnp.int32)))
        return lo, above

    key_hi, above = bisect16(hi_ref, topk)
    key_hi_full = jnp.concatenate([jnp.broadcast_to(key_hi, (tq, LANES))] * rep, axis=1)

    def low_digits(kb, carry):
        key = _float_to_key(sc_ref[kb])
        low = (key & 0xFFFF) - 2 ** 15
        lo_ref[kb] = jnp.where((key >> 16) == key_hi_full, low, -(2 ** 15)).astype(jnp.int16)
        return carry

    lax.fori_loop(0, nkb, low_digits, 0)
    key_lo, _ = bisect16(lo_ref, topk - above)
    thr = _key_to_float((key_hi << 16) | (key_lo + 2 ** 15))
    t_full = jnp.concatenate([jnp.broadcast_to(thr, (tq, LANES))] * rep, axis=1)

    def write_block(kb, carry):
        picked = jnp.where(sc_ref[kb] >= t_full, 0.0, NEG)
        bias_ref[kb] = jnp.where(kpos + kb * tk <= qpos, picked, NEG).astype(bias_ref.dtype)
        return carry

    lax.fori_loop(0, nkb, write_block, 0)

    def fill_block(kb, carry):
        bias_ref[kb] = jnp.full((tq, tk), NEG, bias_ref.dtype)
        return carry

    lax.fori_loop(nkb, nk, fill_block, 0)


def _select(z3, iw, ikt, topk):
    b, s, _ = z3.shape
    tq, tk = SEL_TQ, SEL_TK
    nq, nk = s // tq, s // tk
    return pl.pallas_call(
        functools.partial(_select_kernel, topk=topk),
        grid=(b, nq),
        in_specs=[
            pl.BlockSpec((None, tq, IDX_WIDTH), lambda bb, i: (bb, i, TILE_IQ * COL_TILE // IDX_WIDTH)),
            pl.BlockSpec((None, tq, IDX_HEADS), lambda bb, i: (bb, i, 0)),
            pl.BlockSpec((None, nk, IDX_DIM, tk), lambda bb, i: (bb, 0, 0, 0)),
        ],
        out_specs=pl.BlockSpec((None, None, nk, tq, tk), lambda bb, i: (bb, i, 0, 0, 0)),
        out_shape=jax.ShapeDtypeStruct((b, nq, nk, tq, tk), BF16),
        scratch_shapes=[
            pltpu.VMEM((IDX_HEADS * tq, IDX_DIM), BF16),
            pltpu.VMEM((IDX_HEADS, tq, LANES), F32),
            pltpu.VMEM((nk, tq, tk), F32),
            pltpu.VMEM((nk, tq, tk), jnp.int16),
            pltpu.VMEM((nk, tq, tk), jnp.int16),
        ],
        compiler_params=_params("parallel", "parallel"),
        name="dsa_select",
    )(z3, iw, ikt)


def _dsa_kernel(q_ref, k_ref, v_ref, b_ref, o_ref, m_ref, l_ref, acc_ref, *, tq, tk):
    i = pl.program_id(1)
    kb = pl.program_id(2)
    last = (i * tq + tq - 1) // tk
    nt = (((1,), (1,)), ((), ()))

    @pl.when(kb == 0)
    def _():
        m_ref[...] = jnp.full(m_ref.shape, NEG, F32)
        l_ref[...] = jnp.zeros(l_ref.shape, F32)
        acc_ref[...] = jnp.zeros(acc_ref.shape, F32)

    @pl.when(kb <= last)
    def _():
        bias = b_ref[...].reshape(tq, tk).astype(F32)
        rep = tk // LANES

        def scores(h):
            cols = slice(h * HEAD_DIM, (h + 1) * HEAD_DIM)
            return lax.dot_general(q_ref[:, cols], k_ref[:, cols], nt, preferred_element_type=F32) + bias

        s_next = scores(0)
        for h in range(B_HEADS):
            cols = slice(h * HEAD_DIM, (h + 1) * HEAD_DIM)
            s = s_next
            if h + 1 < B_HEADS:
                s_next = scores(h + 1)
            m_prev = m_ref[h]
            m_new = jnp.maximum(m_prev, jnp.max(s, axis=1, keepdims=True))
            alpha = jnp.exp2(m_prev - m_new)
            p = jnp.exp2(s - jnp.concatenate([m_new] * rep, axis=1))
            l_ref[h] = alpha * l_ref[h] + jnp.sum(p, axis=1, keepdims=True)
            acc_ref[:, cols] = alpha * acc_ref[:, cols] + jnp.dot(p.astype(BF16), v_ref[:, cols],
                                                                  preferred_element_type=F32)
            m_ref[h] = m_new

    @pl.when(kb == last)
    def _():
        for h in range(B_HEADS):
            cols = slice(h * HEAD_DIM, (h + 1) * HEAD_DIM)
            o_ref[:, cols] = (acc_ref[:, cols] / l_ref[h]).astype(o_ref.dtype)


def _dsa_attention(z3, bias5):
    b, s, _ = z3.shape
    tk = SEL_TK
    tq = _pick(s, 512)
    sub = tq // SEL_TQ
    wide = B_WIDTH // COL_TILE

    def last(i):
        return (i * tq + tq - 1) // tk

    return pl.pallas_call(
        functools.partial(_dsa_kernel, tq=tq, tk=tk),
        grid=(b, s // tq, s // tk),
        in_specs=[
            pl.BlockSpec((None, tq, B_WIDTH), lambda bb, i, kb: (bb, i, TILE_BQ // wide)),
            pl.BlockSpec((None, tk, B_WIDTH), lambda bb, i, kb: (bb, jnp.minimum(kb, last(i)), TILE_BK // wide)),
            pl.BlockSpec((None, tk, B_WIDTH), lambda bb, i, kb: (bb, jnp.minimum(kb, last(i)), TILE_BV // wide)),
            pl.BlockSpec((None, sub, None, SEL_TQ, tk), lambda bb, i, kb: (bb, i, jnp.minimum(kb, last(i)), 0, 0)),
        ],
        out_specs=pl.BlockSpec((None, tq, B_WIDTH), lambda bb, i, kb: (bb, i, 0)),
        out_shape=jax.ShapeDtypeStruct((b, s, B_WIDTH), BF16),
        scratch_shapes=[
            pltpu.VMEM((B_HEADS, tq, LANES), F32),
            pltpu.VMEM((B_HEADS, tq, LANES), F32),
            pltpu.VMEM((tq, B_WIDTH), F32),
        ],
        compiler_params=_params("parallel", "parallel", "arbitrary"),
        name="dsa_attention",
    )(z3, z3, z3, bias5)


def _to_token_order(dst_ref, first, src_ref):
    r, per, width = src_ref.shape
    for c in range(width // LANES):
        cols = slice(c * LANES, (c + 1) * LANES)
        if r == 1:
            dst_ref[first + c] = src_ref[0, :, cols]
        else:
            for p in range(r):
                dst_ref[first + c, pl.ds(p, per, stride=r), :] = src_ref[p, :, cols]


def _merge_kernel(o1_ref, o2_ref, o3_ref, l1_ref, l2_ref, l3_ref, ob_ref, h_ref,
                  wga_ref, wgb_ref, bga_ref, bgb_ref, wpa_ref, wpb_ref, out_ref, oa_ref, ot_ref, lt_ref):
    @pl.when(pl.program_id(1) == 0)
    def _():
        nh = A_HEADS_PER_GROUP
        for g, (o_ref, l_ref) in enumerate(((o1_ref, l1_ref), (o2_ref, l2_ref), (o3_ref, l3_ref))):
            _to_token_order(ot_ref, g * nh, o_ref)
            _to_token_order(lt_ref, g, l_ref)
        l1, l2, l3 = lt_ref[0], lt_ref[1], lt_ref[2]
        mx = jnp.maximum(jnp.maximum(l1, l2), l3)
        e1, e2, e3 = jnp.exp(l1 - mx), jnp.exp(l2 - mx), jnp.exp(l3 - mx)
        tot = e1 + e2 + e3
        w1, w2, w3 = e1 / tot, e2 / tot, e3 / tot
        for hh in range(nh):
            oa = (w1[:, hh:hh + 1] * ot_ref[hh] + w2[:, hh:hh + 1] * ot_ref[nh + hh]
                  + w3[:, hh:hh + 1] * ot_ref[2 * nh + hh])
            oa_ref[:, hh * HEAD_DIM:(hh + 1) * HEAD_DIM] = oa.astype(oa_ref.dtype)

    h = h_ref[...]
    ga = jax.nn.sigmoid(jnp.dot(h, wga_ref[...], preferred_element_type=F32) + bga_ref[...])
    gb = jax.nn.sigmoid(jnp.dot(h, wgb_ref[...], preferred_element_type=F32) + bgb_ref[...])
    pa = jnp.dot(oa_ref[...], wpa_ref[...], preferred_element_type=F32)
    pb = jnp.dot(ob_ref[...], wpb_ref[...], preferred_element_type=F32)
    out_ref[...] = (ga * pa + gb * pb).astype(out_ref.dtype)


def _merge(outs, lses, o_b, h, w_gate, b_gate, w_proj_a, w_proj_b, seq):
    n, d = h.shape
    tm = _pick(seq, 512)
    tpb = seq // tm
    tn = _pick(d, COL_TILE)
    nj = d // tn
    row = lambda width: pl.BlockSpec((tm, width), lambda i, j: (i, 0))

    def streams(arr):
        r, width = arr.shape[1], arr.shape[3]
        return pl.BlockSpec((None, r, tm // r, width), lambda i, j: (i // tpb, 0, i % tpb, 0))

    return pl.pallas_call(
        _merge_kernel,
        grid=(n // tm, nj),
        in_specs=[
            *[streams(a) for a in outs], *[streams(a) for a in lses],
            row(B_WIDTH), row(d),
            pl.BlockSpec((d, tn), lambda i, j: (0, j)),
            pl.BlockSpec((d, tn), lambda i, j: (0, nj + j)),
            pl.BlockSpec((1, tn), lambda i, j: (0, j)),
            pl.BlockSpec((1, tn), lambda i, j: (0, nj + j)),
            pl.BlockSpec((A_GROUP_WIDTH, tn), lambda i, j: (0, j)),
            pl.BlockSpec((B_WIDTH, tn), lambda i, j: (0, j)),
        ],
        out_specs=pl.BlockSpec((tm, tn), lambda i, j: (i, j)),
        out_shape=jax.ShapeDtypeStruct((n, d), BF16),
        scratch_shapes=[
            pltpu.VMEM((tm, A_GROUP_WIDTH), BF16),
            pltpu.VMEM((len(outs) * A_HEADS_PER_GROUP, tm, HEAD_DIM), F32),
            pltpu.VMEM((len(lses), tm, LANES), F32),
        ],
        compiler_params=_params("parallel", "arbitrary"),
        name="gated_merge",
    )(*outs, *lses, o_b, h, w_gate, w_gate, b_gate, b_gate, w_proj_a, w_proj_b)


def _outproj_kernel(mg_ref, w_ref, x_ref, g_ref, sc_ref, sh_ref, xo_ref, ho_ref):
    mix = jnp.dot(mg_ref[...], w_ref[...], preferred_element_type=F32)
    x = x_ref[...] + g_ref[...] * mix
    xo_ref[...] = x
    ho_ref[...] = (_rms(x, x.shape[-1]) * (1.0 + sc_ref[...]) + sh_ref[...]).astype(ho_ref.dtype)


def _outproj(merged, w_out, x, mod, seq):
    n, d = x.shape
    tm = _pick(seq, 256)
    tpb = seq // tm
    row = pl.BlockSpec((tm, d), lambda i: (i, 0))
    return pl.pallas_call(
        _outproj_kernel,
        grid=(n // tm,),
        in_specs=[row, pl.BlockSpec((d, d), lambda i: (0, 0)), row,
                  _mod_spec(d, 2, tpb), _mod_spec(d, 4, tpb), _mod_spec(d, 3, tpb)],
        out_specs=[row, row],
        out_shape=[jax.ShapeDtypeStruct((n, d), F32), jax.ShapeDtypeStruct((n, d), BF16)],
        compiler_params=_params("parallel"),
        name="outproj",
    )(merged, w_out, x, mod, mod, mod)


def _ffn_kernel(h_ref, wu_ref, wd_ref, x_ref, g_ref, o_ref, acc_ref):
    c = pl.program_id(1)

    @pl.when(c == 0)
    def _():
        acc_ref[...] = jnp.zeros(acc_ref.shape, F32)

    u = jnp.maximum(jnp.dot(h_ref[...], wu_ref[...], preferred_element_type=F32), 0.0)
    acc_ref[...] += jnp.dot((u * u).astype(BF16), wd_ref[...], preferred_element_type=F32)

    @pl.when(c == pl.num_programs(1) - 1)
    def _():
        o_ref[...] = x_ref[...] + g_ref[...] * acc_ref[...]


def _ffn(h2, w_up, w_down, x, mod, seq):
    n, d = x.shape
    hidden = w_up.shape[1]
    tm = _pick(seq, 512)
    tc = _pick(hidden, 512)
    tpb = seq // tm
    row = pl.BlockSpec((tm, d), lambda i, c: (i, 0))
    return pl.pallas_call(
        _ffn_kernel,
        grid=(n // tm, hidden // tc),
        in_specs=[row, pl.BlockSpec((d, tc), lambda i, c: (0, c)), pl.BlockSpec((tc, d), lambda i, c: (c, 0)),
                  row, _mod_spec(d, 5, tpb)],
        out_specs=row,
        out_shape=jax.ShapeDtypeStruct((n, d), F32),
        scratch_shapes=[pltpu.VMEM((tm, d), F32)],
        compiler_params=_params("parallel", "arbitrary"),
        name="ffn",
    )(h2, w_up, w_down, x, mod)


def _pack_in_weights(w_in, a_q_gain, a_k_gain, b_q_gain, b_k_gain, idx_k_gain):
    d = w_in.shape[0]
    sizes = (A_WIDTH, A_WIDTH, A_WIDTH, B_WIDTH, B_WIDTH, B_WIDTH, IDX_WIDTH, IDX_DIM, IDX_HEADS)
    parts, off = [], 0
    for sz in sizes:
        parts.append(w_in[:, off:off + sz])
        off += sz
    aq, ak, av, bq, bk, bv, iq, ik, iw = parts
    w_main = jnp.concatenate([iq, bq, bk, bv], axis=1).astype(BF16)
    w_idx = jnp.concatenate([ik, iw, jnp.zeros((d, LANES - IDX_DIM - IDX_HEADS), w_in.dtype)], axis=1).astype(BF16)
    ones = lambda width: jnp.ones((width,), F32)
    gain_cols = jnp.concatenate([
        ones(IDX_WIDTH), jnp.tile(b_q_gain * DSA_Q_SCALE, B_HEADS), jnp.tile(b_k_gain, B_HEADS), ones(B_WIDTH),
    ]).reshape(1, Z_WIDTH)
    idx_gain_row = jnp.concatenate([idx_k_gain, ones(LANES - IDX_DIM)]).reshape(1, LANES)
    w_groups = []
    for g in range(len(A_GROUPS)):
        sl = slice(g * A_GROUP_WIDTH, (g + 1) * A_GROUP_WIDTH)
        w_groups.append(jnp.concatenate([aq[:, sl], ak[:, sl], av[:, sl]], axis=1).astype(BF16))
    a_gain_cols = jnp.concatenate([
        jnp.tile(a_q_gain, A_HEADS_PER_GROUP), jnp.tile(a_k_gain, A_HEADS_PER_GROUP), ones(A_GROUP_WIDTH),
    ]).reshape(1, A_PACK_WIDTH)
    return w_main, w_idx, gain_cols, idx_gain_row, w_groups, a_gain_cols


def kernel(x, c, positions, w_ada, b_ada, w_in, a_q_gain, a_k_gain, b_q_gain, b_k_gain, idx_k_gain,
           w_gate, b_gate, w_proj_a, w_proj_b, w_out, w_up, w_down):
    b, s, d = x.shape
    depth = w_ada.shape[0]
    n = b * s
    topk = min(IDX_TOPK, s // 4)
    assert s % SEL_TK == 0 and SEL_TK >= topk and d % COL_TILE == 0 and s // LANES < 2 ** 15

    tabs = _rope_tables(positions)
    c128, s128, c64, s64 = tabs
    mods = _ada(c, w_ada, b_ada)
    xf = x.reshape(n, d)

    for l in range(depth):
        mod = mods[l]
        w_main, w_idx, gain_cols, idx_gain_row, w_groups, a_gain_cols = _pack_in_weights(
            w_in[l], a_q_gain[l], a_k_gain[l], b_q_gain[l], b_k_gain[l], idx_k_gain[l])

        h = _normmod(xf, mod, s, 1, 0)
        z = _inproj(h, w_main, gain_cols, tabs)
        ik, iw = _idxproj(h, w_idx, idx_gain_row, c64, s64)
        z3 = z.reshape(b, s, Z_WIDTH)

        a_outs, a_lses = [], []
        for g, (window, dilation) in enumerate(A_GROUPS):
            qkv = _aproj(h, w_groups[g], a_gain_cols, tabs, b, s, dilation)
            o, lse = _dilated(qkv, window)
            a_outs.append(o)
            a_lses.append(lse)

        ikt = ik.reshape(b, s // SEL_TK, SEL_TK, IDX_DIM).transpose(0, 1, 3, 2)
        bias5 = _select(z3, iw.reshape(b, s, IDX_HEADS), ikt, topk)
        o_b = _dsa_attention(z3, bias5).reshape(n, B_WIDTH)

        merged = _merge(a_outs, a_lses, o_b, h, w_gate[l].astype(BF16), b_gate[l].reshape(1, 2 * d),
                        w_proj_a[l].astype(BF16), w_proj_b[l].astype(BF16), s)
        xf, h2 = _outproj(merged, w_out[l].astype(BF16), xf, mod, s)
        xf = _ffn(h2, w_up[l].astype(BF16), w_down[l].astype(BF16), xf, mod, s)

    return xf.reshape(b, s, d)
```

```python
import functools

import jax
import jax.numpy as jnp
from jax import lax
from jax.experimental import pallas as pl
from jax.experimental.pallas import tpu as pltpu

F32 = jnp.float32
BF16 = jnp.bfloat16

HEAD_DIM = 128
LANES = 128
A_GROUPS = ((128, 1), (512, 4), (2048, 16))
A_HEADS_PER_GROUP = 4
A_GROUP_WIDTH = A_HEADS_PER_GROUP * HEAD_DIM
A_WIDTH = len(A_GROUPS) * A_GROUP_WIDTH
B_HEADS = 8
B_WIDTH = B_HEADS * HEAD_DIM
IDX_HEADS = 16
IDX_DIM = 64
IDX_WIDTH = IDX_HEADS * IDX_DIM
IDX_TOPK = 256
ROPE_THETA = 10000.0
EPS = 1e-6
N_MOD = 6
NEG = -1e30
LOG2_E = 1.4426950408889634
DSA_Q_SCALE = HEAD_DIM ** -0.5 * LOG2_E

COL_TILE = 512
MXU_COLS = 256
EPI_ROPE64, EPI_QK, EPI_PLAIN = 0, 1, 2
IV_EPILOGUES = (EPI_ROPE64,) * (IDX_WIDTH // MXU_COLS) + (EPI_PLAIN,) * (B_WIDTH // MXU_COLS)
QK_EPILOGUES = (EPI_QK,) * (2 * B_WIDTH // MXU_COLS)
A_EPILOGUES = (EPI_QK,) * (2 * A_GROUP_WIDTH // MXU_COLS) + (EPI_PLAIN,) * (A_GROUP_WIDTH // MXU_COLS)
A_PACK_WIDTH = 3 * A_GROUP_WIDTH

SEL_TQ = 128
SEL_TK = 512
VMEM_LIMIT = 52 * 1024 * 1024


def _params(*sem):
    return pltpu.CompilerParams(dimension_semantics=sem, vmem_limit_bytes=VMEM_LIMIT)


def _pick(n, pref):
    t = pref
    while n % t:
        t //= 2
    return t


def _rms(x, width):
    return x * lax.rsqrt(jnp.sum(x * x, axis=-1, keepdims=True) * (1.0 / width) + EPS)


def _swap_half64(y):
    lane = lax.broadcasted_iota(jnp.int32, y.shape, 1)
    return jnp.where((lane & 63) < 32, pltpu.roll(y, 96, 1), pltpu.roll(y, 32, 1))


def _rope_tables_kernel(pos_ref, f128_ref, g128_ref, f64_ref, g64_ref, c128_ref, s128_ref, c64_ref, s64_ref):
    pos = pos_ref[...]
    a = pos * f128_ref[...]
    c128_ref[...] = jnp.cos(a)
    s128_ref[...] = jnp.sin(a) * g128_ref[...]
    a = pos * f64_ref[...]
    c64_ref[...] = jnp.cos(a)
    s64_ref[...] = jnp.sin(a) * g64_ref[...]


def _rope_tables(positions):
    n = positions.size
    pos = positions.reshape(n, 1).astype(F32)

    def freq(d):
        half = d // 2
        inv = jnp.power(ROPE_THETA, -jnp.arange(half, dtype=F32) * 2.0 / d)
        f = jnp.tile(jnp.concatenate([inv, inv]), LANES // d)
        g = jnp.tile(jnp.concatenate([-jnp.ones((half,), F32), jnp.ones((half,), F32)]), LANES // d)
        return f.reshape(1, LANES), g.reshape(1, LANES)

    f128, g128 = freq(HEAD_DIM)
    f64, g64 = freq(IDX_DIM)
    tm = _pick(n, 1024)
    row = pl.BlockSpec((1, LANES), lambda i: (0, 0))
    tab = pl.BlockSpec((tm, LANES), lambda i: (i, 0))
    return pl.pallas_call(
        _rope_tables_kernel,
        grid=(n // tm,),
        in_specs=[pl.BlockSpec((tm, 1), lambda i: (i, 0)), row, row, row, row],
        out_specs=[tab, tab, tab, tab],
        out_shape=[jax.ShapeDtypeStruct((n, LANES), F32)] * 4,
        compiler_params=_params("parallel"),
        name="rope_tables",
    )(pos, f128, g128, f64, g64)


def _ada_kernel(c_ref, w_ref, b_ref, o_ref):
    c = c_ref[...]
    act = (c * jax.nn.sigmoid(c)).astype(BF16)
    o_ref[...] = jnp.dot(act, w_ref[...].astype(BF16), preferred_element_type=F32) + b_ref[...]


def _ada(c, w_ada, b_ada):
    depth, d, n6 = w_ada.shape
    b = c.shape[0]
    rows = 8
    c_pad = jnp.zeros((rows, d), F32).at[:b].set(c)
    tn = _pick(n6, 1024)
    out = pl.pallas_call(
        _ada_kernel,
        grid=(depth, n6 // tn),
        in_specs=[
            pl.BlockSpec((rows, d), lambda l, j: (0, 0)),
            pl.BlockSpec((None, d, tn), lambda l, j: (l, 0, j)),
            pl.BlockSpec((None, 1, tn), lambda l, j: (l, 0, j)),
        ],
        out_specs=pl.BlockSpec((None, rows, tn), lambda l, j: (l, 0, j)),
        out_shape=jax.ShapeDtypeStruct((depth, rows, n6), F32),
        compiler_params=_params("parallel", "parallel"),
        name="adaln",
    )(c_pad, w_ada, b_ada.reshape(depth, 1, n6))
    return out[:, :b].reshape(depth, b, N_MOD, 1, d)


def _mod_spec(d, which, tiles_per_batch):
    return pl.BlockSpec((None, None, 1, d), lambda i, *_: (i // tiles_per_batch, which, 0, 0))


def _normmod_kernel(x_ref, sc_ref, sh_ref, o_ref):
    x = x_ref[...]
    y = _rms(x, x.shape[-1])
    o_ref[...] = (y * (1.0 + sc_ref[...]) + sh_ref[...]).astype(o_ref.dtype)


def _normmod(x, mod, seq, which_scale, which_shift):
    n, d = x.shape
    tm = _pick(seq, 512)
    tpb = seq // tm
    return pl.pallas_call(
        _normmod_kernel,
        grid=(n // tm,),
        in_specs=[pl.BlockSpec((tm, d), lambda i: (i, 0)), _mod_spec(d, which_scale, tpb), _mod_spec(d, which_shift, tpb)],
        out_specs=pl.BlockSpec((tm, d), lambda i: (i, 0)),
        out_shape=jax.ShapeDtypeStruct((n, d), BF16),
        compiler_params=_params("parallel"),
        name="normmod",
    )(x, mod, mod)


def _proj_kernel(h_ref, w_ref, g_ref, c128_ref, s128_ref, c64_ref, s64_ref, o_ref, z_ref, *, epilogues, streams):
    h = h_ref[...]
    per = h.shape[0] // streams
    for t, kind in enumerate(epilogues):
        z = jnp.dot(h, w_ref[:, t * MXU_COLS:(t + 1) * MXU_COLS], preferred_element_type=F32)
        for c in range(MXU_COLS // LANES):
            slab = t * (MXU_COLS // LANES) + c
            cols = slice(slab * LANES, (slab + 1) * LANES)
            y = z[:, c * LANES:(c + 1) * LANES]
            if kind == EPI_QK:
                y = _rms(y, HEAD_DIM) * g_ref[:, cols]
                y = y * c128_ref[...] + pltpu.roll(y, HEAD_DIM // 2, 1) * s128_ref[...]
            elif kind == EPI_ROPE64:
                y = y * c64_ref[...] + _swap_half64(y) * s64_ref[...]
            if streams == 1:
                o_ref[..., cols] = y.astype(o_ref.dtype).reshape(o_ref.shape[:-1] + (LANES,))
            else:
                z_ref[slab] = y
                for p in range(streams):
                    o_ref[p, :, cols] = z_ref[slab, pl.ds(p, per, stride=streams), :].astype(o_ref.dtype)


def _proj_call(h, w, gain_cols, tabs, epilogues, out_spec, out_shape, tm, streams, name):
    d, width = w.shape
    assert width == len(epilogues) * MXU_COLS
    tab = pl.BlockSpec((tm, LANES), lambda i: (i, 0))
    slabs = width // LANES if streams > 1 else 1
    return pl.pallas_call(
        functools.partial(_proj_kernel, epilogues=epilogues, streams=streams),
        grid=(h.shape[0] // tm,),
        in_specs=[
            pl.BlockSpec((tm, d), lambda i: (i, 0)),
            pl.BlockSpec((d, width), lambda i: (0, 0)),
            pl.BlockSpec((1, width), lambda i: (0, 0)),
            tab, tab, tab, tab,
        ],
        out_specs=out_spec,
        out_shape=out_shape,
        scratch_shapes=[pltpu.VMEM((slabs, tm, LANES), F32)],
        compiler_params=_params("parallel"),
        name=name,
    )(h, w, gain_cols, *tabs)


def _bproj(h, w, gain_cols, tabs, epilogues, name):
    n = h.shape[0]
    tm = _pick(n, 512)
    width = w.shape[1]
    return _proj_call(h, w, gain_cols, tabs, epilogues, pl.BlockSpec((tm, width), lambda i: (i, 0)),
                      jax.ShapeDtypeStruct((n, width), BF16), tm, 1, name)


def _aproj(h, w_group, gain_cols, tabs, batch, seq, dilation):
    r = dilation
    tm = _pick(seq, 512)
    tpb = seq // tm
    assert tm % (r * 16) == 0
    return _proj_call(h, w_group, gain_cols, tabs, A_EPILOGUES,
                      pl.BlockSpec((None, r, tm // r, A_PACK_WIDTH), lambda i: (i // tpb, 0, i % tpb, 0)),
                      jax.ShapeDtypeStruct((batch, r, seq // r, A_PACK_WIDTH), BF16), tm, r, f"aproj_r{r}")


def _idxproj_kernel(h_ref, w_ref, g_ref, c64_ref, s64_ref, ik_ref, iw_ref):
    z = jnp.dot(h_ref[...], w_ref[...], preferred_element_type=F32)
    lane = lax.broadcasted_iota(jnp.int32, z.shape, 1)
    is_k = lane < IDX_DIM
    zk = jnp.where(is_k, z, 0.0)
    y = _rms(zk, IDX_DIM) * g_ref[...]
    y = y * c64_ref[...] + _swap_half64(y) * s64_ref[...]
    ik_ref[...] = y[:, :IDX_DIM].astype(ik_ref.dtype)
    iw_ref[...] = z.T[IDX_DIM:IDX_DIM + IDX_HEADS, :] * (IDX_HEADS ** -0.5 * IDX_DIM ** -0.5)


def _idxproj(h, w_idx, gain_row, c64, s64):
    n, d = h.shape
    tm = _pick(n, 512)
    tab = pl.BlockSpec((tm, LANES), lambda i: (i, 0))
    return pl.pallas_call(
        _idxproj_kernel,
        grid=(n // tm,),
        in_specs=[
            pl.BlockSpec((tm, d), lambda i: (i, 0)),
            pl.BlockSpec((d, LANES), lambda i: (0, 0)),
            pl.BlockSpec((1, LANES), lambda i: (0, 0)),
            tab, tab,
        ],
        out_specs=[pl.BlockSpec((tm, IDX_DIM), lambda i: (i, 0)), pl.BlockSpec((IDX_HEADS, tm), lambda i: (0, i))],
        out_shape=[jax.ShapeDtypeStruct((n, IDX_DIM), BF16), jax.ShapeDtypeStruct((IDX_HEADS, n), F32)],
        compiler_params=_params("parallel"),
        name="idxproj",
    )(h, w_idx, gain_row, c64, s64)


def _dilated_kernel(q_ref, kc_ref, kp_ref, vc_ref, vp_ref, o_ref, lse_ref, *, tq):
    i = pl.program_id(2)
    blk = LANES
    scale = HEAD_DIM ** -0.5
    row = lax.broadcasted_iota(jnp.int32, (blk, blk), 0)
    col = lax.broadcasted_iota(jnp.int32, (blk, blk), 1)
    cur_ok = col <= row
    prev_ok = col >= row
    lane = lax.broadcasted_iota(jnp.int32, (blk, LANES), 1)
    nt = (((1,), (1,)), ((), ()))
    for j in range(tq // blk):
        rows = slice(j * blk, (j + 1) * blk)
        lse_tile = jnp.zeros((blk, LANES), F32)
        for hh in range(A_HEADS_PER_GROUP):
            cols = slice(hh * HEAD_DIM, (hh + 1) * HEAD_DIM)
            q = q_ref[rows, cols]
            kc = kc_ref[rows, cols]
            vc = vc_ref[rows, cols]
            if j == 0:
                kp, vp = kp_ref[:, cols], vp_ref[:, cols]
                p_ok = prev_ok & (i > 0)
            else:
                prows = slice((j - 1) * blk, j * blk)
                kp, vp = kc_ref[prows, cols], vc_ref[prows, cols]
                p_ok = prev_ok
            s_c = lax.dot_general(q, kc, nt, preferred_element_type=F32) * scale
            s_p = lax.dot_general(q, kp, nt, preferred_element_type=F32) * scale
            s_c = jnp.where(cur_ok, s_c, -jnp.inf)
            s_p = jnp.where(p_ok, s_p, -jnp.inf)
            m = jnp.maximum(jnp.max(s_c, axis=1, keepdims=True), jnp.max(s_p, axis=1, keepdims=True))
            e_c = jnp.exp(s_c - m)
            e_p = jnp.exp(s_p - m)
            den = jnp.sum(e_c, axis=1, keepdims=True) + jnp.sum(e_p, axis=1, keepdims=True)
            acc = jnp.dot(e_c.astype(BF16), vc, preferred_element_type=F32)
            acc = acc + jnp.dot(e_p.astype(BF16), vp, preferred_element_type=F32)
            o_ref[rows, cols] = acc / den
            lse_tile = jnp.where(lane == hh, m + jnp.log(den), lse_tile)
        lse_ref[rows, :] = lse_tile


def _dilated(qkv, window):
    b, r, m, _ = qkv.shape
    assert window // r == LANES and m % LANES == 0
    tq = _pick(m, 512)
    nsub = tq // LANES

    def cur(tile):
        return pl.BlockSpec((None, None, tq, COL_TILE), lambda bb, p, i: (bb, p, i, tile))

    def prev(tile):
        return pl.BlockSpec((None, None, LANES, COL_TILE),
                            lambda bb, p, i: (bb, p, jnp.maximum(i * nsub - 1, 0), tile))

    return pl.pallas_call(
        functools.partial(_dilated_kernel, tq=tq),
        grid=(b, r, m // tq),
        in_specs=[cur(0), cur(1), prev(1), cur(2), prev(2)],
        out_specs=[
            pl.BlockSpec((None, None, tq, A_GROUP_WIDTH), lambda bb, p, i: (bb, p, i, 0)),
            pl.BlockSpec((None, None, tq, LANES), lambda bb, p, i: (bb, p, i, 0)),
        ],
        out_shape=[jax.ShapeDtypeStruct((b, r, m, A_GROUP_WIDTH), F32), jax.ShapeDtypeStruct((b, r, m, LANES), F32)],
        compiler_params=_params("parallel", "parallel", "parallel"),
        name=f"dilated_r{r}",
    )(qkv, qkv, qkv, qkv, qkv)


def _key_to_float(key):
    bits = jnp.where(key >= 0, key, key ^ 0x7FFFFFFF)
    return lax.bitcast_convert_type(bits, F32)


def _float_to_key(x):
    bits = lax.bitcast_convert_type(x, jnp.int32)
    return jnp.where(bits >= 0, bits, bits ^ 0x7FFFFFFF)


PACK16 = 16


def _select_kernel(iq_ref, wt_ref, k_ref, bias_ref, qt_ref, sc_ref, hi_ref, lo_ref, *, topk):
    i = pl.program_id(1)
    tq, tk = SEL_TQ, SEL_TK
    nk = sc_ref.shape[0]
    nkb = (i * tq + tq + tk - 1) // tk
    slabs = tk // PACK16

    q_t = iq_ref[...].astype(F32).T
    for h in range(IDX_HEADS):
        qt_ref[:, h * tq:(h + 1) * tq] = q_t[h * IDX_DIM:(h + 1) * IDX_DIM, :].astype(qt_ref.dtype)

    kpos = lax.broadcasted_iota(jnp.int32, (tk, tq), 0)
    qpos = i * tq + lax.broadcasted_iota(jnp.int32, (tk, tq), 1)

    def rows(x):
        return jnp.concatenate([x] * slabs, axis=0)

    def score_block(kb, carry):
        keys = k_ref[pl.ds(pl.multiple_of(kb * tk, tk), tk), :]
        logits = jnp.dot(keys, qt_ref[...], preferred_element_type=F32)
        acc = jnp.zeros((tk, tq), F32)
        for h in range(IDX_HEADS):
            acc = acc + wt_ref[h:h + 1, :] * jnp.maximum(logits[:, h * tq:(h + 1) * tq], 0.0)
        masked = jnp.where(kpos + kb * tk <= qpos, acc, -jnp.inf)
        sc_ref[kb] = masked
        hi_ref[kb] = (_float_to_key(masked) >> 16).astype(jnp.int16)
        return carry

    lax.fori_loop(0, nkb, score_block, 0)

    def bisect16(ref, need):
        def count_ge(t):
            t16 = t.astype(jnp.int16)

            def body(kb, accs):
                accs = list(accs)
                for r in range(slabs):
                    hit = jnp.where(ref[kb, r * PACK16:(r + 1) * PACK16, :] >= t16, jnp.int16(1), jnp.int16(0))
                    accs[r % len(accs)] = accs[r % len(accs)] + hit
                return tuple(accs)

            zero = jnp.zeros((PACK16, tq), jnp.int16)
            accs = lax.fori_loop(0, nkb, body, (zero,) * 4)
            acc = (accs[0] + accs[1]) + (accs[2] + accs[3])
            cnt = jnp.sum(acc.astype(jnp.int32), axis=0, keepdims=True)
            return jnp.broadcast_to(cnt, (PACK16, tq))

        def step(_, carry):
            lo, hi, above = carry
            mid = (lo + hi) >> 1
            cnt = count_ge(mid)
            ge = cnt >= need
            return jnp.where(ge, mid, lo), jnp.where(ge, hi, mid), jnp.where(ge, above, cnt)

        lo0 = jnp.full((PACK16, tq), -(2 ** 15), jnp.int32)
        hi0 = jnp.full((PACK16, tq), 2 ** 15, jnp.int32)
        lo, _, above = lax.fori_loop(0, 16, step, (lo0, hi0, jnp.zeros((PACK16, tq), jnp.int32)))
        return lo, above

    key_hi, above = bisect16(hi_ref, topk)
    key_hi_rows = rows(key_hi)

    def low_digits(kb, carry):
        key = _float_to_key(sc_ref[kb])
        low = (key & 0xFFFF) - 2 ** 15
        lo_ref[kb] = jnp.where((key >> 16) == key_hi_rows, low, -(2 ** 15)).astype(jnp.int16)
        return carry

    lax.fori_loop(0, nkb, low_digits, 0)
    key_lo, _ = bisect16(lo_ref, topk - above)
    thr_rows = rows(_key_to_float((key_hi << 16) | (key_lo + 2 ** 15)))

    def write_block(kb, carry):
        picked = jnp.where(sc_ref[kb] >= thr_rows, 0.0, NEG)
        picked = jnp.where(kpos + kb * tk <= qpos, picked, NEG)
        bias_ref[kb] = picked.T.astype(bias_ref.dtype)
        return carry

    lax.fori_loop(0, nkb, write_block, 0)

    def fill_block(kb, carry):
        bias_ref[kb] = jnp.full((tq, tk), NEG, bias_ref.dtype)
        return carry

    lax.fori_loop(nkb, nk, fill_block, 0)


def _select(z_iv, iw_t, ik, topk):
    b, s, _ = z_iv.shape
    tq, tk = SEL_TQ, SEL_TK
    nq, nk = s // tq, s // tk
    return pl.pallas_call(
        functools.partial(_select_kernel, topk=topk),
        grid=(b, nq),
        in_specs=[
            pl.BlockSpec((None, tq, IDX_WIDTH), lambda bb, i: (bb, i, 0)),
            pl.BlockSpec((IDX_HEADS, tq), lambda bb, i: (0, bb * nq + i)),
            pl.BlockSpec((None, s, IDX_DIM), lambda bb, i: (bb, 0, 0)),
        ],
        out_specs=pl.BlockSpec((None, None, nk, tq, tk), lambda bb, i: (bb, i, 0, 0, 0)),
        out_shape=jax.ShapeDtypeStruct((b, nq, nk, tq, tk), BF16),
        scratch_shapes=[
            pltpu.VMEM((IDX_DIM, IDX_HEADS * tq), BF16),
            pltpu.VMEM((nk, tk, tq), F32),
            pltpu.VMEM((nk, tk, tq), jnp.int16),
            pltpu.VMEM((nk, tk, tq), jnp.int16),
        ],
        compiler_params=_params("parallel", "parallel"),
        name="dsa_select",
    )(z_iv, iw_t, ik)


def _dsa_kernel(q_ref, k_ref, v_ref, b_ref, o_ref, m_ref, l_ref, acc_ref, *, tq, tk):
    i = pl.program_id(1)
    kb = pl.program_id(2)
    last = (i * tq + tq - 1) // tk
    nt = (((1,), (1,)), ((), ()))

    @pl.when(kb == 0)
    def _():
        m_ref[...] = jnp.full(m_ref.shape, NEG, F32)
        l_ref[...] = jnp.zeros(l_ref.shape, F32)
        acc_ref[...] = jnp.zeros(acc_ref.shape, F32)

    @pl.when(kb <= last)
    def _():
        bias = b_ref[...].reshape(tq, tk).astype(F32)
        rep = tk // LANES

        def scores(h):
            cols = slice(h * HEAD_DIM, (h + 1) * HEAD_DIM)
            return lax.dot_general(q_ref[:, cols], k_ref[:, cols], nt, preferred_element_type=F32) + bias

        s_next = scores(0)
        for h in range(B_HEADS):
            cols = slice(h * HEAD_DIM, (h + 1) * HEAD_DIM)
            s = s_next
            if h + 1 < B_HEADS:
                s_next = scores(h + 1)
            m_prev = m_ref[h]
            m_new = jnp.maximum(m_prev, jnp.max(s, axis=1, keepdims=True))
            alpha = jnp.exp2(m_prev - m_new)
            p = jnp.exp2(s - jnp.concatenate([m_new] * rep, axis=1))
            l_ref[h] = alpha * l_ref[h] + jnp.sum(p, axis=1, keepdims=True)
            acc_ref[:, cols] = alpha * acc_ref[:, cols] + jnp.dot(p.astype(BF16), v_ref[:, cols],
                                                                  preferred_element_type=F32)
            m_ref[h] = m_new

    @pl.when(kb == last)
    def _():
        for h in range(B_HEADS):
            cols = slice(h * HEAD_DIM, (h + 1) * HEAD_DIM)
            o_ref[:, cols] = (acc_ref[:, cols] / l_ref[h]).astype(o_ref.dtype)


def _dsa_attention(z_qk, z_iv, bias5):
    b, s, _ = z_qk.shape
    tk = SEL_TK
    tq = _pick(s, 512)
    sub = tq // SEL_TQ
    assert IDX_WIDTH == B_WIDTH

    def last(i):
        return (i * tq + tq - 1) // tk

    return pl.pallas_call(
        functools.partial(_dsa_kernel, tq=tq, tk=tk),
        grid=(b, s // tq, s // tk),
        in_specs=[
            pl.BlockSpec((None, tq, B_WIDTH), lambda bb, i, kb: (bb, i, 0)),
            pl.BlockSpec((None, tk, B_WIDTH), lambda bb, i, kb: (bb, jnp.minimum(kb, last(i)), 1)),
            pl.BlockSpec((None, tk, B_WIDTH), lambda bb, i, kb: (bb, jnp.minimum(kb, last(i)), 1)),
            pl.BlockSpec((None, sub, None, SEL_TQ, tk), lambda bb, i, kb: (bb, i, jnp.minimum(kb, last(i)), 0, 0)),
        ],
        out_specs=pl.BlockSpec((None, tq, B_WIDTH), lambda bb, i, kb: (bb, i, 0)),
        out_shape=jax.ShapeDtypeStruct((b, s, B_WIDTH), BF16),
        scratch_shapes=[
            pltpu.VMEM((B_HEADS, tq, LANES), F32),
            pltpu.VMEM((B_HEADS, tq, LANES), F32),
            pltpu.VMEM((tq, B_WIDTH), F32),
        ],
        compiler_params=_params("parallel", "parallel", "arbitrary"),
        name="dsa_attention",
    )(z_qk, z_qk, z_iv, bias5)


def _to_token_order(dst_ref, first, src_ref):
    r, per, width = src_ref.shape
    for c in range(width // LANES):
        cols = slice(c * LANES, (c + 1) * LANES)
        if r == 1:
            dst_ref[first + c] = src_ref[0, :, cols]
        else:
            for p in range(r):
                dst_ref[first + c, pl.ds(p, per, stride=r), :] = src_ref[p, :, cols]


def _merge_kernel(o1_ref, o2_ref, o3_ref, l1_ref, l2_ref, l3_ref, ob_ref, h_ref,
                  wga_ref, wgb_ref, bga_ref, bgb_ref, wpa_ref, wpb_ref, out_ref, oa_ref, ot_ref, lt_ref):
    @pl.when(pl.program_id(1) == 0)
    def _():
        nh = A_HEADS_PER_GROUP
        for g, (o_ref, l_ref) in enumerate(((o1_ref, l1_ref), (o2_ref, l2_ref), (o3_ref, l3_ref))):
            _to_token_order(ot_ref, g * nh, o_ref)
            _to_token_order(lt_ref, g, l_ref)
        l1, l2, l3 = lt_ref[0], lt_ref[1], lt_ref[2]
        mx = jnp.maximum(jnp.maximum(l1, l2), l3)
        e1, e2, e3 = jnp.exp(l1 - mx), jnp.exp(l2 - mx), jnp.exp(l3 - mx)
        tot = e1 + e2 + e3
        w1, w2, w3 = e1 / tot, e2 / tot, e3 / tot
        for hh in range(nh):
            oa = (w1[:, hh:hh + 1] * ot_ref[hh] + w2[:, hh:hh + 1] * ot_ref[nh + hh]
                  + w3[:, hh:hh + 1] * ot_ref[2 * nh + hh])
            oa_ref[:, hh * HEAD_DIM:(hh + 1) * HEAD_DIM] = oa.astype(oa_ref.dtype)

    h = h_ref[...]
    ga = jax.nn.sigmoid(jnp.dot(h, wga_ref[...], preferred_element_type=F32) + bga_ref[...])
    gb = jax.nn.sigmoid(jnp.dot(h, wgb_ref[...], preferred_element_type=F32) + bgb_ref[...])
    pa = jnp.dot(oa_ref[...], wpa_ref[...], preferred_element_type=F32)
    pb = jnp.dot(ob_ref[...], wpb_ref[...], preferred_element_type=F32)
    out_ref[...] = (ga * pa + gb * pb).astype(out_ref.dtype)


def _merge(outs, lses, o_b, h, w_gate, b_gate, w_proj_a, w_proj_b, seq):
    n, d = h.shape
    tm = _pick(seq, 512)
    tpb = seq // tm
    tn = _pick(d, COL_TILE)
    nj = d // tn
    row = lambda width: pl.BlockSpec((tm, width), lambda i, j: (i, 0))

    def streams(arr):
        r, width = arr.shape[1], arr.shape[3]
        return pl.BlockSpec((None, r, tm // r, width), lambda i, j: (i // tpb, 0, i % tpb, 0))

    return pl.pallas_call(
        _merge_kernel,
        grid=(n // tm, nj),
        in_specs=[
            *[streams(a) for a in outs], *[streams(a) for a in lses],
            row(B_WIDTH), row(d),
            pl.BlockSpec((d, tn), lambda i, j: (0, j)),
            pl.BlockSpec((d, tn), lambda i, j: (0, nj + j)),
            pl.BlockSpec((1, tn), lambda i, j: (0, j)),
            pl.BlockSpec((1, tn), lambda i, j: (0, nj + j)),
            pl.BlockSpec((A_GROUP_WIDTH, tn), lambda i, j: (0, j)),
            pl.BlockSpec((B_WIDTH, tn), lambda i, j: (0, j)),
        ],
        out_specs=pl.BlockSpec((tm, tn), lambda i, j: (i, j)),
        out_shape=jax.ShapeDtypeStruct((n, d), BF16),
        scratch_shapes=[
            pltpu.VMEM((tm, A_GROUP_WIDTH), BF16),
            pltpu.VMEM((len(outs) * A_HEADS_PER_GROUP, tm, HEAD_DIM), F32),
            pltpu.VMEM((len(lses), tm, LANES), F32),
        ],
        compiler_params=_params("parallel", "arbitrary"),
        name="gated_merge",
    )(*outs, *lses, o_b, h, w_gate, w_gate, b_gate, b_gate, w_proj_a, w_proj_b)


def _outproj_kernel(mg_ref, w_ref, x_ref, g_ref, sc_ref, sh_ref, xo_ref, ho_ref):
    mix = jnp.dot(mg_ref[...], w_ref[...], preferred_element_type=F32)
    x = x_ref[...] + g_ref[...] * mix
    xo_ref[...] = x
    ho_ref[...] = (_rms(x, x.shape[-1]) * (1.0 + sc_ref[...]) + sh_ref[...]).astype(ho_ref.dtype)


def _outproj(merged, w_out, x, mod, seq):
    n, d = x.shape
    tm = _pick(seq, 256)
    tpb = seq // tm
    row = pl.BlockSpec((tm, d), lambda i: (i, 0))
    return pl.pallas_call(
        _outproj_kernel,
        grid=(n // tm,),
        in_specs=[row, pl.BlockSpec((d, d), lambda i: (0, 0)), row,
                  _mod_spec(d, 2, tpb), _mod_spec(d, 4, tpb), _mod_spec(d, 3, tpb)],
        out_specs=[row, row],
        out_shape=[jax.ShapeDtypeStruct((n, d), F32), jax.ShapeDtypeStruct((n, d), BF16)],
        compiler_params=_params("parallel"),
        name="outproj",
    )(merged, w_out, x, mod, mod, mod)


def _ffn_kernel(h_ref, wu_ref, wd_ref, x_ref, g_ref, o_ref, acc_ref):
    c = pl.program_id(1)

    @pl.when(c == 0)
    def _():
        acc_ref[...] = jnp.zeros(acc_ref.shape, F32)

    u = jnp.maximum(jnp.dot(h_ref[...], wu_ref[...], preferred_element_type=F32), 0.0)
    acc_ref[...] += jnp.dot((u * u).astype(BF16), wd_ref[...], preferred_element_type=F32)

    @pl.when(c == pl.num_programs(1) - 1)
    def _():
        o_ref[...] = x_ref[...] + g_ref[...] * acc_ref[...]


def _ffn(h2, w_up, w_down, x, mod, seq):
    n, d = x.shape
    hidden = w_up.shape[1]
    tm = _pick(seq, 512)
    tc = _pick(hidden, 512)
    tpb = seq // tm
    row = pl.BlockSpec((tm, d), lambda i, c: (i, 0))
    return pl.pallas_call(
        _ffn_kernel,
        grid=(n // tm, hidden // tc),
        in_specs=[row, pl.BlockSpec((d, tc), lambda i, c: (0, c)), pl.BlockSpec((tc, d), lambda i, c: (c, 0)),
                  row, _mod_spec(d, 5, tpb)],
        out_specs=row,
        out_shape=jax.ShapeDtypeStruct((n, d), F32),
        scratch_shapes=[pltpu.VMEM((tm, d), F32)],
        compiler_params=_params("parallel", "arbitrary"),
        name="ffn",
    )(h2, w_up, w_down, x, mod)


def _pack_in_weights(w_in, a_q_gain, a_k_gain, b_q_gain, b_k_gain, idx_k_gain):
    d = w_in.shape[0]
    sizes = (A_WIDTH, A_WIDTH, A_WIDTH, B_WIDTH, B_WIDTH, B_WIDTH, IDX_WIDTH, IDX_DIM, IDX_HEADS)
    parts, off = [], 0
    for sz in sizes:
        parts.append(w_in[:, off:off + sz])
        off += sz
    aq, ak, av, bq, bk, bv, iq, ik, iw = parts
    w_iv = jnp.concatenate([iq, bv], axis=1).astype(BF16)
    w_qk = jnp.concatenate([bq, bk], axis=1).astype(BF16)
    w_idx = jnp.concatenate([ik, iw, jnp.zeros((d, LANES - IDX_DIM - IDX_HEADS), w_in.dtype)], axis=1).astype(BF16)
    ones = lambda width: jnp.ones((width,), F32)
    iv_gain_cols = ones(IDX_WIDTH + B_WIDTH).reshape(1, -1)
    qk_gain_cols = jnp.concatenate([jnp.tile(b_q_gain * DSA_Q_SCALE, B_HEADS), jnp.tile(b_k_gain, B_HEADS)]).reshape(1, -1)
    idx_gain_row = jnp.concatenate([idx_k_gain, ones(LANES - IDX_DIM)]).reshape(1, LANES)
    w_groups = []
    for g in range(len(A_GROUPS)):
        sl = slice(g * A_GROUP_WIDTH, (g + 1) * A_GROUP_WIDTH)
        w_groups.append(jnp.concatenate([aq[:, sl], ak[:, sl], av[:, sl]], axis=1).astype(BF16))
    a_gain_cols = jnp.concatenate([
        jnp.tile(a_q_gain, A_HEADS_PER_GROUP), jnp.tile(a_k_gain, A_HEADS_PER_GROUP), ones(A_GROUP_WIDTH),
    ]).reshape(1, A_PACK_WIDTH)
    return w_iv, iv_gain_cols, w_qk, qk_gain_cols, w_idx, idx_gain_row, w_groups, a_gain_cols


def kernel(x, c, positions, w_ada, b_ada, w_in, a_q_gain, a_k_gain, b_q_gain, b_k_gain, idx_k_gain,
           w_gate, b_gate, w_proj_a, w_proj_b, w_out, w_up, w_down):
    b, s, d = x.shape
    depth = w_ada.shape[0]
    n = b * s
    topk = min(IDX_TOPK, s // 4)
    assert s % SEL_TK == 0 and SEL_TK >= topk and d % COL_TILE == 0 and s // LANES < 2 ** 15

    tabs = _rope_tables(positions)
    c128, s128, c64, s64 = tabs
    mods = _ada(c, w_ada, b_ada)
    xf = x.reshape(n, d)

    for l in range(depth):
        mod = mods[l]
        w_iv, iv_gain_cols, w_qk, qk_gain_cols, w_idx, idx_gain_row, w_groups, a_gain_cols = _pack_in_weights(
            w_in[l], a_q_gain[l], a_k_gain[l], b_q_gain[l], b_k_gain[l], idx_k_gain[l])

        h = _normmod(xf, mod, s, 1, 0)
        z_iv = _bproj(h, w_iv, iv_gain_cols, tabs, IV_EPILOGUES, "proj_iq_bv").reshape(b, s, -1)
        z_qk = _bproj(h, w_qk, qk_gain_cols, tabs, QK_EPILOGUES, "proj_bq_bk").reshape(b, s, -1)
        ik, iw = _idxproj(h, w_idx, idx_gain_row, c64, s64)

        a_outs, a_lses = [], []
        for g, (window, dilation) in enumerate(A_GROUPS):
            qkv = _aproj(h, w_groups[g], a_gain_cols, tabs, b, s, dilation)
            o, lse = _dilated(qkv, window)
            a_outs.append(o)
            a_lses.append(lse)

        bias5 = _select(z_iv, iw, ik.reshape(b, s, IDX_DIM), topk)
        o_b = _dsa_attention(z_qk, z_iv, bias5).reshape(n, B_WIDTH)

        merged = _merge(a_outs, a_lses, o_b, h, w_gate[l].astype(BF16), b_gate[l].reshape(1, 2 * d),
                        w_proj_a[l].astype(BF16), w_proj_b[l].astype(BF16), s)
        xf, h2 = _outproj(merged, w_out[l].astype(BF16), xf, mod, s)
        xf = _ffn(h2, w_up[l].astype(BF16), w_down[l].astype(BF16), xf, mod, s)

    return xf.reshape(b, s, d)
```

```python
import functools

import jax
import jax.numpy as jnp
from jax import lax
from jax.experimental import pallas as pl
from jax.experimental.pallas import tpu as pltpu

F32 = jnp.float32
BF16 = jnp.bfloat16

HEAD_DIM = 128
LANES = 128
A_GROUPS = ((128, 1), (512, 4), (2048, 16))
A_HEADS_PER_GROUP = 4
A_GROUP_WIDTH = A_HEADS_PER_GROUP * HEAD_DIM
A_WIDTH = len(A_GROUPS) * A_GROUP_WIDTH
B_HEADS = 8
B_WIDTH = B_HEADS * HEAD_DIM
IDX_HEADS = 16
IDX_DIM = 64
IDX_WIDTH = IDX_HEADS * IDX_DIM
IDX_TOPK = 256
ROPE_THETA = 10000.0
EPS = 1e-6
N_MOD = 6
NEG = -1e30
LOG2_E = 1.4426950408889634
DSA_Q_SCALE = HEAD_DIM ** -0.5 * LOG2_E

COL_TILE = 512
MXU_COLS = 256
EPI_ROPE64, EPI_QK, EPI_PLAIN = 0, 1, 2
IV_EPILOGUES = (EPI_ROPE64,) * (IDX_WIDTH // MXU_COLS) + (EPI_PLAIN,) * (B_WIDTH // MXU_COLS)
QK_EPILOGUES = (EPI_QK,) * (2 * B_WIDTH // MXU_COLS)
A_EPILOGUES = (EPI_QK,) * (2 * A_GROUP_WIDTH // MXU_COLS) + (EPI_PLAIN,) * (A_GROUP_WIDTH // MXU_COLS)
A_PACK_WIDTH = 3 * A_GROUP_WIDTH

SEL_TQ = 128
SEL_TK = 512
SEL_GROUP = 4
VMEM_LIMIT = 52 * 1024 * 1024


def _params(*sem):
    return pltpu.CompilerParams(dimension_semantics=sem, vmem_limit_bytes=VMEM_LIMIT)


def _pick(n, pref):
    t = pref
    while n % t:
        t //= 2
    return t


def _rms(x, width):
    return x * lax.rsqrt(jnp.sum(x * x, axis=-1, keepdims=True) * (1.0 / width) + EPS)


def _swap_half64(y):
    lane = lax.broadcasted_iota(jnp.int32, y.shape, 1)
    return jnp.where((lane & 63) < 32, pltpu.roll(y, 96, 1), pltpu.roll(y, 32, 1))


def _rope_tables_kernel(pos_ref, f128_ref, g128_ref, f64_ref, g64_ref, c128_ref, s128_ref, c64_ref, s64_ref):
    pos = pos_ref[...]
    a = pos * f128_ref[...]
    c128_ref[...] = jnp.cos(a)
    s128_ref[...] = jnp.sin(a) * g128_ref[...]
    a = pos * f64_ref[...]
    c64_ref[...] = jnp.cos(a)
    s64_ref[...] = jnp.sin(a) * g64_ref[...]


def _rope_tables(positions):
    n = positions.size
    pos = positions.reshape(n, 1).astype(F32)

    def freq(d):
        half = d // 2
        inv = jnp.power(ROPE_THETA, -jnp.arange(half, dtype=F32) * 2.0 / d)
        f = jnp.tile(jnp.concatenate([inv, inv]), LANES // d)
        g = jnp.tile(jnp.concatenate([-jnp.ones((half,), F32), jnp.ones((half,), F32)]), LANES // d)
        return f.reshape(1, LANES), g.reshape(1, LANES)

    f128, g128 = freq(HEAD_DIM)
    f64, g64 = freq(IDX_DIM)
    tm = _pick(n, 1024)
    row = pl.BlockSpec((1, LANES), lambda i: (0, 0))
    tab = pl.BlockSpec((tm, LANES), lambda i: (i, 0))
    return pl.pallas_call(
        _rope_tables_kernel,
        grid=(n // tm,),
        in_specs=[pl.BlockSpec((tm, 1), lambda i: (i, 0)), row, row, row, row],
        out_specs=[tab, tab, tab, tab],
        out_shape=[jax.ShapeDtypeStruct((n, LANES), F32)] * 4,
        compiler_params=_params("parallel"),
        name="rope_tables",
    )(pos, f128, g128, f64, g64)


def _ada_kernel(c_ref, w_ref, b_ref, o_ref):
    c = c_ref[...]
    act = (c * jax.nn.sigmoid(c)).astype(BF16)
    o_ref[...] = jnp.dot(act, w_ref[...].astype(BF16), preferred_element_type=F32) + b_ref[...]


def _ada(c, w_ada, b_ada):
    depth, d, n6 = w_ada.shape
    b = c.shape[0]
    rows = 8
    c_pad = jnp.zeros((rows, d), F32).at[:b].set(c)
    tn = _pick(n6, 1024)
    out = pl.pallas_call(
        _ada_kernel,
        grid=(depth, n6 // tn),
        in_specs=[
            pl.BlockSpec((rows, d), lambda l, j: (0, 0)),
            pl.BlockSpec((None, d, tn), lambda l, j: (l, 0, j)),
            pl.BlockSpec((None, 1, tn), lambda l, j: (l, 0, j)),
        ],
        out_specs=pl.BlockSpec((None, rows, tn), lambda l, j: (l, 0, j)),
        out_shape=jax.ShapeDtypeStruct((depth, rows, n6), F32),
        compiler_params=_params("parallel", "parallel"),
        name="adaln",
    )(c_pad, w_ada, b_ada.reshape(depth, 1, n6))
    return out[:, :b].reshape(depth, b, N_MOD, 1, d)


def _mod_spec(d, which, tiles_per_batch):
    return pl.BlockSpec((None, None, 1, d), lambda i, *_: (i // tiles_per_batch, which, 0, 0))


def _normmod_kernel(x_ref, sc_ref, sh_ref, o_ref):
    x = x_ref[...]
    y = _rms(x, x.shape[-1])
    o_ref[...] = (y * (1.0 + sc_ref[...]) + sh_ref[...]).astype(o_ref.dtype)


def _normmod(x, mod, seq, which_scale, which_shift):
    n, d = x.shape
    tm = _pick(seq, 512)
    tpb = seq // tm
    return pl.pallas_call(
        _normmod_kernel,
        grid=(n // tm,),
        in_specs=[pl.BlockSpec((tm, d), lambda i: (i, 0)), _mod_spec(d, which_scale, tpb), _mod_spec(d, which_shift, tpb)],
        out_specs=pl.BlockSpec((tm, d), lambda i: (i, 0)),
        out_shape=jax.ShapeDtypeStruct((n, d), BF16),
        compiler_params=_params("parallel"),
        name="normmod",
    )(x, mod, mod)


def _proj_kernel(h_ref, w_ref, g_ref, c128_ref, s128_ref, c64_ref, s64_ref, o_ref, z_ref, *, epilogues, streams):
    h = h_ref[...]
    per = h.shape[0] // streams
    for t, kind in enumerate(epilogues):
        z = jnp.dot(h, w_ref[:, t * MXU_COLS:(t + 1) * MXU_COLS], preferred_element_type=F32)
        for c in range(MXU_COLS // LANES):
            slab = t * (MXU_COLS // LANES) + c
            cols = slice(slab * LANES, (slab + 1) * LANES)
            y = z[:, c * LANES:(c + 1) * LANES]
            if kind == EPI_QK:
                y = _rms(y, HEAD_DIM) * g_ref[:, cols]
                y = y * c128_ref[...] + pltpu.roll(y, HEAD_DIM // 2, 1) * s128_ref[...]
            elif kind == EPI_ROPE64:
                y = y * c64_ref[...] + _swap_half64(y) * s64_ref[...]
            if streams == 1:
                o_ref[..., cols] = y.astype(o_ref.dtype).reshape(o_ref.shape[:-1] + (LANES,))
            else:
                z_ref[slab] = y
                for p in range(streams):
                    o_ref[p, :, cols] = z_ref[slab, pl.ds(p, per, stride=streams), :].astype(o_ref.dtype)


def _proj_call(h, w, gain_cols, tabs, epilogues, out_spec, out_shape, tm, streams, name):
    d, width = w.shape
    assert width == len(epilogues) * MXU_COLS
    tab = pl.BlockSpec((tm, LANES), lambda i: (i, 0))
    slabs = width // LANES if streams > 1 else 1
    return pl.pallas_call(
        functools.partial(_proj_kernel, epilogues=epilogues, streams=streams),
        grid=(h.shape[0] // tm,),
        in_specs=[
            pl.BlockSpec((tm, d), lambda i: (i, 0)),
            pl.BlockSpec((d, width), lambda i: (0, 0)),
            pl.BlockSpec((1, width), lambda i: (0, 0)),
            tab, tab, tab, tab,
        ],
        out_specs=out_spec,
        out_shape=out_shape,
        scratch_shapes=[pltpu.VMEM((slabs, tm, LANES), F32)],
        compiler_params=_params("parallel"),
        name=name,
    )(h, w, gain_cols, *tabs)


def _bproj(h, w, gain_cols, tabs, epilogues, name):
    n = h.shape[0]
    tm = _pick(n, 512)
    width = w.shape[1]
    return _proj_call(h, w, gain_cols, tabs, epilogues, pl.BlockSpec((tm, width), lambda i: (i, 0)),
                      jax.ShapeDtypeStruct((n, width), BF16), tm, 1, name)


def _aproj(h, w_group, gain_cols, tabs, batch, seq, dilation):
    r = dilation
    tm = _pick(seq, 512)
    tpb = seq // tm
    assert tm % (r * 16) == 0
    return _proj_call(h, w_group, gain_cols, tabs, A_EPILOGUES,
                      pl.BlockSpec((None, r, tm // r, A_PACK_WIDTH), lambda i: (i // tpb, 0, i % tpb, 0)),
                      jax.ShapeDtypeStruct((batch, r, seq // r, A_PACK_WIDTH), BF16), tm, r, f"aproj_r{r}")


def _idxproj_kernel(h_ref, w_ref, g_ref, c64_ref, s64_ref, ik_ref, iw_ref):
    z = jnp.dot(h_ref[...], w_ref[...], preferred_element_type=F32)
    lane = lax.broadcasted_iota(jnp.int32, z.shape, 1)
    is_k = lane < IDX_DIM
    zk = jnp.where(is_k, z, 0.0)
    y = _rms(zk, IDX_DIM) * g_ref[...]
    y = y * c64_ref[...] + _swap_half64(y) * s64_ref[...]
    ik_ref[...] = y[:, :IDX_DIM].astype(ik_ref.dtype)
    iw_ref[...] = z.T[IDX_DIM:IDX_DIM + IDX_HEADS, :] * (IDX_HEADS ** -0.5 * IDX_DIM ** -0.5)


def _idxproj(h, w_idx, gain_row, c64, s64):
    n, d = h.shape
    tm = _pick(n, 512)
    tab = pl.BlockSpec((tm, LANES), lambda i: (i, 0))
    return pl.pallas_call(
        _idxproj_kernel,
        grid=(n // tm,),
        in_specs=[
            pl.BlockSpec((tm, d), lambda i: (i, 0)),
            pl.BlockSpec((d, LANES), lambda i: (0, 0)),
            pl.BlockSpec((1, LANES), lambda i: (0, 0)),
            tab, tab,
        ],
        out_specs=[pl.BlockSpec((tm, IDX_DIM), lambda i: (i, 0)), pl.BlockSpec((IDX_HEADS, tm), lambda i: (0, i))],
        out_shape=[jax.ShapeDtypeStruct((n, IDX_DIM), BF16), jax.ShapeDtypeStruct((IDX_HEADS, n), F32)],
        compiler_params=_params("parallel"),
        name="idxproj",
    )(h, w_idx, gain_row, c64, s64)


def _dilated_kernel(q_ref, kc_ref, kp_ref, vc_ref, vp_ref, o_ref, lse_ref, *, tq):
    i = pl.program_id(2)
    blk = LANES
    scale = HEAD_DIM ** -0.5
    nkeys = blk + tq
    row = lax.broadcasted_iota(jnp.int32, (tq, nkeys), 0)
    col = lax.broadcasted_iota(jnp.int32, (tq, nkeys), 1)
    dist = row + blk - col
    band = jnp.where(dist >= 0, jnp.where(dist <= blk, 0.0, -jnp.inf), -jnp.inf)
    first = jnp.where(col >= blk, 0.0, -jnp.inf)
    bias = band + jnp.where(i > 0, 0.0, first)
    lane = lax.broadcasted_iota(jnp.int32, (tq, LANES), 1)
    nt = (((1,), (1,)), ((), ()))

    def scores(hh):
        cols = slice(hh * HEAD_DIM, (hh + 1) * HEAD_DIM)
        keys = jnp.concatenate([kp_ref[:, cols], kc_ref[:, cols]], axis=0)
        return lax.dot_general(q_ref[:, cols], keys, nt, preferred_element_type=F32) * scale + bias

    lse_tile = jnp.zeros((tq, LANES), F32)
    s_next = scores(0)
    for hh in range(A_HEADS_PER_GROUP):
        cols = slice(hh * HEAD_DIM, (hh + 1) * HEAD_DIM)
        s = s_next
        if hh + 1 < A_HEADS_PER_GROUP:
            s_next = scores(hh + 1)
        m = jnp.max(s, axis=1, keepdims=True)
        e = jnp.exp(s - m)
        den = jnp.sum(e, axis=1, keepdims=True)
        values = jnp.concatenate([vp_ref[:, cols], vc_ref[:, cols]], axis=0)
        acc = jnp.dot(e.astype(BF16), values, preferred_element_type=F32)
        o_ref[:, cols] = acc / den
        lse_tile = jnp.where(lane == hh, m + jnp.log(den), lse_tile)
    lse_ref[...] = lse_tile


def _dilated(qkv, window):
    b, r, m, _ = qkv.shape
    assert window // r == LANES and m % LANES == 0
    tq = _pick(m, 512)
    nsub = tq // LANES

    def cur(tile):
        return pl.BlockSpec((None, None, tq, COL_TILE), lambda bb, p, i: (bb, p, i, tile))

    def prev(tile):
        return pl.BlockSpec((None, None, LANES, COL_TILE),
                            lambda bb, p, i: (bb, p, jnp.maximum(i * nsub - 1, 0), tile))

    return pl.pallas_call(
        functools.partial(_dilated_kernel, tq=tq),
        grid=(b, r, m // tq),
        in_specs=[cur(0), cur(1), prev(1), cur(2), prev(2)],
        out_specs=[
            pl.BlockSpec((None, None, tq, A_GROUP_WIDTH), lambda bb, p, i: (bb, p, i, 0)),
            pl.BlockSpec((None, None, tq, LANES), lambda bb, p, i: (bb, p, i, 0)),
        ],
        out_shape=[jax.ShapeDtypeStruct((b, r, m, A_GROUP_WIDTH), F32), jax.ShapeDtypeStruct((b, r, m, LANES), F32)],
        compiler_params=_params("parallel", "parallel", "parallel"),
        name=f"dilated_r{r}",
    )(qkv, qkv, qkv, qkv, qkv)


def _key_to_float(key):
    bits = jnp.where(key >= 0, key, key ^ 0x7FFFFFFF)
    return lax.bitcast_convert_type(bits, F32)


def _float_to_key(x):
    bits = lax.bitcast_convert_type(x, jnp.int32)
    return jnp.where(bits >= 0, bits, bits ^ 0x7FFFFFFF)


PACK16 = 16


def _select_kernel(iq_ref, wt_ref, k_ref, bias_ref, qt_ref, sc_ref, hi_ref, lo_ref, *, topk):
    i = pl.program_id(1)
    tq, tk = SEL_TQ, SEL_TK
    nk = sc_ref.shape[0]
    nkb = (i * tq + tq + tk - 1) // tk
    slabs = tk // PACK16

    q_t = iq_ref[...].astype(F32).T
    for h in range(IDX_HEADS):
        qt_ref[:, h * tq:(h + 1) * tq] = q_t[h * IDX_DIM:(h + 1) * IDX_DIM, :].astype(qt_ref.dtype)

    kpos = lax.broadcasted_iota(jnp.int32, (tk, tq), 0)
    qpos = i * tq + lax.broadcasted_iota(jnp.int32, (tk, tq), 1)

    def rows(x):
        return jnp.concatenate([x] * slabs, axis=0)

    def score_block(kb, carry):
        keys = k_ref[pl.ds(pl.multiple_of(kb * tk, tk), tk), :]
        logits = jnp.dot(keys, qt_ref[...], preferred_element_type=F32)
        acc = jnp.zeros((tk, tq), F32)
        for h in range(IDX_HEADS):
            acc = acc + wt_ref[h:h + 1, :] * jnp.maximum(logits[:, h * tq:(h + 1) * tq], 0.0)
        masked = jnp.where(kpos + kb * tk <= qpos, acc, -jnp.inf)
        sc_ref[kb] = masked
        hi_ref[kb] = (_float_to_key(masked) >> 16).astype(jnp.int16)
        return carry

    lax.fori_loop(0, nkb, score_block, 0)

    ngr = (nkb + SEL_GROUP - 1) // SEL_GROUP
    nkp = ngr * SEL_GROUP

    def pad_block(kb, carry):
        neg_inf = jnp.full((tk, tq), -jnp.inf, F32)
        sc_ref[kb] = neg_inf
        hi_ref[kb] = (_float_to_key(neg_inf) >> 16).astype(jnp.int16)
        return carry

    lax.fori_loop(nkb, nkp, pad_block, 0)

    def bisect16(ref, need):
        def count_ge(t):
            t16 = t.astype(jnp.int16)

            def body(g, accs):
                accs = list(accs)
                for j in range(SEL_GROUP):
                    for r in range(slabs):
                        blk = ref[g * SEL_GROUP + j, r * PACK16:(r + 1) * PACK16, :]
                        hit = jnp.where(blk >= t16, jnp.int16(1), jnp.int16(0))
                        accs[r % len(accs)] = accs[r % len(accs)] + hit
                return tuple(accs)

            zero = jnp.zeros((PACK16, tq), jnp.int16)
            accs = lax.fori_loop(0, ngr, body, (zero,) * 4)
            acc = (accs[0] + accs[1]) + (accs[2] + accs[3])
            cnt = jnp.sum(acc.astype(jnp.int32), axis=0, keepdims=True)
            return jnp.broadcast_to(cnt, (PACK16, tq))

        def step(_, carry):
            lo, hi, above = carry
            mid = (lo + hi) >> 1
            cnt = count_ge(mid)
            ge = cnt >= need
            return jnp.where(ge, mid, lo), jnp.where(ge, hi, mid), jnp.where(ge, above, cnt)

        lo0 = jnp.full((PACK16, tq), -(2 ** 15), jnp.int32)
        hi0 = jnp.full((PACK16, tq), 2 ** 15, jnp.int32)
        lo, _, above = lax.fori_loop(0, 16, step, (lo0, hi0, jnp.zeros((PACK16, tq), jnp.int32)))
        return lo, above

    key_hi, above = bisect16(hi_ref, topk)
    key_hi_rows = rows(key_hi)

    def low_digits(kb, carry):
        key = _float_to_key(sc_ref[kb])
        low = (key & 0xFFFF) - 2 ** 15
        lo_ref[kb] = jnp.where((key >> 16) == key_hi_rows, low, -(2 ** 15)).astype(jnp.int16)
        return carry

    lax.fori_loop(0, nkp, low_digits, 0)
    key_lo, _ = bisect16(lo_ref, topk - above)
    thr_rows = rows(_key_to_float((key_hi << 16) | (key_lo + 2 ** 15)))

    def write_block(kb, carry):
        picked = jnp.where(sc_ref[kb] >= thr_rows, 0.0, NEG)
        picked = jnp.where(kpos + kb * tk <= qpos, picked, NEG)
        bias_ref[kb] = picked.T.astype(bias_ref.dtype)
        return carry

    lax.fori_loop(0, nkb, write_block, 0)

    def fill_block(kb, carry):
        bias_ref[kb] = jnp.full((tq, tk), NEG, bias_ref.dtype)
        return carry

    lax.fori_loop(nkb, nk, fill_block, 0)


def _select(z_iv, iw_t, ik, topk):
    b, s, _ = z_iv.shape
    tq, tk = SEL_TQ, SEL_TK
    nq, nk = s // tq, s // tk
    return pl.pallas_call(
        functools.partial(_select_kernel, topk=topk),
        grid=(b, nq),
        in_specs=[
            pl.BlockSpec((None, tq, IDX_WIDTH), lambda bb, i: (bb, i, 0)),
            pl.BlockSpec((IDX_HEADS, tq), lambda bb, i: (0, bb * nq + i)),
            pl.BlockSpec((None, s, IDX_DIM), lambda bb, i: (bb, 0, 0)),
        ],
        out_specs=pl.BlockSpec((None, None, nk, tq, tk), lambda bb, i: (bb, i, 0, 0, 0)),
        out_shape=jax.ShapeDtypeStruct((b, nq, nk, tq, tk), BF16),
        scratch_shapes=[
            pltpu.VMEM((IDX_DIM, IDX_HEADS * tq), BF16),
            pltpu.VMEM((nk, tk, tq), F32),
            pltpu.VMEM((nk, tk, tq), jnp.int16),
            pltpu.VMEM((nk, tk, tq), jnp.int16),
        ],
        compiler_params=_params("parallel", "parallel"),
        name="dsa_select",
    )(z_iv, iw_t, ik)


def _dsa_kernel(q_ref, k_ref, v_ref, b_ref, o_ref, m_ref, l_ref, acc_ref, *, tq, tk):
    i = pl.program_id(1)
    kb = pl.program_id(2)
    last = (i * tq + tq - 1) // tk
    nt = (((1,), (1,)), ((), ()))

    @pl.when(kb == 0)
    def _():
        m_ref[...] = jnp.full(m_ref.shape, NEG, F32)
        l_ref[...] = jnp.zeros(l_ref.shape, F32)
        acc_ref[...] = jnp.zeros(acc_ref.shape, F32)

    @pl.when(kb <= last)
    def _():
        bias = b_ref[...].reshape(tq, tk).astype(F32)
        rep = tk // LANES

        def scores(h):
            cols = slice(h * HEAD_DIM, (h + 1) * HEAD_DIM)
            return lax.dot_general(q_ref[:, cols], k_ref[:, cols], nt, preferred_element_type=F32) + bias

        s_next = scores(0)
        for h in range(B_HEADS):
            cols = slice(h * HEAD_DIM, (h + 1) * HEAD_DIM)
            s = s_next
            if h + 1 < B_HEADS:
                s_next = scores(h + 1)
            m_prev = m_ref[h]
            m_new = jnp.maximum(m_prev, jnp.max(s, axis=1, keepdims=True))
            alpha = jnp.exp2(m_prev - m_new)
            p = jnp.exp2(s - jnp.concatenate([m_new] * rep, axis=1))
            l_ref[h] = alpha * l_ref[h] + jnp.sum(p, axis=1, keepdims=True)
            acc_ref[:, cols] = alpha * acc_ref[:, cols] + jnp.dot(p.astype(BF16), v_ref[:, cols],
                                                                  preferred_element_type=F32)
            m_ref[h] = m_new

    @pl.when(kb == last)
    def _():
        for h in range(B_HEADS):
            cols = slice(h * HEAD_DIM, (h + 1) * HEAD_DIM)
            o_ref[:, cols] = (acc_ref[:, cols] / l_ref[h]).astype(o_ref.dtype)


def _dsa_attention(z_qk, z_iv, bias5):
    b, s, _ = z_qk.shape
    tk = SEL_TK
    tq = _pick(s, 512)
    sub = tq // SEL_TQ
    assert IDX_WIDTH == B_WIDTH

    def last(i):
        return (i * tq + tq - 1) // tk

    return pl.pallas_call(
        functools.partial(_dsa_kernel, tq=tq, tk=tk),
        grid=(b, s // tq, s // tk),
        in_specs=[
            pl.BlockSpec((None, tq, B_WIDTH), lambda bb, i, kb: (bb, i, 0)),
            pl.BlockSpec((None, tk, B_WIDTH), lambda bb, i, kb: (bb, jnp.minimum(kb, last(i)), 1)),
            pl.BlockSpec((None, tk, B_WIDTH), lambda bb, i, kb: (bb, jnp.minimum(kb, last(i)), 1)),
            pl.BlockSpec((None, sub, None, SEL_TQ, tk), lambda bb, i, kb: (bb, i, jnp.minimum(kb, last(i)), 0, 0)),
        ],
        out_specs=pl.BlockSpec((None, tq, B_WIDTH), lambda bb, i, kb: (bb, i, 0)),
        out_shape=jax.ShapeDtypeStruct((b, s, B_WIDTH), BF16),
        scratch_shapes=[
            pltpu.VMEM((B_HEADS, tq, LANES), F32),
            pltpu.VMEM((B_HEADS, tq, LANES), F32),
            pltpu.VMEM((tq, B_WIDTH), F32),
        ],
        compiler_params=_params("parallel", "parallel", "arbitrary"),
        name="dsa_attention",
    )(z_qk, z_qk, z_iv, bias5)


def _to_token_order(dst_ref, first, src_ref):
    r, per, width = src_ref.shape
    for c in range(width // LANES):
        cols = slice(c * LANES, (c + 1) * LANES)
        if r == 1:
            dst_ref[first + c] = src_ref[0, :, cols]
        else:
            for p in range(r):
                dst_ref[first + c, pl.ds(p, per, stride=r), :] = src_ref[p, :, cols]


def _merge_kernel(o1_ref, o2_ref, o3_ref, l1_ref, l2_ref, l3_ref, ob_ref, h_ref,
                  wga_ref, wgb_ref, bga_ref, bgb_ref, wpa_ref, wpb_ref, out_ref, oa_ref, ot_ref, lt_ref):
    @pl.when(pl.program_id(1) == 0)
    def _():
        nh = A_HEADS_PER_GROUP
        for g, (o_ref, l_ref) in enumerate(((o1_ref, l1_ref), (o2_ref, l2_ref), (o3_ref, l3_ref))):
            _to_token_order(ot_ref, g * nh, o_ref)
            _to_token_order(lt_ref, g, l_ref)
        l1, l2, l3 = lt_ref[0], lt_ref[1], lt_ref[2]
        mx = jnp.maximum(jnp.maximum(l1, l2), l3)
        e1, e2, e3 = jnp.exp(l1 - mx), jnp.exp(l2 - mx), jnp.exp(l3 - mx)
        tot = e1 + e2 + e3
        w1, w2, w3 = e1 / tot, e2 / tot, e3 / tot
        for hh in range(nh):
            oa = (w1[:, hh:hh + 1] * ot_ref[hh] + w2[:, hh:hh + 1] * ot_ref[nh + hh]
                  + w3[:, hh:hh + 1] * ot_ref[2 * nh + hh])
            oa_ref[:, hh * HEAD_DIM:(hh + 1) * HEAD_DIM] = oa.astype(oa_ref.dtype)

    h = h_ref[...]
    ga = jax.nn.sigmoid(jnp.dot(h, wga_ref[...], preferred_element_type=F32) + bga_ref[...])
    gb = jax.nn.sigmoid(jnp.dot(h, wgb_ref[...], preferred_element_type=F32) + bgb_ref[...])
    pa = jnp.dot(oa_ref[...], wpa_ref[...], preferred_element_type=F32)
    pb = jnp.dot(ob_ref[...], wpb_ref[...], preferred_element_type=F32)
    out_ref[...] = (ga * pa + gb * pb).astype(out_ref.dtype)


def _merge(outs, lses, o_b, h, w_gate, b_gate, w_proj_a, w_proj_b, seq):
    n, d = h.shape
    tm = _pick(seq, 512)
    tpb = seq // tm
    tn = _pick(d, COL_TILE)
    nj = d // tn
    row = lambda width: pl.BlockSpec((tm, width), lambda i, j: (i, 0))

    def streams(arr):
        r, width = arr.shape[1], arr.shape[3]
        return pl.BlockSpec((None, r, tm // r, width), lambda i, j: (i // tpb, 0, i % tpb, 0))

    return pl.pallas_call(
        _merge_kernel,
        grid=(n // tm, nj),
        in_specs=[
            *[streams(a) for a in outs], *[streams(a) for a in lses],
            row(B_WIDTH), row(d),
            pl.BlockSpec((d, tn), lambda i, j: (0, j)),
            pl.BlockSpec((d, tn), lambda i, j: (0, nj + j)),
            pl.BlockSpec((1, tn), lambda i, j: (0, j)),
            pl.BlockSpec((1, tn), lambda i, j: (0, nj + j)),
            pl.BlockSpec((A_GROUP_WIDTH, tn), lambda i, j: (0, j)),
            pl.BlockSpec((B_WIDTH, tn), lambda i, j: (0, j)),
        ],
        out_specs=pl.BlockSpec((tm, tn), lambda i, j: (i, j)),
        out_shape=jax.ShapeDtypeStruct((n, d), BF16),
        scratch_shapes=[
            pltpu.VMEM((tm, A_GROUP_WIDTH), BF16),
            pltpu.VMEM((len(outs) * A_HEADS_PER_GROUP, tm, HEAD_DIM), F32),
            pltpu.VMEM((len(lses), tm, LANES), F32),
        ],
        compiler_params=_params("parallel", "arbitrary"),
        name="gated_merge",
    )(*outs, *lses, o_b, h, w_gate, w_gate, b_gate, b_gate, w_proj_a, w_proj_b)


def _outproj_kernel(mg_ref, w_ref, x_ref, g_ref, sc_ref, sh_ref, xo_ref, ho_ref):
    mix = jnp.dot(mg_ref[...], w_ref[...], preferred_element_type=F32)
    x = x_ref[...] + g_ref[...] * mix
    xo_ref[...] = x
    ho_ref[...] = (_rms(x, x.shape[-1]) * (1.0 + sc_ref[...]) + sh_ref[...]).astype(ho_ref.dtype)


def _outproj(merged, w_out, x, mod, seq):
    n, d = x.shape
    tm = _pick(seq, 256)
    tpb = seq // tm
    row = pl.BlockSpec((tm, d), lambda i: (i, 0))
    return pl.pallas_call(
        _outproj_kernel,
        grid=(n // tm,),
        in_specs=[row, pl.BlockSpec((d, d), lambda i: (0, 0)), row,
                  _mod_spec(d, 2, tpb), _mod_spec(d, 4, tpb), _mod_spec(d, 3, tpb)],
        out_specs=[row, row],
        out_shape=[jax.ShapeDtypeStruct((n, d), F32), jax.ShapeDtypeStruct((n, d), BF16)],
        compiler_params=_params("parallel"),
        name="outproj",
    )(merged, w_out, x, mod, mod, mod)


def _ffn_kernel(h_ref, wu_ref, wd_ref, x_ref, g_ref, o_ref, acc_ref):
    c = pl.program_id(1)

    @pl.when(c == 0)
    def _():
        acc_ref[...] = jnp.zeros(acc_ref.shape, F32)

    u = jnp.maximum(jnp.dot(h_ref[...], wu_ref[...], preferred_element_type=F32), 0.0)
    acc_ref[...] += jnp.dot((u * u).astype(BF16), wd_ref[...], preferred_element_type=F32)

    @pl.when(c == pl.num_programs(1) - 1)
    def _():
        o_ref[...] = x_ref[...] + g_ref[...] * acc_ref[...]


def _ffn(h2, w_up, w_down, x, mod, seq):
    n, d = x.shape
    hidden = w_up.shape[1]
    tm = _pick(seq, 512)
    tc = _pick(hidden, 512)
    tpb = seq // tm
    row = pl.BlockSpec((tm, d), lambda i, c: (i, 0))
    return pl.pallas_call(
        _ffn_kernel,
        grid=(n // tm, hidden // tc),
        in_specs=[row, pl.BlockSpec((d, tc), lambda i, c: (0, c)), pl.BlockSpec((tc, d), lambda i, c: (c, 0)),
                  row, _mod_spec(d, 5, tpb)],
        out_specs=row,
        out_shape=jax.ShapeDtypeStruct((n, d), F32),
        scratch_shapes=[pltpu.VMEM((tm, d), F32)],
        compiler_params=_params("parallel", "arbitrary"),
        name="ffn",
    )(h2, w_up, w_down, x, mod)


def _pack_in_weights(w_in, a_q_gain, a_k_gain, b_q_gain, b_k_gain, idx_k_gain):
    d = w_in.shape[0]
    sizes = (A_WIDTH, A_WIDTH, A_WIDTH, B_WIDTH, B_WIDTH, B_WIDTH, IDX_WIDTH, IDX_DIM, IDX_HEADS)
    parts, off = [], 0
    for sz in sizes:
        parts.append(w_in[:, off:off + sz])
        off += sz
    aq, ak, av, bq, bk, bv, iq, ik, iw = parts
    w_iv = jnp.concatenate([iq, bv], axis=1).astype(BF16)
    w_qk = jnp.concatenate([bq, bk], axis=1).astype(BF16)
    w_idx = jnp.concatenate([ik, iw, jnp.zeros((d, LANES - IDX_DIM - IDX_HEADS), w_in.dtype)], axis=1).astype(BF16)
    ones = lambda width: jnp.ones((width,), F32)
    iv_gain_cols = ones(IDX_WIDTH + B_WIDTH).reshape(1, -1)
    qk_gain_cols = jnp.concatenate([jnp.tile(b_q_gain * DSA_Q_SCALE, B_HEADS), jnp.tile(b_k_gain, B_HEADS)]).reshape(1, -1)
    idx_gain_row = jnp.concatenate([idx_k_gain, ones(LANES - IDX_DIM)]).reshape(1, LANES)
    w_groups = []
    for g in range(len(A_GROUPS)):
        sl = slice(g * A_GROUP_WIDTH, (g + 1) * A_GROUP_WIDTH)
        w_groups.append(jnp.concatenate([aq[:, sl], ak[:, sl], av[:, sl]], axis=1).astype(BF16))
    a_gain_cols = jnp.concatenate([
        jnp.tile(a_q_gain, A_HEADS_PER_GROUP), jnp.tile(a_k_gain, A_HEADS_PER_GROUP), ones(A_GROUP_WIDTH),
    ]).reshape(1, A_PACK_WIDTH)
    return w_iv, iv_gain_cols, w_qk, qk_gain_cols, w_idx, idx_gain_row, w_groups, a_gain_cols


def kernel(x, c, positions, w_ada, b_ada, w_in, a_q_gain, a_k_gain, b_q_gain, b_k_gain, idx_k_gain,
           w_gate, b_gate, w_proj_a, w_proj_b, w_out, w_up, w_down):
    b, s, d = x.shape
    depth = w_ada.shape[0]
    n = b * s
    topk = min(IDX_TOPK, s // 4)
    assert s % (SEL_TK * SEL_GROUP) == 0 and SEL_TK >= topk and d % COL_TILE == 0 and s // LANES < 2 ** 15

    tabs = _rope_tables(positions)
    c128, s128, c64, s64 = tabs
    mods = _ada(c, w_ada, b_ada)
    xf = x.reshape(n, d)

    for l in range(depth):
        mod = mods[l]
        w_iv, iv_gain_cols, w_qk, qk_gain_cols, w_idx, idx_gain_row, w_groups, a_gain_cols = _pack_in_weights(
            w_in[l], a_q_gain[l], a_k_gain[l], b_q_gain[l], b_k_gain[l], idx_k_gain[l])

        h = _normmod(xf, mod, s, 1, 0)
        z_iv = _bproj(h, w_iv, iv_gain_cols, tabs, IV_EPILOGUES, "proj_iq_bv").reshape(b, s, -1)
        z_qk = _bproj(h, w_qk, qk_gain_cols, tabs, QK_EPILOGUES, "proj_bq_bk").reshape(b, s, -1)
        ik, iw = _idxproj(h, w_idx, idx_gain_row, c64, s64)

        a_outs, a_lses = [], []
        for g, (window, dilation) in enumerate(A_GROUPS):
            qkv = _aproj(h, w_groups[g], a_gain_cols, tabs, b, s, dilation)
            o, lse = _dilated(qkv, window)
            a_outs.append(o)
            a_lses.append(lse)

        bias5 = _select(z_iv, iw, ik.reshape(b, s, IDX_DIM), topk)
        o_b = _dsa_attention(z_qk, z_iv, bias5).reshape(n, B_WIDTH)

        merged = _merge(a_outs, a_lses, o_b, h, w_gate[l].astype(BF16), b_gate[l].reshape(1, 2 * d),
                        w_proj_a[l].astype(BF16), w_proj_b[l].astype(BF16), s)
        xf, h2 = _outproj(merged, w_out[l].astype(BF16), xf, mod, s)
        xf = _ffn(h2, w_up[l].astype(BF16), w_down[l].astype(BF16), xf, mod, s)

    return xf.reshape(b, s, d)
```

```python
import functools

import jax
import jax.numpy as jnp
from jax import lax
from jax.experimental import pallas as pl
from jax.experimental.pallas import tpu as pltpu

F32 = jnp.float32
BF16 = jnp.bfloat16

HEAD_DIM = 128
LANES = 128
A_GROUPS = ((128, 1), (512, 4), (2048, 16))
A_HEADS_PER_GROUP = 4
A_GROUP_WIDTH = A_HEADS_PER_GROUP * HEAD_DIM
A_WIDTH = len(A_GROUPS) * A_GROUP_WIDTH
B_HEADS = 8
B_WIDTH = B_HEADS * HEAD_DIM
IDX_HEADS = 16
IDX_DIM = 64
IDX_WIDTH = IDX_HEADS * IDX_DIM
IDX_TOPK = 256
ROPE_THETA = 10000.0
EPS = 1e-6
N_MOD = 6
NEG = -1e30
LOG2_E = 1.4426950408889634
DSA_Q_SCALE = HEAD_DIM ** -0.5 * LOG2_E

COL_TILE = 512
MXU_COLS = 256
EPI_ROPE64, EPI_QK, EPI_PLAIN = 0, 1, 2
IV_EPILOGUES = (EPI_ROPE64,) * (IDX_WIDTH // MXU_COLS) + (EPI_PLAIN,) * (B_WIDTH // MXU_COLS)
QK_EPILOGUES = (EPI_QK,) * (2 * B_WIDTH // MXU_COLS)
A_EPILOGUES = (EPI_QK,) * (2 * A_GROUP_WIDTH // MXU_COLS) + (EPI_PLAIN,) * (A_GROUP_WIDTH // MXU_COLS)
A_PACK_WIDTH = 3 * A_GROUP_WIDTH

SEL_TQ = 128
SEL_TK = 512
SEL_GROUP = 4
SEL_UNTESTED_STEPS = 11
VMEM_LIMIT = 52 * 1024 * 1024


def _params(*sem):
    return pltpu.CompilerParams(dimension_semantics=sem, vmem_limit_bytes=VMEM_LIMIT)


def _pick(n, pref):
    t = pref
    while n % t:
        t //= 2
    return t


def _rms(x, width):
    return x * lax.rsqrt(jnp.sum(x * x, axis=-1, keepdims=True) * (1.0 / width) + EPS)


def _swap_half64(y):
    lane = lax.broadcasted_iota(jnp.int32, y.shape, 1)
    return jnp.where((lane & 63) < 32, pltpu.roll(y, 96, 1), pltpu.roll(y, 32, 1))


def _rope_tables_kernel(pos_ref, f128_ref, g128_ref, f64_ref, g64_ref, c128_ref, s128_ref, c64_ref, s64_ref):
    pos = pos_ref[...]
    a = pos * f128_ref[...]
    c128_ref[...] = jnp.cos(a)
    s128_ref[...] = jnp.sin(a) * g128_ref[...]
    a = pos * f64_ref[...]
    c64_ref[...] = jnp.cos(a)
    s64_ref[...] = jnp.sin(a) * g64_ref[...]


def _rope_tables(positions):
    n = positions.size
    pos = positions.reshape(n, 1).astype(F32)

    def freq(d):
        half = d // 2
        inv = jnp.power(ROPE_THETA, -jnp.arange(half, dtype=F32) * 2.0 / d)
        f = jnp.tile(jnp.concatenate([inv, inv]), LANES // d)
        g = jnp.tile(jnp.concatenate([-jnp.ones((half,), F32), jnp.ones((half,), F32)]), LANES // d)
        return f.reshape(1, LANES), g.reshape(1, LANES)

    f128, g128 = freq(HEAD_DIM)
    f64, g64 = freq(IDX_DIM)
    tm = _pick(n, 1024)
    row = pl.BlockSpec((1, LANES), lambda i: (0, 0))
    tab = pl.BlockSpec((tm, LANES), lambda i: (i, 0))
    return pl.pallas_call(
        _rope_tables_kernel,
        grid=(n // tm,),
        in_specs=[pl.BlockSpec((tm, 1), lambda i: (i, 0)), row, row, row, row],
        out_specs=[tab, tab, tab, tab],
        out_shape=[jax.ShapeDtypeStruct((n, LANES), F32)] * 4,
        compiler_params=_params("parallel"),
        name="rope_tables",
    )(pos, f128, g128, f64, g64)


def _ada_kernel(c_ref, w_ref, b_ref, o_ref):
    c = c_ref[...]
    act = (c * jax.nn.sigmoid(c)).astype(BF16)
    o_ref[...] = jnp.dot(act, w_ref[...].astype(BF16), preferred_element_type=F32) + b_ref[...]


def _ada(c, w_ada, b_ada):
    depth, d, n6 = w_ada.shape
    b = c.shape[0]
    rows = 8
    c_pad = jnp.zeros((rows, d), F32).at[:b].set(c)
    tn = _pick(n6, 1024)
    out = pl.pallas_call(
        _ada_kernel,
        grid=(depth, n6 // tn),
        in_specs=[
            pl.BlockSpec((rows, d), lambda l, j: (0, 0)),
            pl.BlockSpec((None, d, tn), lambda l, j: (l, 0, j)),
            pl.BlockSpec((None, 1, tn), lambda l, j: (l, 0, j)),
        ],
        out_specs=pl.BlockSpec((None, rows, tn), lambda l, j: (l, 0, j)),
        out_shape=jax.ShapeDtypeStruct((depth, rows, n6), F32),
        compiler_params=_params("parallel", "parallel"),
        name="adaln",
    )(c_pad, w_ada, b_ada.reshape(depth, 1, n6))
    return out[:, :b].reshape(depth, b, N_MOD, 1, d)


def _mod_spec(d, which, tiles_per_batch):
    return pl.BlockSpec((None, None, 1, d), lambda i, *_: (i // tiles_per_batch, which, 0, 0))


def _normmod_kernel(x_ref, sc_ref, sh_ref, o_ref):
    x = x_ref[...]
    y = _rms(x, x.shape[-1])
    o_ref[...] = (y * (1.0 + sc_ref[...]) + sh_ref[...]).astype(o_ref.dtype)


def _normmod(x, mod, seq, which_scale, which_shift):
    n, d = x.shape
    tm = _pick(seq, 512)
    tpb = seq // tm
    return pl.pallas_call(
        _normmod_kernel,
        grid=(n // tm,),
        in_specs=[pl.BlockSpec((tm, d), lambda i: (i, 0)), _mod_spec(d, which_scale, tpb), _mod_spec(d, which_shift, tpb)],
        out_specs=pl.BlockSpec((tm, d), lambda i: (i, 0)),
        out_shape=jax.ShapeDtypeStruct((n, d), BF16),
        compiler_params=_params("parallel"),
        name="normmod",
    )(x, mod, mod)


def _proj_kernel(h_ref, w_ref, g_ref, c128_ref, s128_ref, c64_ref, s64_ref, o_ref, z_ref, *, epilogues, streams):
    h = h_ref[...]
    per = h.shape[0] // streams
    for t, kind in enumerate(epilogues):
        z = jnp.dot(h, w_ref[:, t * MXU_COLS:(t + 1) * MXU_COLS], preferred_element_type=F32)
        for c in range(MXU_COLS // LANES):
            slab = t * (MXU_COLS // LANES) + c
            cols = slice(slab * LANES, (slab + 1) * LANES)
            y = z[:, c * LANES:(c + 1) * LANES]
            if kind == EPI_QK:
                y = _rms(y, HEAD_DIM) * g_ref[:, cols]
                y = y * c128_ref[...] + pltpu.roll(y, HEAD_DIM // 2, 1) * s128_ref[...]
            elif kind == EPI_ROPE64:
                y = y * c64_ref[...] + _swap_half64(y) * s64_ref[...]
            if streams == 1:
                o_ref[..., cols] = y.astype(o_ref.dtype).reshape(o_ref.shape[:-1] + (LANES,))
            else:
                z_ref[slab] = y
                for p in range(streams):
                    o_ref[p, :, cols] = z_ref[slab, pl.ds(p, per, stride=streams), :].astype(o_ref.dtype)


def _proj_call(h, w, gain_cols, tabs, epilogues, out_spec, out_shape, tm, streams, name):
    d, width = w.shape
    assert width == len(epilogues) * MXU_COLS
    tab = pl.BlockSpec((tm, LANES), lambda i: (i, 0))
    slabs = width // LANES if streams > 1 else 1
    return pl.pallas_call(
        functools.partial(_proj_kernel, epilogues=epilogues, streams=streams),
        grid=(h.shape[0] // tm,),
        in_specs=[
            pl.BlockSpec((tm, d), lambda i: (i, 0)),
            pl.BlockSpec((d, width), lambda i: (0, 0)),
            pl.BlockSpec((1, width), lambda i: (0, 0)),
            tab, tab, tab, tab,
        ],
        out_specs=out_spec,
        out_shape=out_shape,
        scratch_shapes=[pltpu.VMEM((slabs, tm, LANES), F32)],
        compiler_params=_params("parallel"),
        name=name,
    )(h, w, gain_cols, *tabs)


def _bproj(h, w, gain_cols, tabs, epilogues, name):
    n = h.shape[0]
    tm = _pick(n, 512)
    width = w.shape[1]
    return _proj_call(h, w, gain_cols, tabs, epilogues, pl.BlockSpec((tm, width), lambda i: (i, 0)),
                      jax.ShapeDtypeStruct((n, width), BF16), tm, 1, name)


def _aproj(h, w_group, gain_cols, tabs, batch, seq, dilation):
    r = dilation
    tm = _pick(seq, 512)
    tpb = seq // tm
    assert tm % (r * 16) == 0
    return _proj_call(h, w_group, gain_cols, tabs, A_EPILOGUES,
                      pl.BlockSpec((None, r, tm // r, A_PACK_WIDTH), lambda i: (i // tpb, 0, i % tpb, 0)),
                      jax.ShapeDtypeStruct((batch, r, seq // r, A_PACK_WIDTH), BF16), tm, r, f"aproj_r{r}")


def _idxproj_kernel(h_ref, w_ref, g_ref, c64_ref, s64_ref, ik_ref, iw_ref):
    z = jnp.dot(h_ref[...], w_ref[...], preferred_element_type=F32)
    lane = lax.broadcasted_iota(jnp.int32, z.shape, 1)
    is_k = lane < IDX_DIM
    zk = jnp.where(is_k, z, 0.0)
    y = _rms(zk, IDX_DIM) * g_ref[...]
    y = y * c64_ref[...] + _swap_half64(y) * s64_ref[...]
    ik_ref[...] = y[:, :IDX_DIM].astype(ik_ref.dtype)
    iw_ref[...] = z.T[IDX_DIM:IDX_DIM + IDX_HEADS, :] * (IDX_HEADS ** -0.5 * IDX_DIM ** -0.5)


def _idxproj(h, w_idx, gain_row, c64, s64):
    n, d = h.shape
    tm = _pick(n, 512)
    tab = pl.BlockSpec((tm, LANES), lambda i: (i, 0))
    return pl.pallas_call(
        _idxproj_kernel,
        grid=(n // tm,),
        in_specs=[
            pl.BlockSpec((tm, d), lambda i: (i, 0)),
            pl.BlockSpec((d, LANES), lambda i: (0, 0)),
            pl.BlockSpec((1, LANES), lambda i: (0, 0)),
            tab, tab,
        ],
        out_specs=[pl.BlockSpec((tm, IDX_DIM), lambda i: (i, 0)), pl.BlockSpec((IDX_HEADS, tm), lambda i: (0, i))],
        out_shape=[jax.ShapeDtypeStruct((n, IDX_DIM), BF16), jax.ShapeDtypeStruct((IDX_HEADS, n), F32)],
        compiler_params=_params("parallel"),
        name="idxproj",
    )(h, w_idx, gain_row, c64, s64)


def _dilated_kernel(q_ref, kc_ref, kp_ref, vc_ref, vp_ref, o_ref, lse_ref, *, tq):
    i = pl.program_id(2)
    blk = LANES
    scale = HEAD_DIM ** -0.5
    nkeys = blk + tq
    row = lax.broadcasted_iota(jnp.int32, (tq, nkeys), 0)
    col = lax.broadcasted_iota(jnp.int32, (tq, nkeys), 1)
    dist = row + blk - col
    band = jnp.where(dist >= 0, jnp.where(dist <= blk, 0.0, -jnp.inf), -jnp.inf)
    first = jnp.where(col >= blk, 0.0, -jnp.inf)
    bias = band + jnp.where(i > 0, 0.0, first)
    lane = lax.broadcasted_iota(jnp.int32, (tq, LANES), 1)
    nt = (((1,), (1,)), ((), ()))

    def scores(hh):
        cols = slice(hh * HEAD_DIM, (hh + 1) * HEAD_DIM)
        keys = jnp.concatenate([kp_ref[:, cols], kc_ref[:, cols]], axis=0)
        return lax.dot_general(q_ref[:, cols], keys, nt, preferred_element_type=F32) * scale + bias

    lse_tile = jnp.zeros((tq, LANES), F32)
    s_next = scores(0)
    for hh in range(A_HEADS_PER_GROUP):
        cols = slice(hh * HEAD_DIM, (hh + 1) * HEAD_DIM)
        s = s_next
        if hh + 1 < A_HEADS_PER_GROUP:
            s_next = scores(hh + 1)
        m = jnp.max(s, axis=1, keepdims=True)
        e = jnp.exp(s - m)
        den = jnp.sum(e, axis=1, keepdims=True)
        values = jnp.concatenate([vp_ref[:, cols], vc_ref[:, cols]], axis=0)
        acc = jnp.dot(e.astype(BF16), values, preferred_element_type=F32)
        o_ref[:, cols] = acc / den
        lse_tile = jnp.where(lane == hh, m + jnp.log(den), lse_tile)
    lse_ref[...] = lse_tile


def _dilated(qkv, window):
    b, r, m, _ = qkv.shape
    assert window // r == LANES and m % LANES == 0
    tq = _pick(m, 512)
    nsub = tq // LANES

    def cur(tile):
        return pl.BlockSpec((None, None, tq, COL_TILE), lambda bb, p, i: (bb, p, i, tile))

    def prev(tile):
        return pl.BlockSpec((None, None, LANES, COL_TILE),
                            lambda bb, p, i: (bb, p, jnp.maximum(i * nsub - 1, 0), tile))

    return pl.pallas_call(
        functools.partial(_dilated_kernel, tq=tq),
        grid=(b, r, m // tq),
        in_specs=[cur(0), cur(1), prev(1), cur(2), prev(2)],
        out_specs=[
            pl.BlockSpec((None, None, tq, A_GROUP_WIDTH), lambda bb, p, i: (bb, p, i, 0)),
            pl.BlockSpec((None, None, tq, LANES), lambda bb, p, i: (bb, p, i, 0)),
        ],
        out_shape=[jax.ShapeDtypeStruct((b, r, m, A_GROUP_WIDTH), F32), jax.ShapeDtypeStruct((b, r, m, LANES), F32)],
        compiler_params=_params("parallel", "parallel", "parallel"),
        name=f"dilated_r{r}",
    )(qkv, qkv, qkv, qkv, qkv)


def _key_to_float(key):
    bits = jnp.where(key >= 0, key, key ^ 0x7FFFFFFF)
    return lax.bitcast_convert_type(bits, F32)


def _float_to_key(x):
    bits = lax.bitcast_convert_type(x, jnp.int32)
    return jnp.where(bits >= 0, bits, bits ^ 0x7FFFFFFF)


PACK16 = 16


def _select_kernel(iq_ref, wt_ref, k_ref, bias_ref, qt_ref, sc_ref, hi_ref, lo_ref, gm_ref, *, topk):
    i = pl.program_id(1)
    tq, tk = SEL_TQ, SEL_TK
    nk = sc_ref.shape[0]
    nkb = (i * tq + tq + tk - 1) // tk
    slabs = tk // PACK16

    q_t = iq_ref[...].astype(F32).T
    for h in range(IDX_HEADS):
        qt_ref[:, h * tq:(h + 1) * tq] = q_t[h * IDX_DIM:(h + 1) * IDX_DIM, :].astype(qt_ref.dtype)

    kpos = lax.broadcasted_iota(jnp.int32, (tk, tq), 0)
    qpos = i * tq + lax.broadcasted_iota(jnp.int32, (tk, tq), 1)

    def rows(x):
        return jnp.concatenate([x] * slabs, axis=0)

    def score_block(kb, carry):
        keys = k_ref[pl.ds(pl.multiple_of(kb * tk, tk), tk), :]
        logits = jnp.dot(keys, qt_ref[...], preferred_element_type=F32)
        acc = jnp.zeros((tk, tq), F32)
        for h in range(IDX_HEADS):
            acc = acc + wt_ref[h:h + 1, :] * jnp.maximum(logits[:, h * tq:(h + 1) * tq], 0.0)
        masked = jnp.where(kpos + kb * tk <= qpos, acc, -jnp.inf)
        sc_ref[kb] = masked
        hi_ref[kb] = (_float_to_key(masked) >> 16).astype(jnp.int16)
        gm_ref[...] = jnp.maximum(gm_ref[...], jnp.maximum(masked[:tk // 2], masked[tk // 2:]))
        return carry

    gm_ref[...] = jnp.full(gm_ref.shape, -jnp.inf, F32)
    lax.fori_loop(0, nkb, score_block, 0)

    def reps(x):
        return jnp.broadcast_to(x, (PACK16, tq))

    bound_lo = reps(_float_to_key(jnp.min(gm_ref[...], axis=0, keepdims=True)) >> 16)
    bound_hi = reps(_float_to_key(jnp.max(gm_ref[...], axis=0, keepdims=True)) >> 16) + 1

    ngr = (nkb + SEL_GROUP - 1) // SEL_GROUP
    nkp = ngr * SEL_GROUP

    def pad_block(kb, carry):
        neg_inf = jnp.full((tk, tq), -jnp.inf, F32)
        sc_ref[kb] = neg_inf
        hi_ref[kb] = (_float_to_key(neg_inf) >> 16).astype(jnp.int16)
        return carry

    lax.fori_loop(nkb, nkp, pad_block, 0)

    def bisect16(ref, need, lo0, hi0, high_digit):
        def count_ge(t):
            t16 = t.astype(jnp.int16)

            def body(g, accs):
                accs = list(accs)
                for j in range(SEL_GROUP):
                    for r in range(slabs):
                        blk = ref[g * SEL_GROUP + j, r * PACK16:(r + 1) * PACK16, :]
                        hit = jnp.where(blk >= t16, jnp.int16(1), jnp.int16(0))
                        accs[r % len(accs)] = accs[r % len(accs)] + hit
                return tuple(accs)

            zero = jnp.zeros((PACK16, tq), jnp.int16)
            accs = lax.fori_loop(0, ngr, body, (zero,) * 4)
            acc = (accs[0] + accs[1]) + (accs[2] + accs[3])
            cnt = jnp.sum(acc.astype(jnp.int32), axis=0, keepdims=True)
            return jnp.broadcast_to(cnt, (PACK16, tq))

        def open_brackets(carry):
            _, lo, hi, _ = carry
            return jnp.max(hi - lo) > 1

        def step(carry):
            it, lo, hi, above = carry
            mid = (lo + hi) >> 1
            if high_digit:
                v_mid = 0.5 * _key_to_float((lo << 16) | 0xFFFF) + 0.5 * _key_to_float(hi << 16)
                by_value = jnp.minimum(jnp.maximum(_float_to_key(v_mid) >> 16, lo + 1), hi - 1)
                mid = jnp.where(it % 3 == 2, mid, by_value)
            mid = jnp.where(hi - lo > 1, mid, lo)
            cnt = count_ge(mid)
            ge = cnt >= need
            lo, hi, above = jnp.where(ge, mid, lo), jnp.where(ge, hi, mid), jnp.where(ge, above, cnt)
            if not high_digit:
                hi = jnp.where(cnt == need, mid + 1, hi)
            return it + 1, lo, hi, above

        carry = (jnp.int32(0), lo0, hi0, jnp.zeros((PACK16, tq), jnp.int32))
        carry = lax.fori_loop(0, SEL_UNTESTED_STEPS, lambda _, c: step(c), carry)
        _, lo, _, above = lax.while_loop(open_brackets, step, carry)
        return lo, above

    key_hi, above = bisect16(hi_ref, topk, bound_lo, bound_hi, True)
    key_hi_rows = rows(key_hi)

    def low_digits(kb, carry):
        key = _float_to_key(sc_ref[kb])
        low = (key & 0xFFFF) - 2 ** 15
        lo_ref[kb] = jnp.where((key >> 16) == key_hi_rows, low, -(2 ** 15)).astype(jnp.int16)
        return carry

    lax.fori_loop(0, nkp, low_digits, 0)
    digit_lo = jnp.full((PACK16, tq), -(2 ** 15), jnp.int32)
    digit_hi = jnp.full((PACK16, tq), 2 ** 15, jnp.int32)
    key_lo, _ = bisect16(lo_ref, topk - above, digit_lo, digit_hi, False)
    thr_rows = rows(_key_to_float((key_hi << 16) | (key_lo + 2 ** 15)))

    def write_block(kb, carry):
        picked = jnp.where(sc_ref[kb] >= thr_rows, 0.0, NEG)
        picked = jnp.where(kpos + kb * tk <= qpos, picked, NEG)
        bias_ref[kb] = picked.T.astype(bias_ref.dtype)
        return carry

    lax.fori_loop(0, nkb, write_block, 0)

    def fill_block(kb, carry):
        bias_ref[kb] = jnp.full((tq, tk), NEG, bias_ref.dtype)
        return carry

    lax.fori_loop(nkb, nk, fill_block, 0)


def _select(z_iv, iw_t, ik, topk):
    b, s, _ = z_iv.shape
    tq, tk = SEL_TQ, SEL_TK
    nq, nk = s // tq, s // tk
    return pl.pallas_call(
        functools.partial(_select_kernel, topk=topk),
        grid=(b, nq),
        in_specs=[
            pl.BlockSpec((None, tq, IDX_WIDTH), lambda bb, i: (bb, i, 0)),
            pl.BlockSpec((IDX_HEADS, tq), lambda bb, i: (0, bb * nq + i)),
            pl.BlockSpec((None, s, IDX_DIM), lambda bb, i: (bb, 0, 0)),
        ],
        out_specs=pl.BlockSpec((None, None, nk, tq, tk), lambda bb, i: (bb, i, 0, 0, 0)),
        out_shape=jax.ShapeDtypeStruct((b, nq, nk, tq, tk), BF16),
        scratch_shapes=[
            pltpu.VMEM((IDX_DIM, IDX_HEADS * tq), BF16),
            pltpu.VMEM((nk, tk, tq), F32),
            pltpu.VMEM((nk, tk, tq), jnp.int16),
            pltpu.VMEM((nk, tk, tq), jnp.int16),
            pltpu.VMEM((tk // 2, tq), F32),
        ],
        compiler_params=_params("parallel", "parallel"),
        name="dsa_select",
    )(z_iv, iw_t, ik)


def _dsa_kernel(q_ref, k_ref, v_ref, b_ref, o_ref, m_ref, l_ref, acc_ref, *, tq, tk):
    i = pl.program_id(1)
    kb = pl.program_id(2)
    last = (i * tq + tq - 1) // tk
    nt = (((1,), (1,)), ((), ()))

    @pl.when(kb == 0)
    def _():
        m_ref[...] = jnp.full(m_ref.shape, NEG, F32)
        l_ref[...] = jnp.zeros(l_ref.shape, F32)
        acc_ref[...] = jnp.zeros(acc_ref.shape, F32)

    @pl.when(kb <= last)
    def _():
        bias = b_ref[...].reshape(tq, tk).astype(F32)
        rep = tk // LANES

        def scores(h):
            cols = slice(h * HEAD_DIM, (h + 1) * HEAD_DIM)
            return lax.dot_general(q_ref[:, cols], k_ref[:, cols], nt, preferred_element_type=F32) + bias

        s_next = scores(0)
        for h in range(B_HEADS):
            cols = slice(h * HEAD_DIM, (h + 1) * HEAD_DIM)
            s = s_next
            if h + 1 < B_HEADS:
                s_next = scores(h + 1)
            m_prev = m_ref[h]
            m_new = jnp.maximum(m_prev, jnp.max(s, axis=1, keepdims=True))
            alpha = jnp.exp2(m_prev - m_new)
            p = jnp.exp2(s - jnp.concatenate([m_new] * rep, axis=1))
            l_ref[h] = alpha * l_ref[h] + jnp.sum(p, axis=1, keepdims=True)
            acc_ref[:, cols] = alpha * acc_ref[:, cols] + jnp.dot(p.astype(BF16), v_ref[:, cols],
                                                                  preferred_element_type=F32)
            m_ref[h] = m_new

    @pl.when(kb == last)
    def _():
        for h in range(B_HEADS):
            cols = slice(h * HEAD_DIM, (h + 1) * HEAD_DIM)
            o_ref[:, cols] = (acc_ref[:, cols] / l_ref[h]).astype(o_ref.dtype)


def _dsa_attention(z_qk, z_iv, bias5):
    b, s, _ = z_qk.shape
    tk = SEL_TK
    tq = _pick(s, 512)
    sub = tq // SEL_TQ
    assert IDX_WIDTH == B_WIDTH

    def last(i):
        return (i * tq + tq - 1) // tk

    return pl.pallas_call(
        functools.partial(_dsa_kernel, tq=tq, tk=tk),
        grid=(b, s // tq, s // tk),
        in_specs=[
            pl.BlockSpec((None, tq, B_WIDTH), lambda bb, i, kb: (bb, i, 0)),
            pl.BlockSpec((None, tk, B_WIDTH), lambda bb, i, kb: (bb, jnp.minimum(kb, last(i)), 1)),
            pl.BlockSpec((None, tk, B_WIDTH), lambda bb, i, kb: (bb, jnp.minimum(kb, last(i)), 1)),
            pl.BlockSpec((None, sub, None, SEL_TQ, tk), lambda bb, i, kb: (bb, i, jnp.minimum(kb, last(i)), 0, 0)),
        ],
        out_specs=pl.BlockSpec((None, tq, B_WIDTH), lambda bb, i, kb: (bb, i, 0)),
        out_shape=jax.ShapeDtypeStruct((b, s, B_WIDTH), BF16),
        scratch_shapes=[
            pltpu.VMEM((B_HEADS, tq, LANES), F32),
            pltpu.VMEM((B_HEADS, tq, LANES), F32),
            pltpu.VMEM((tq, B_WIDTH), F32),
        ],
        compiler_params=_params("parallel", "parallel", "arbitrary"),
        name="dsa_attention",
    )(z_qk, z_qk, z_iv, bias5)


def _to_token_order(dst_ref, first, src_ref):
    r, per, width = src_ref.shape
    for c in range(width // LANES):
        cols = slice(c * LANES, (c + 1) * LANES)
        if r == 1:
            dst_ref[first + c] = src_ref[0, :, cols]
        else:
            for p in range(r):
                dst_ref[first + c, pl.ds(p, per, stride=r), :] = src_ref[p, :, cols]


def _merge_kernel(o1_ref, o2_ref, o3_ref, l1_ref, l2_ref, l3_ref, ob_ref, h_ref,
                  wga_ref, wgb_ref, bga_ref, bgb_ref, wpa_ref, wpb_ref, out_ref, oa_ref, ot_ref, lt_ref):
    @pl.when(pl.program_id(1) == 0)
    def _():
        nh = A_HEADS_PER_GROUP
        for g, (o_ref, l_ref) in enumerate(((o1_ref, l1_ref), (o2_ref, l2_ref), (o3_ref, l3_ref))):
            _to_token_order(ot_ref, g * nh, o_ref)
            _to_token_order(lt_ref, g, l_ref)
        l1, l2, l3 = lt_ref[0], lt_ref[1], lt_ref[2]
        mx = jnp.maximum(jnp.maximum(l1, l2), l3)
        e1, e2, e3 = jnp.exp(l1 - mx), jnp.exp(l2 - mx), jnp.exp(l3 - mx)
        tot = e1 + e2 + e3
        w1, w2, w3 = e1 / tot, e2 / tot, e3 / tot
        for hh in range(nh):
            oa = (w1[:, hh:hh + 1] * ot_ref[hh] + w2[:, hh:hh + 1] * ot_ref[nh + hh]
                  + w3[:, hh:hh + 1] * ot_ref[2 * nh + hh])
            oa_ref[:, hh * HEAD_DIM:(hh + 1) * HEAD_DIM] = oa.astype(oa_ref.dtype)

    h = h_ref[...]
    ga = jax.nn.sigmoid(jnp.dot(h, wga_ref[...], preferred_element_type=F32) + bga_ref[...])
    gb = jax.nn.sigmoid(jnp.dot(h, wgb_ref[...], preferred_element_type=F32) + bgb_ref[...])
    pa = jnp.dot(oa_ref[...], wpa_ref[...], preferred_element_type=F32)
    pb = jnp.dot(ob_ref[...], wpb_ref[...], preferred_element_type=F32)
    out_ref[...] = (ga * pa + gb * pb).astype(out_ref.dtype)


def _merge(outs, lses, o_b, h, w_gate, b_gate, w_proj_a, w_proj_b, seq):
    n, d = h.shape
    tm = _pick(seq, 512)
    tpb = seq // tm
    tn = _pick(d, COL_TILE)
    nj = d // tn
    row = lambda width: pl.BlockSpec((tm, width), lambda i, j: (i, 0))

    def streams(arr):
        r, width = arr.shape[1], arr.shape[3]
        return pl.BlockSpec((None, r, tm // r, width), lambda i, j: (i // tpb, 0, i % tpb, 0))

    return pl.pallas_call(
        _merge_kernel,
        grid=(n // tm, nj),
        in_specs=[
            *[streams(a) for a in outs], *[streams(a) for a in lses],
            row(B_WIDTH), row(d),
            pl.BlockSpec((d, tn), lambda i, j: (0, j)),
            pl.BlockSpec((d, tn), lambda i, j: (0, nj + j)),
            pl.BlockSpec((1, tn), lambda i, j: (0, j)),
            pl.BlockSpec((1, tn), lambda i, j: (0, nj + j)),
            pl.BlockSpec((A_GROUP_WIDTH, tn), lambda i, j: (0, j)),
            pl.BlockSpec((B_WIDTH, tn), lambda i, j: (0, j)),
        ],
        out_specs=pl.BlockSpec((tm, tn), lambda i, j: (i, j)),
        out_shape=jax.ShapeDtypeStruct((n, d), BF16),
        scratch_shapes=[
            pltpu.VMEM((tm, A_GROUP_WIDTH), BF16),
            pltpu.VMEM((len(outs) * A_HEADS_PER_GROUP, tm, HEAD_DIM), F32),
            pltpu.VMEM((len(lses), tm, LANES), F32),
        ],
        compiler_params=_params("parallel", "arbitrary"),
        name="gated_merge",
    )(*outs, *lses, o_b, h, w_gate, w_gate, b_gate, b_gate, w_proj_a, w_proj_b)


def _outproj_kernel(mg_ref, w_ref, x_ref, g_ref, sc_ref, sh_ref, xo_ref, ho_ref):
    mix = jnp.dot(mg_ref[...], w_ref[...], preferred_element_type=F32)
    x = x_ref[...] + g_ref[...] * mix
    xo_ref[...] = x
    ho_ref[...] = (_rms(x, x.shape[-1]) * (1.0 + sc_ref[...]) + sh_ref[...]).astype(ho_ref.dtype)


def _outproj(merged, w_out, x, mod, seq):
    n, d = x.shape
    tm = _pick(seq, 256)
    tpb = seq // tm
    row = pl.BlockSpec((tm, d), lambda i: (i, 0))
    return pl.pallas_call(
        _outproj_kernel,
        grid=(n // tm,),
        in_specs=[row, pl.BlockSpec((d, d), lambda i: (0, 0)), row,
                  _mod_spec(d, 2, tpb), _mod_spec(d, 4, tpb), _mod_spec(d, 3, tpb)],
        out_specs=[row, row],
        out_shape=[jax.ShapeDtypeStruct((n, d), F32), jax.ShapeDtypeStruct((n, d), BF16)],
        compiler_params=_params("parallel"),
        name="outproj",
    )(merged, w_out, x, mod, mod, mod)


def _ffn_kernel(h_ref, wu_ref, wd_ref, x_ref, g_ref, o_ref, acc_ref):
    c = pl.program_id(1)

    @pl.when(c == 0)
    def _():
        acc_ref[...] = jnp.zeros(acc_ref.shape, F32)

    u = jnp.maximum(jnp.dot(h_ref[...], wu_ref[...], preferred_element_type=F32), 0.0)
    acc_ref[...] += jnp.dot((u * u).astype(BF16), wd_ref[...], preferred_element_type=F32)

    @pl.when(c == pl.num_programs(1) - 1)
    def _():
        o_ref[...] = x_ref[...] + g_ref[...] * acc_ref[...]


def _ffn(h2, w_up, w_down, x, mod, seq):
    n, d = x.shape
    hidden = w_up.shape[1]
    tm = _pick(seq, 512)
    tc = _pick(hidden, 512)
    tpb = seq // tm
    row = pl.BlockSpec((tm, d), lambda i, c: (i, 0))
    return pl.pallas_call(
        _ffn_kernel,
        grid=(n // tm, hidden // tc),
        in_specs=[row, pl.BlockSpec((d, tc), lambda i, c: (0, c)), pl.BlockSpec((tc, d), lambda i, c: (c, 0)),
                  row, _mod_spec(d, 5, tpb)],
        out_specs=row,
        out_shape=jax.ShapeDtypeStruct((n, d), F32),
        scratch_shapes=[pltpu.VMEM((tm, d), F32)],
        compiler_params=_params("parallel", "arbitrary"),
        name="ffn",
    )(h2, w_up, w_down, x, mod)


def _pack_in_weights(w_in, a_q_gain, a_k_gain, b_q_gain, b_k_gain, idx_k_gain):
    d = w_in.shape[0]
    sizes = (A_WIDTH, A_WIDTH, A_WIDTH, B_WIDTH, B_WIDTH, B_WIDTH, IDX_WIDTH, IDX_DIM, IDX_HEADS)
    parts, off = [], 0
    for sz in sizes:
        parts.append(w_in[:, off:off + sz])
        off += sz
    aq, ak, av, bq, bk, bv, iq, ik, iw = parts
    w_iv = jnp.concatenate([iq, bv], axis=1).astype(BF16)
    w_qk = jnp.concatenate([bq, bk], axis=1).astype(BF16)
    w_idx = jnp.concatenate([ik, iw, jnp.zeros((d, LANES - IDX_DIM - IDX_HEADS), w_in.dtype)], axis=1).astype(BF16)
    ones = lambda width: jnp.ones((width,), F32)
    iv_gain_cols = ones(IDX_WIDTH + B_WIDTH).reshape(1, -1)
    qk_gain_cols = jnp.concatenate([jnp.tile(b_q_gain * DSA_Q_SCALE, B_HEADS), jnp.tile(b_k_gain, B_HEADS)]).reshape(1, -1)
    idx_gain_row = jnp.concatenate([idx_k_gain, ones(LANES - IDX_DIM)]).reshape(1, LANES)
    w_groups = []
    for g in range(len(A_GROUPS)):
        sl = slice(g * A_GROUP_WIDTH, (g + 1) * A_GROUP_WIDTH)
        w_groups.append(jnp.concatenate([aq[:, sl], ak[:, sl], av[:, sl]], axis=1).astype(BF16))
    a_gain_cols = jnp.concatenate([
        jnp.tile(a_q_gain, A_HEADS_PER_GROUP), jnp.tile(a_k_gain, A_HEADS_PER_GROUP), ones(A_GROUP_WIDTH),
    ]).reshape(1, A_PACK_WIDTH)
    return w_iv, iv_gain_cols, w_qk, qk_gain_cols, w_idx, idx_gain_row, w_groups, a_gain_cols


def kernel(x, c, positions, w_ada, b_ada, w_in, a_q_gain, a_k_gain, b_q_gain, b_k_gain, idx_k_gain,
           w_gate, b_gate, w_proj_a, w_proj_b, w_out, w_up, w_down):
    b, s, d = x.shape
    depth = w_ada.shape[0]
    n = b * s
    topk = min(IDX_TOPK, s // 4)
    assert s % (SEL_TK * SEL_GROUP) == 0 and SEL_TK // 2 >= topk and d % COL_TILE == 0 and s // LANES < 2 ** 15

    tabs = _rope_tables(positions)
    c128, s128, c64, s64 = tabs
    mods = _ada(c, w_ada, b_ada)
    xf = x.reshape(n, d)

    for l in range(depth):
        mod = mods[l]
        w_iv, iv_gain_cols, w_qk, qk_gain_cols, w_idx, idx_gain_row, w_groups, a_gain_cols = _pack_in_weights(
            w_in[l], a_q_gain[l], a_k_gain[l], b_q_gain[l], b_k_gain[l], idx_k_gain[l])

        h = _normmod(xf, mod, s, 1, 0)
        z_iv = _bproj(h, w_iv, iv_gain_cols, tabs, IV_EPILOGUES, "proj_iq_bv").reshape(b, s, -1)
        z_qk = _bproj(h, w_qk, qk_gain_cols, tabs, QK_EPILOGUES, "proj_bq_bk").reshape(b, s, -1)
        ik, iw = _idxproj(h, w_idx, idx_gain_row, c64, s64)

        a_outs, a_lses = [], []
        for g, (window, dilation) in enumerate(A_GROUPS):
            qkv = _aproj(h, w_groups[g], a_gain_cols, tabs, b, s, dilation)
            o, lse = _dilated(qkv, window)
            a_outs.append(o)
            a_lses.append(lse)

        bias5 = _select(z_iv, iw, ik.reshape(b, s, IDX_DIM), topk)
        o_b = _dsa_attention(z_qk, z_iv, bias5).reshape(n, B_WIDTH)

        merged = _merge(a_outs, a_lses, o_b, h, w_gate[l].astype(BF16), b_gate[l].reshape(1, 2 * d),
                        w_proj_a[l].astype(BF16), w_proj_b[l].astype(BF16), s)
        xf, h2 = _outproj(merged, w_out[l].astype(BF16), xf, mod, s)
        xf = _ffn(h2, w_up[l].astype(BF16), w_down[l].astype(BF16), xf, mod, s)

    return xf.reshape(b, s, d)
```

```python
import functools
import math

import jax
import jax.numpy as jnp
from jax import lax
from jax.experimental import pallas as pl
from jax.experimental.pallas import tpu as pltpu

F32 = jnp.float32
BF16 = jnp.bfloat16

HEAD_DIM = 128
LANES = 128
A_GROUPS = ((128, 1), (512, 4), (2048, 16))
A_HEADS_PER_GROUP = 4
A_GROUP_WIDTH = A_HEADS_PER_GROUP * HEAD_DIM
A_WIDTH = len(A_GROUPS) * A_GROUP_WIDTH
B_HEADS = 8
B_WIDTH = B_HEADS * HEAD_DIM
IDX_HEADS = 16
IDX_DIM = 64
IDX_WIDTH = IDX_HEADS * IDX_DIM
IDX_TOPK = 256
ROPE_THETA = 10000.0
EPS = 1e-6
N_MOD = 6
NEG = -1e30
LOG2_E = 1.4426950408889634
DSA_Q_SCALE = HEAD_DIM ** -0.5 * LOG2_E

COL_TILE = 512
MXU_COLS = 256
EPI_ROPE64, EPI_QK, EPI_PLAIN = 0, 1, 2
IV_EPILOGUES = (EPI_ROPE64,) * (IDX_WIDTH // MXU_COLS) + (EPI_PLAIN,) * (B_WIDTH // MXU_COLS)
QK_EPILOGUES = (EPI_QK,) * (2 * B_WIDTH // MXU_COLS)
A_EPILOGUES = (EPI_QK,) * (2 * A_GROUP_WIDTH // MXU_COLS) + (EPI_PLAIN,) * (A_GROUP_WIDTH // MXU_COLS)
A_PACK_WIDTH = 3 * A_GROUP_WIDTH

SEL_TQ = 128
SEL_TK = 512
SEL_GROUP = 4
SEL_UNTESTED_STEPS = (10, 9)
VMEM_LIMIT = 52 * 1024 * 1024


def _params(*sem):
    return pltpu.CompilerParams(dimension_semantics=sem, vmem_limit_bytes=VMEM_LIMIT)


def _pick(n, pref):
    t = pref
    while n % t:
        t //= 2
    return t


def _rms(x, width):
    return x * lax.rsqrt(jnp.sum(x * x, axis=-1, keepdims=True) * (1.0 / width) + EPS)


def _swap_half64(y):
    lane = lax.broadcasted_iota(jnp.int32, y.shape, 1)
    return jnp.where((lane & 63) < 32, pltpu.roll(y, 96, 1), pltpu.roll(y, 32, 1))


def _rope_tables_kernel(pos_ref, f128_ref, g128_ref, f64_ref, g64_ref, c128_ref, s128_ref, c64_ref, s64_ref):
    pos = pos_ref[...]
    a = pos * f128_ref[...]
    c128_ref[...] = jnp.cos(a)
    s128_ref[...] = jnp.sin(a) * g128_ref[...]
    a = pos * f64_ref[...]
    c64_ref[...] = jnp.cos(a)
    s64_ref[...] = jnp.sin(a) * g64_ref[...]


def _rope_tables(positions):
    n = positions.size
    pos = positions.reshape(n, 1).astype(F32)

    def freq(d):
        half = d // 2
        inv = jnp.power(ROPE_THETA, -jnp.arange(half, dtype=F32) * 2.0 / d)
        f = jnp.tile(jnp.concatenate([inv, inv]), LANES // d)
        g = jnp.tile(jnp.concatenate([-jnp.ones((half,), F32), jnp.ones((half,), F32)]), LANES // d)
        return f.reshape(1, LANES), g.reshape(1, LANES)

    f128, g128 = freq(HEAD_DIM)
    f64, g64 = freq(IDX_DIM)
    tm = _pick(n, 1024)
    row = pl.BlockSpec((1, LANES), lambda i: (0, 0))
    tab = pl.BlockSpec((tm, LANES), lambda i: (i, 0))
    return pl.pallas_call(
        _rope_tables_kernel,
        grid=(n // tm,),
        in_specs=[pl.BlockSpec((tm, 1), lambda i: (i, 0)), row, row, row, row],
        out_specs=[tab, tab, tab, tab],
        out_shape=[jax.ShapeDtypeStruct((n, LANES), F32)] * 4,
        compiler_params=_params("parallel"),
        name="rope_tables",
    )(pos, f128, g128, f64, g64)


def _ada_kernel(c_ref, w_ref, b_ref, o_ref):
    c = c_ref[...]
    act = (c * jax.nn.sigmoid(c)).astype(BF16)
    o_ref[...] = jnp.dot(act, w_ref[...].astype(BF16), preferred_element_type=F32) + b_ref[...]


def _ada(c, w_ada, b_ada):
    depth, d, n6 = w_ada.shape
    b = c.shape[0]
    rows = 8
    c_pad = jnp.zeros((rows, d), F32).at[:b].set(c)
    tn = _pick(n6, 1024)
    out = pl.pallas_call(
        _ada_kernel,
        grid=(depth, n6 // tn),
        in_specs=[
            pl.BlockSpec((rows, d), lambda l, j: (0, 0)),
            pl.BlockSpec((None, d, tn), lambda l, j: (l, 0, j)),
            pl.BlockSpec((None, 1, tn), lambda l, j: (l, 0, j)),
        ],
        out_specs=pl.BlockSpec((None, rows, tn), lambda l, j: (l, 0, j)),
        out_shape=jax.ShapeDtypeStruct((depth, rows, n6), F32),
        compiler_params=_params("parallel", "parallel"),
        name="adaln",
    )(c_pad, w_ada, b_ada.reshape(depth, 1, n6))
    return out[:, :b].reshape(depth, b, N_MOD, 1, d)


def _mod_spec(d, which, tiles_per_batch):
    return pl.BlockSpec((None, None, 1, d), lambda i, *_: (i // tiles_per_batch, which, 0, 0))


def _normmod_kernel(x_ref, sc_ref, sh_ref, o_ref):
    x = x_ref[...]
    y = _rms(x, x.shape[-1])
    o_ref[...] = (y * (1.0 + sc_ref[...]) + sh_ref[...]).astype(o_ref.dtype)


def _normmod(x, mod, seq, which_scale, which_shift):
    n, d = x.shape
    tm = _pick(seq, 512)
    tpb = seq // tm
    return pl.pallas_call(
        _normmod_kernel,
        grid=(n // tm,),
        in_specs=[pl.BlockSpec((tm, d), lambda i: (i, 0)), _mod_spec(d, which_scale, tpb), _mod_spec(d, which_shift, tpb)],
        out_specs=pl.BlockSpec((tm, d), lambda i: (i, 0)),
        out_shape=jax.ShapeDtypeStruct((n, d), BF16),
        compiler_params=_params("parallel"),
        name="normmod",
    )(x, mod, mod)


def _proj_kernel(h_ref, w_ref, g_ref, c128_ref, s128_ref, c64_ref, s64_ref, o_ref, z_ref, *, epilogues, streams):
    h = h_ref[...]
    per = h.shape[0] // streams
    for t, kind in enumerate(epilogues):
        z = jnp.dot(h, w_ref[:, t * MXU_COLS:(t + 1) * MXU_COLS], preferred_element_type=F32)
        for c in range(MXU_COLS // LANES):
            slab = t * (MXU_COLS // LANES) + c
            cols = slice(slab * LANES, (slab + 1) * LANES)
            y = z[:, c * LANES:(c + 1) * LANES]
            if kind == EPI_QK:
                y = _rms(y, HEAD_DIM) * g_ref[:, cols]
                y = y * c128_ref[...] + pltpu.roll(y, HEAD_DIM // 2, 1) * s128_ref[...]
            elif kind == EPI_ROPE64:
                y = y * c64_ref[...] + _swap_half64(y) * s64_ref[...]
            if streams == 1:
                o_ref[..., cols] = y.astype(o_ref.dtype).reshape(o_ref.shape[:-1] + (LANES,))
            else:
                z_ref[slab] = y
                for p in range(streams):
                    o_ref[p, :, cols] = z_ref[slab, pl.ds(p, per, stride=streams), :].astype(o_ref.dtype)


def _proj_call(h, w, gain_cols, tabs, epilogues, out_spec, out_shape, tm, streams, name):
    d, width = w.shape
    assert width == len(epilogues) * MXU_COLS
    tab = pl.BlockSpec((tm, LANES), lambda i: (i, 0))
    slabs = width // LANES if streams > 1 else 1
    return pl.pallas_call(
        functools.partial(_proj_kernel, epilogues=epilogues, streams=streams),
        grid=(h.shape[0] // tm,),
        in_specs=[
            pl.BlockSpec((tm, d), lambda i: (i, 0)),
            pl.BlockSpec((d, width), lambda i: (0, 0)),
            pl.BlockSpec((1, width), lambda i: (0, 0)),
            tab, tab, tab, tab,
        ],
        out_specs=out_spec,
        out_shape=out_shape,
        scratch_shapes=[pltpu.VMEM((slabs, tm, LANES), F32)],
        compiler_params=_params("parallel"),
        name=name,
    )(h, w, gain_cols, *tabs)


def _bproj(h, w, gain_cols, tabs, epilogues, name):
    n = h.shape[0]
    tm = _pick(n, 512)
    width = w.shape[1]
    return _proj_call(h, w, gain_cols, tabs, epilogues, pl.BlockSpec((tm, width), lambda i: (i, 0)),
                      jax.ShapeDtypeStruct((n, width), BF16), tm, 1, name)


def _aproj(h, w_group, gain_cols, tabs, batch, seq, dilation):
    r = dilation
    tm = _pick(seq, 512)
    tpb = seq // tm
    assert tm % (r * 16) == 0
    return _proj_call(h, w_group, gain_cols, tabs, A_EPILOGUES,
                      pl.BlockSpec((None, r, tm // r, A_PACK_WIDTH), lambda i: (i // tpb, 0, i % tpb, 0)),
                      jax.ShapeDtypeStruct((batch, r, seq // r, A_PACK_WIDTH), BF16), tm, r, f"aproj_r{r}")


def _idxproj_kernel(h_ref, w_ref, g_ref, c64_ref, s64_ref, ik_ref, iw_ref):
    z = jnp.dot(h_ref[...], w_ref[...], preferred_element_type=F32)
    lane = lax.broadcasted_iota(jnp.int32, z.shape, 1)
    is_k = lane < IDX_DIM
    zk = jnp.where(is_k, z, 0.0)
    y = _rms(zk, IDX_DIM) * g_ref[...]
    y = y * c64_ref[...] + _swap_half64(y) * s64_ref[...]
    ik_ref[...] = y[:, :IDX_DIM].astype(ik_ref.dtype)
    iw_ref[...] = z.T[IDX_DIM:IDX_DIM + IDX_HEADS, :] * (IDX_HEADS ** -0.5 * IDX_DIM ** -0.5)


def _idxproj(h, w_idx, gain_row, c64, s64):
    n, d = h.shape
    tm = _pick(n, 512)
    tab = pl.BlockSpec((tm, LANES), lambda i: (i, 0))
    return pl.pallas_call(
        _idxproj_kernel,
        grid=(n // tm,),
        in_specs=[
            pl.BlockSpec((tm, d), lambda i: (i, 0)),
            pl.BlockSpec((d, LANES), lambda i: (0, 0)),
            pl.BlockSpec((1, LANES), lambda i: (0, 0)),
            tab, tab,
        ],
        out_specs=[pl.BlockSpec((tm, IDX_DIM), lambda i: (i, 0)), pl.BlockSpec((IDX_HEADS, tm), lambda i: (0, i))],
        out_shape=[jax.ShapeDtypeStruct((n, IDX_DIM), BF16), jax.ShapeDtypeStruct((IDX_HEADS, n), F32)],
        compiler_params=_params("parallel"),
        name="idxproj",
    )(h, w_idx, gain_row, c64, s64)


def _dilated_kernel(q_ref, kc_ref, kp_ref, vc_ref, vp_ref, o_ref, lse_ref, *, tq):
    i = pl.program_id(2)
    blk = LANES
    scale = HEAD_DIM ** -0.5
    nkeys = blk + tq
    row = lax.broadcasted_iota(jnp.int32, (tq, nkeys), 0)
    col = lax.broadcasted_iota(jnp.int32, (tq, nkeys), 1)
    dist = row + blk - col
    band = jnp.where(dist >= 0, jnp.where(dist <= blk, 0.0, -jnp.inf), -jnp.inf)
    first = jnp.where(col >= blk, 0.0, -jnp.inf)
    bias = band + jnp.where(i > 0, 0.0, first)
    lane = lax.broadcasted_iota(jnp.int32, (tq, LANES), 1)
    nt = (((1,), (1,)), ((), ()))

    def scores(hh):
        cols = slice(hh * HEAD_DIM, (hh + 1) * HEAD_DIM)
        keys = jnp.concatenate([kp_ref[:, cols], kc_ref[:, cols]], axis=0)
        return lax.dot_general(q_ref[:, cols], keys, nt, preferred_element_type=F32) * scale + bias

    lse_tile = jnp.zeros((tq, LANES), F32)
    s_next = scores(0)
    for hh in range(A_HEADS_PER_GROUP):
        cols = slice(hh * HEAD_DIM, (hh + 1) * HEAD_DIM)
        s = s_next
        if hh + 1 < A_HEADS_PER_GROUP:
            s_next = scores(hh + 1)
        m = jnp.max(s, axis=1, keepdims=True)
        e = jnp.exp(s - m)
        den = jnp.sum(e, axis=1, keepdims=True)
        values = jnp.concatenate([vp_ref[:, cols], vc_ref[:, cols]], axis=0)
        acc = jnp.dot(e.astype(BF16), values, preferred_element_type=F32)
        o_ref[:, cols] = acc / den
        lse_tile = jnp.where(lane == hh, m + jnp.log(den), lse_tile)
    lse_ref[...] = lse_tile


def _dilated(qkv, window):
    b, r, m, _ = qkv.shape
    assert window // r == LANES and m % LANES == 0
    tq = _pick(m, 512)
    nsub = tq // LANES

    def cur(tile):
        return pl.BlockSpec((None, None, tq, COL_TILE), lambda bb, p, i: (bb, p, i, tile))

    def prev(tile):
        return pl.BlockSpec((None, None, LANES, COL_TILE),
                            lambda bb, p, i: (bb, p, jnp.maximum(i * nsub - 1, 0), tile))

    return pl.pallas_call(
        functools.partial(_dilated_kernel, tq=tq),
        grid=(b, r, m // tq),
        in_specs=[cur(0), cur(1), prev(1), cur(2), prev(2)],
        out_specs=[
            pl.BlockSpec((None, None, tq, A_GROUP_WIDTH), lambda bb, p, i: (bb, p, i, 0)),
            pl.BlockSpec((None, None, tq, LANES), lambda bb, p, i: (bb, p, i, 0)),
        ],
        out_shape=[jax.ShapeDtypeStruct((b, r, m, A_GROUP_WIDTH), F32), jax.ShapeDtypeStruct((b, r, m, LANES), F32)],
        compiler_params=_params("parallel", "parallel", "parallel"),
        name=f"dilated_r{r}",
    )(qkv, qkv, qkv, qkv, qkv)


def _key_to_float(key):
    bits = jnp.where(key >= 0, key, key ^ 0x7FFFFFFF)
    return lax.bitcast_convert_type(bits, F32)


def _float_to_key(x):
    bits = lax.bitcast_convert_type(x, jnp.int32)
    return jnp.where(bits >= 0, bits, bits ^ 0x7FFFFFFF)


PACK16 = 16


def _select_kernel(iq_ref, wt_ref, k_ref, bias_ref, qt_ref, sc_ref, hi_ref, lo_ref, gm_ref, *, topk):
    i = pl.program_id(1)
    tq, tk = SEL_TQ, SEL_TK
    nk = sc_ref.shape[0]
    nkb = (i * tq + tq + tk - 1) // tk
    slabs = tk // PACK16

    q_t = iq_ref[...].astype(F32).T
    for h in range(IDX_HEADS):
        qt_ref[:, h * tq:(h + 1) * tq] = q_t[h * IDX_DIM:(h + 1) * IDX_DIM, :].astype(qt_ref.dtype)

    kpos = lax.broadcasted_iota(jnp.int32, (tk, tq), 0)
    qpos = i * tq + lax.broadcasted_iota(jnp.int32, (tk, tq), 1)

    def rows(x):
        return jnp.concatenate([x] * slabs, axis=0)

    def score_block(kb, carry):
        keys = k_ref[pl.ds(pl.multiple_of(kb * tk, tk), tk), :]
        logits = jnp.dot(keys, qt_ref[...], preferred_element_type=F32)
        acc = jnp.zeros((tk, tq), F32)
        for h in range(IDX_HEADS):
            acc = acc + wt_ref[h:h + 1, :] * jnp.maximum(logits[:, h * tq:(h + 1) * tq], 0.0)
        masked = jnp.where(kpos + kb * tk <= qpos, acc, -jnp.inf)
        sc_ref[kb] = masked
        hi_ref[kb] = (_float_to_key(masked) >> 16).astype(jnp.int16)
        gm_ref[...] = jnp.maximum(gm_ref[...], jnp.maximum(masked[:tk // 2], masked[tk // 2:]))
        return carry

    gm_ref[...] = jnp.full(gm_ref.shape, -jnp.inf, F32)
    lax.fori_loop(0, nkb, score_block, 0)

    def reps(x):
        return jnp.broadcast_to(x, (PACK16, tq))

    bound_lo = reps(_float_to_key(jnp.min(gm_ref[...], axis=0, keepdims=True)) >> 16)
    bound_hi = reps(_float_to_key(jnp.max(gm_ref[...], axis=0, keepdims=True)) >> 16) + 1

    ngr = (nkb + SEL_GROUP - 1) // SEL_GROUP
    nkp = ngr * SEL_GROUP

    def pad_block(kb, carry):
        neg_inf = jnp.full((tk, tq), -jnp.inf, F32)
        sc_ref[kb] = neg_inf
        hi_ref[kb] = (_float_to_key(neg_inf) >> 16).astype(jnp.int16)
        return carry

    lax.fori_loop(nkb, nkp, pad_block, 0)

    def bisect16(ref, need, lo0, hi0, high_digit, untested_steps):
        def count_ge(t):
            t16 = t.astype(jnp.int16)

            def body(g, accs):
                accs = list(accs)
                for j in range(SEL_GROUP):
                    for r in range(slabs):
                        blk = ref[g * SEL_GROUP + j, r * PACK16:(r + 1) * PACK16, :]
                        hit = jnp.where(blk >= t16, jnp.int16(1), jnp.int16(0))
                        accs[r % len(accs)] = accs[r % len(accs)] + hit
                return tuple(accs)

            zero = jnp.zeros((PACK16, tq), jnp.int16)
            accs = lax.fori_loop(0, ngr, body, (zero,) * 4)
            acc = (accs[0] + accs[1]) + (accs[2] + accs[3])
            cnt = jnp.sum(acc.astype(jnp.int32), axis=0, keepdims=True)
            return jnp.broadcast_to(cnt, (PACK16, tq))

        def open_brackets(carry):
            _, lo, hi, _, _ = carry
            return jnp.max(hi - lo) > 1

        def step(carry):
            it, lo, hi, below, above = carry
            mid = (lo + hi) >> 1
            if high_digit:
                log_lo = jnp.log(below.astype(F32))
                frac = (log_lo - math.log(need - 0.5)) / (log_lo - jnp.log(jnp.maximum(above.astype(F32), 0.5)))
                frac = jnp.minimum(jnp.maximum(frac, 0.05), 0.95)
                v_lo, v_hi = _key_to_float((lo << 16) | 0xFFFF), _key_to_float(hi << 16)
                by_value = _float_to_key(v_lo + (v_hi - v_lo) * frac) >> 16
                by_value = jnp.minimum(jnp.maximum(by_value, lo + 1), hi - 1)
                mid = jnp.where(it % 4 == 3, mid, by_value)
            mid = jnp.where(hi - lo > 1, mid, lo)
            cnt = count_ge(mid)
            ge = cnt >= need
            lo, hi = jnp.where(ge, mid, lo), jnp.where(ge, hi, mid)
            below, above = jnp.where(ge, cnt, below), jnp.where(ge, above, cnt)
            if not high_digit:
                hi = jnp.where(cnt == need, mid + 1, hi)
            return it + 1, lo, hi, below, above

        everything = jnp.zeros((PACK16, tq), jnp.int32) + nkp * tk
        carry = (jnp.int32(0), lo0, hi0, everything, jnp.zeros((PACK16, tq), jnp.int32))
        carry = lax.fori_loop(0, untested_steps, lambda _, c: step(c), carry)
        _, lo, _, _, above = lax.while_loop(open_brackets, step, carry)
        return lo, above

    key_hi, above = bisect16(hi_ref, topk, bound_lo, bound_hi, True, SEL_UNTESTED_STEPS[0])
    key_hi_rows = rows(key_hi)

    def low_digits(kb, carry):
        key = _float_to_key(sc_ref[kb])
        low = (key & 0xFFFF) - 2 ** 15
        lo_ref[kb] = jnp.where((key >> 16) == key_hi_rows, low, -(2 ** 15)).astype(jnp.int16)
        return carry

    lax.fori_loop(0, nkp, low_digits, 0)
    digit_lo = jnp.full((PACK16, tq), -(2 ** 15), jnp.int32)
    digit_hi = jnp.full((PACK16, tq), 2 ** 15, jnp.int32)
    key_lo, _ = bisect16(lo_ref, topk - above, digit_lo, digit_hi, False, SEL_UNTESTED_STEPS[1])
    thr_rows = rows(_key_to_float((key_hi << 16) | (key_lo + 2 ** 15)))

    def write_block(kb, carry):
        picked = jnp.where(sc_ref[kb] >= thr_rows, 0.0, NEG)
        picked = jnp.where(kpos + kb * tk <= qpos, picked, NEG)
        bias_ref[kb] = picked.T.astype(bias_ref.dtype)
        return carry

    lax.fori_loop(0, nkb, write_block, 0)

    def fill_block(kb, carry):
        bias_ref[kb] = jnp.full((tq, tk), NEG, bias_ref.dtype)
        return carry

    lax.fori_loop(nkb, nk, fill_block, 0)


def _select(z_iv, iw_t, ik, topk):
    b, s, _ = z_iv.shape
    tq, tk = SEL_TQ, SEL_TK
    nq, nk = s // tq, s // tk
    return pl.pallas_call(
        functools.partial(_select_kernel, topk=topk),
        grid=(b, nq),
        in_specs=[
            pl.BlockSpec((None, tq, IDX_WIDTH), lambda bb, i: (bb, i, 0)),
            pl.BlockSpec((IDX_HEADS, tq), lambda bb, i: (0, bb * nq + i)),
            pl.BlockSpec((None, s, IDX_DIM), lambda bb, i: (bb, 0, 0)),
        ],
        out_specs=pl.BlockSpec((None, None, nk, tq, tk), lambda bb, i: (bb, i, 0, 0, 0)),
        out_shape=jax.ShapeDtypeStruct((b, nq, nk, tq, tk), BF16),
        scratch_shapes=[
            pltpu.VMEM((IDX_DIM, IDX_HEADS * tq), BF16),
            pltpu.VMEM((nk, tk, tq), F32),
            pltpu.VMEM((nk, tk, tq), jnp.int16),
            pltpu.VMEM((nk, tk, tq), jnp.int16),
            pltpu.VMEM((tk // 2, tq), F32),
        ],
        compiler_params=_params("parallel", "parallel"),
        name="dsa_select",
    )(z_iv, iw_t, ik)


def _dsa_kernel(q_ref, k_ref, v_ref, b_ref, o_ref, m_ref, l_ref, acc_ref, *, tq, tk):
    i = pl.program_id(1)
    kb = pl.program_id(2)
    last = (i * tq + tq - 1) // tk
    nt = (((1,), (1,)), ((), ()))

    @pl.when(kb == 0)
    def _():
        m_ref[...] = jnp.full(m_ref.shape, NEG, F32)
        l_ref[...] = jnp.zeros(l_ref.shape, F32)
        acc_ref[...] = jnp.zeros(acc_ref.shape, F32)

    @pl.when(kb <= last)
    def _():
        bias = b_ref[...].reshape(tq, tk).astype(F32)
        rep = tk // LANES

        def scores(h):
            cols = slice(h * HEAD_DIM, (h + 1) * HEAD_DIM)
            return lax.dot_general(q_ref[:, cols], k_ref[:, cols], nt, preferred_element_type=F32) + bias

        s_next = scores(0)
        for h in range(B_HEADS):
            cols = slice(h * HEAD_DIM, (h + 1) * HEAD_DIM)
            s = s_next
            if h + 1 < B_HEADS:
                s_next = scores(h + 1)
            m_prev = m_ref[h]
            m_new = jnp.maximum(m_prev, jnp.max(s, axis=1, keepdims=True))
            alpha = jnp.exp2(m_prev - m_new)
            p = jnp.exp2(s - jnp.concatenate([m_new] * rep, axis=1))
            l_ref[h] = alpha * l_ref[h] + jnp.sum(p, axis=1, keepdims=True)
            acc_ref[:, cols] = alpha * acc_ref[:, cols] + jnp.dot(p.astype(BF16), v_ref[:, cols],
                                                                  preferred_element_type=F32)
            m_ref[h] = m_new

    @pl.when(kb == last)
    def _():
        for h in range(B_HEADS):
            cols = slice(h * HEAD_DIM, (h + 1) * HEAD_DIM)
            o_ref[:, cols] = (acc_ref[:, cols] / l_ref[h]).astype(o_ref.dtype)


def _dsa_attention(z_qk, z_iv, bias5):
    b, s, _ = z_qk.shape
    tk = SEL_TK
    tq = _pick(s, 512)
    sub = tq // SEL_TQ
    assert IDX_WIDTH == B_WIDTH

    def last(i):
        return (i * tq + tq - 1) // tk

    return pl.pallas_call(
        functools.partial(_dsa_kernel, tq=tq, tk=tk),
        grid=(b, s // tq, s // tk),
        in_specs=[
            pl.BlockSpec((None, tq, B_WIDTH), lambda bb, i, kb: (bb, i, 0)),
            pl.BlockSpec((None, tk, B_WIDTH), lambda bb, i, kb: (bb, jnp.minimum(kb, last(i)), 1)),
            pl.BlockSpec((None, tk, B_WIDTH), lambda bb, i, kb: (bb, jnp.minimum(kb, last(i)), 1)),
            pl.BlockSpec((None, sub, None, SEL_TQ, tk), lambda bb, i, kb: (bb, i, jnp.minimum(kb, last(i)), 0, 0)),
        ],
        out_specs=pl.BlockSpec((None, tq, B_WIDTH), lambda bb, i, kb: (bb, i, 0)),
        out_shape=jax.ShapeDtypeStruct((b, s, B_WIDTH), BF16),
        scratch_shapes=[
            pltpu.VMEM((B_HEADS, tq, LANES), F32),
            pltpu.VMEM((B_HEADS, tq, LANES), F32),
            pltpu.VMEM((tq, B_WIDTH), F32),
        ],
        compiler_params=_params("parallel", "parallel", "arbitrary"),
        name="dsa_attention",
    )(z_qk, z_qk, z_iv, bias5)


def _to_token_order(dst_ref, first, src_ref):
    r, per, width = src_ref.shape
    for c in range(width // LANES):
        cols = slice(c * LANES, (c + 1) * LANES)
        if r == 1:
            dst_ref[first + c] = src_ref[0, :, cols]
        else:
            for p in range(r):
                dst_ref[first + c, pl.ds(p, per, stride=r), :] = src_ref[p, :, cols]


def _merge_kernel(o1_ref, o2_ref, o3_ref, l1_ref, l2_ref, l3_ref, ob_ref, h_ref,
                  wga_ref, wgb_ref, bga_ref, bgb_ref, wpa_ref, wpb_ref, out_ref, oa_ref, ot_ref, lt_ref):
    @pl.when(pl.program_id(1) == 0)
    def _():
        nh = A_HEADS_PER_GROUP
        for g, (o_ref, l_ref) in enumerate(((o1_ref, l1_ref), (o2_ref, l2_ref), (o3_ref, l3_ref))):
            _to_token_order(ot_ref, g * nh, o_ref)
            _to_token_order(lt_ref, g, l_ref)
        l1, l2, l3 = lt_ref[0], lt_ref[1], lt_ref[2]
        mx = jnp.maximum(jnp.maximum(l1, l2), l3)
        e1, e2, e3 = jnp.exp(l1 - mx), jnp.exp(l2 - mx), jnp.exp(l3 - mx)
        tot = e1 + e2 + e3
        w1, w2, w3 = e1 / tot, e2 / tot, e3 / tot
        for hh in range(nh):
            oa = (w1[:, hh:hh + 1] * ot_ref[hh] + w2[:, hh:hh + 1] * ot_ref[nh + hh]
                  + w3[:, hh:hh + 1] * ot_ref[2 * nh + hh])
            oa_ref[:, hh * HEAD_DIM:(hh + 1) * HEAD_DIM] = oa.astype(oa_ref.dtype)

    h = h_ref[...]
    ga = jax.nn.sigmoid(jnp.dot(h, wga_ref[...], preferred_element_type=F32) + bga_ref[...])
    gb = jax.nn.sigmoid(jnp.dot(h, wgb_ref[...], preferred_element_type=F32) + bgb_ref[...])
    pa = jnp.dot(oa_ref[...], wpa_ref[...], preferred_element_type=F32)
    pb = jnp.dot(ob_ref[...], wpb_ref[...], preferred_element_type=F32)
    out_ref[...] = (ga * pa + gb * pb).astype(out_ref.dtype)


def _merge(outs, lses, o_b, h, w_gate, b_gate, w_proj_a, w_proj_b, seq):
    n, d = h.shape
    tm = _pick(seq, 512)
    tpb = seq // tm
    tn = _pick(d, COL_TILE)
    nj = d // tn
    row = lambda width: pl.BlockSpec((tm, width), lambda i, j: (i, 0))

    def streams(arr):
        r, width = arr.shape[1], arr.shape[3]
        return pl.BlockSpec((None, r, tm // r, width), lambda i, j: (i // tpb, 0, i % tpb, 0))

    return pl.pallas_call(
        _merge_kernel,
        grid=(n // tm, nj),
        in_specs=[
            *[streams(a) for a in outs], *[streams(a) for a in lses],
            row(B_WIDTH), row(d),
            pl.BlockSpec((d, tn), lambda i, j: (0, j)),
            pl.BlockSpec((d, tn), lambda i, j: (0, nj + j)),
            pl.BlockSpec((1, tn), lambda i, j: (0, j)),
            pl.BlockSpec((1, tn), lambda i, j: (0, nj + j)),
            pl.BlockSpec((A_GROUP_WIDTH, tn), lambda i, j: (0, j)),
            pl.BlockSpec((B_WIDTH, tn), lambda i, j: (0, j)),
        ],
        out_specs=pl.BlockSpec((tm, tn), lambda i, j: (i, j)),
        out_shape=jax.ShapeDtypeStruct((n, d), BF16),
        scratch_shapes=[
            pltpu.VMEM((tm, A_GROUP_WIDTH), BF16),
            pltpu.VMEM((len(outs) * A_HEADS_PER_GROUP, tm, HEAD_DIM), F32),
            pltpu.VMEM((len(lses), tm, LANES), F32),
        ],
        compiler_params=_params("parallel", "arbitrary"),
        name="gated_merge",
    )(*outs, *lses, o_b, h, w_gate, w_gate, b_gate, b_gate, w_proj_a, w_proj_b)


def _outproj_kernel(mg_ref, w_ref, x_ref, g_ref, sc_ref, sh_ref, xo_ref, ho_ref):
    mix = jnp.dot(mg_ref[...], w_ref[...], preferred_element_type=F32)
    x = x_ref[...] + g_ref[...] * mix
    xo_ref[...] = x
    ho_ref[...] = (_rms(x, x.shape[-1]) * (1.0 + sc_ref[...]) + sh_ref[...]).astype(ho_ref.dtype)


def _outproj(merged, w_out, x, mod, seq):
    n, d = x.shape
    tm = _pick(seq, 256)
    tpb = seq // tm
    row = pl.BlockSpec((tm, d), lambda i: (i, 0))
    return pl.pallas_call(
        _outproj_kernel,
        grid=(n // tm,),
        in_specs=[row, pl.BlockSpec((d, d), lambda i: (0, 0)), row,
                  _mod_spec(d, 2, tpb), _mod_spec(d, 4, tpb), _mod_spec(d, 3, tpb)],
        out_specs=[row, row],
        out_shape=[jax.ShapeDtypeStruct((n, d), F32), jax.ShapeDtypeStruct((n, d), BF16)],
        compiler_params=_params("parallel"),
        name="outproj",
    )(merged, w_out, x, mod, mod, mod)


def _ffn_kernel(h_ref, wu_ref, wd_ref, x_ref, g_ref, o_ref, acc_ref):
    c = pl.program_id(1)

    @pl.when(c == 0)
    def _():
        acc_ref[...] = jnp.zeros(acc_ref.shape, F32)

    u = jnp.maximum(jnp.dot(h_ref[...], wu_ref[...], preferred_element_type=F32), 0.0)
    acc_ref[...] += jnp.dot((u * u).astype(BF16), wd_ref[...], preferred_element_type=F32)

    @pl.when(c == pl.num_programs(1) - 1)
    def _():
        o_ref[...] = x_ref[...] + g_ref[...] * acc_ref[...]


def _ffn(h2, w_up, w_down, x, mod, seq):
    n, d = x.shape
    hidden = w_up.shape[1]
    tm = _pick(seq, 512)
    tc = _pick(hidden, 512)
    tpb = seq // tm
    row = pl.BlockSpec((tm, d), lambda i, c: (i, 0))
    return pl.pallas_call(
        _ffn_kernel,
        grid=(n // tm, hidden // tc),
        in_specs=[row, pl.BlockSpec((d, tc), lambda i, c: (0, c)), pl.BlockSpec((tc, d), lambda i, c: (c, 0)),
                  row, _mod_spec(d, 5, tpb)],
        out_specs=row,
        out_shape=jax.ShapeDtypeStruct((n, d), F32),
        scratch_shapes=[pltpu.VMEM((tm, d), F32)],
        compiler_params=_params("parallel", "arbitrary"),
        name="ffn",
    )(h2, w_up, w_down, x, mod)


def _pack_in_weights(w_in, a_q_gain, a_k_gain, b_q_gain, b_k_gain, idx_k_gain):
    d = w_in.shape[0]
    sizes = (A_WIDTH, A_WIDTH, A_WIDTH, B_WIDTH, B_WIDTH, B_WIDTH, IDX_WIDTH, IDX_DIM, IDX_HEADS)
    parts, off = [], 0
    for sz in sizes:
        parts.append(w_in[:, off:off + sz])
        off += sz
    aq, ak, av, bq, bk, bv, iq, ik, iw = parts
    w_iv = jnp.concatenate([iq, bv], axis=1).astype(BF16)
    w_qk = jnp.concatenate([bq, bk], axis=1).astype(BF16)
    w_idx = jnp.concatenate([ik, iw, jnp.zeros((d, LANES - IDX_DIM - IDX_HEADS), w_in.dtype)], axis=1).astype(BF16)
    ones = lambda width: jnp.ones((width,), F32)
    iv_gain_cols = ones(IDX_WIDTH + B_WIDTH).reshape(1, -1)
    qk_gain_cols = jnp.concatenate([jnp.tile(b_q_gain * DSA_Q_SCALE, B_HEADS), jnp.tile(b_k_gain, B_HEADS)]).reshape(1, -1)
    idx_gain_row = jnp.concatenate([idx_k_gain, ones(LANES - IDX_DIM)]).reshape(1, LANES)
    w_groups = []
    for g in range(len(A_GROUPS)):
        sl = slice(g * A_GROUP_WIDTH, (g + 1) * A_GROUP_WIDTH)
        w_groups.append(jnp.concatenate([aq[:, sl], ak[:, sl], av[:, sl]], axis=1).astype(BF16))
    a_gain_cols = jnp.concatenate([
        jnp.tile(a_q_gain, A_HEADS_PER_GROUP), jnp.tile(a_k_gain, A_HEADS_PER_GROUP), ones(A_GROUP_WIDTH),
    ]).reshape(1, A_PACK_WIDTH)
    return w_iv, iv_gain_cols, w_qk, qk_gain_cols, w_idx, idx_gain_row, w_groups, a_gain_cols


def kernel(x, c, positions, w_ada, b_ada, w_in, a_q_gain, a_k_gain, b_q_gain, b_k_gain, idx_k_gain,
           w_gate, b_gate, w_proj_a, w_proj_b, w_out, w_up, w_down):
    b, s, d = x.shape
    depth = w_ada.shape[0]
    n = b * s
    topk = min(IDX_TOPK, s // 4)
    assert s % (SEL_TK * SEL_GROUP) == 0 and SEL_TK // 2 >= topk and d % COL_TILE == 0 and s // LANES < 2 ** 15

    tabs = _rope_tables(positions)
    c128, s128, c64, s64 = tabs
    mods = _ada(c, w_ada, b_ada)
    xf = x.reshape(n, d)

    for l in range(depth):
        mod = mods[l]
        w_iv, iv_gain_cols, w_qk, qk_gain_cols, w_idx, idx_gain_row, w_groups, a_gain_cols = _pack_in_weights(
            w_in[l], a_q_gain[l], a_k_gain[l], b_q_gain[l], b_k_gain[l], idx_k_gain[l])

        h = _normmod(xf, mod, s, 1, 0)
        z_iv = _bproj(h, w_iv, iv_gain_cols, tabs, IV_EPILOGUES, "proj_iq_bv").reshape(b, s, -1)
        z_qk = _bproj(h, w_qk, qk_gain_cols, tabs, QK_EPILOGUES, "proj_bq_bk").reshape(b, s, -1)
        ik, iw = _idxproj(h, w_idx, idx_gain_row, c64, s64)

        a_outs, a_lses = [], []
        for g, (window, dilation) in enumerate(A_GROUPS):
            qkv = _aproj(h, w_groups[g], a_gain_cols, tabs, b, s, dilation)
            o, lse = _dilated(qkv, window)
            a_outs.append(o)
            a_lses.append(lse)

        bias5 = _select(z_iv, iw, ik.reshape(b, s, IDX_DIM), topk)
        o_b = _dsa_attention(z_qk, z_iv, bias5).reshape(n, B_WIDTH)

        merged = _merge(a_outs, a_lses, o_b, h, w_gate[l].astype(BF16), b_gate[l].reshape(1, 2 * d),
                        w_proj_a[l].astype(BF16), w_proj_b[l].astype(BF16), s)
        xf, h2 = _outproj(merged, w_out[l].astype(BF16), xf, mod, s)
        xf = _ffn(h2, w_up[l].astype(BF16), w_down[l].astype(BF16), xf, mod, s)

    return xf.reshape(b, s, d)
```

```python
import functools
import math

import jax
import jax.numpy as jnp
from jax import lax
from jax.experimental import pallas as pl
from jax.experimental.pallas import tpu as pltpu

F32 = jnp.float32
BF16 = jnp.bfloat16

HEAD_DIM = 128
LANES = 128
A_GROUPS = ((128, 1), (512, 4), (2048, 16))
A_HEADS_PER_GROUP = 4
A_GROUP_WIDTH = A_HEADS_PER_GROUP * HEAD_DIM
A_WIDTH = len(A_GROUPS) * A_GROUP_WIDTH
B_HEADS = 8
B_WIDTH = B_HEADS * HEAD_DIM
IDX_HEADS = 16
IDX_DIM = 64
IDX_WIDTH = IDX_HEADS * IDX_DIM
IDX_TOPK = 256
ROPE_THETA = 10000.0
EPS = 1e-6
N_MOD = 6
NEG = -1e30
LOG2_E = 1.4426950408889634
DSA_Q_SCALE = HEAD_DIM ** -0.5 * LOG2_E

COL_TILE = 512
MXU_COLS = 256
EPI_ROPE64, EPI_QK, EPI_PLAIN = 0, 1, 2
IV_EPILOGUES = (EPI_ROPE64,) * (IDX_WIDTH // MXU_COLS) + (EPI_PLAIN,) * (B_WIDTH // MXU_COLS)
QK_EPILOGUES = (EPI_QK,) * (2 * B_WIDTH // MXU_COLS)
A_EPILOGUES = (EPI_QK,) * (2 * A_GROUP_WIDTH // MXU_COLS) + (EPI_PLAIN,) * (A_GROUP_WIDTH // MXU_COLS)
A_PACK_WIDTH = 3 * A_GROUP_WIDTH

SEL_TQ = 128
SEL_TK = 512
SEL_GROUP = 4
SEL_UNTESTED_STEPS = (10, 9)
VMEM_LIMIT = 52 * 1024 * 1024


def _params(*sem):
    return pltpu.CompilerParams(dimension_semantics=sem, vmem_limit_bytes=VMEM_LIMIT)


def _pick(n, pref):
    t = pref
    while n % t:
        t //= 2
    return t


def _rms(x, width):
    return x * lax.rsqrt(jnp.sum(x * x, axis=-1, keepdims=True) * (1.0 / width) + EPS)


def _swap_half64(y):
    lane = lax.broadcasted_iota(jnp.int32, y.shape, 1)
    return jnp.where((lane & 63) < 32, pltpu.roll(y, 96, 1), pltpu.roll(y, 32, 1))


def _rope_tables_kernel(pos_ref, f128_ref, g128_ref, f64_ref, g64_ref, c128_ref, s128_ref, c64_ref, s64_ref):
    pos = pos_ref[...]
    a = pos * f128_ref[...]
    c128_ref[...] = jnp.cos(a)
    s128_ref[...] = jnp.sin(a) * g128_ref[...]
    a = pos * f64_ref[...]
    c64_ref[...] = jnp.cos(a)
    s64_ref[...] = jnp.sin(a) * g64_ref[...]


def _rope_tables(positions):
    n = positions.size
    pos = positions.reshape(n, 1).astype(F32)

    def freq(d):
        half = d // 2
        inv = jnp.power(ROPE_THETA, -jnp.arange(half, dtype=F32) * 2.0 / d)
        f = jnp.tile(jnp.concatenate([inv, inv]), LANES // d)
        g = jnp.tile(jnp.concatenate([-jnp.ones((half,), F32), jnp.ones((half,), F32)]), LANES // d)
        return f.reshape(1, LANES), g.reshape(1, LANES)

    f128, g128 = freq(HEAD_DIM)
    f64, g64 = freq(IDX_DIM)
    tm = _pick(n, 1024)
    row = pl.BlockSpec((1, LANES), lambda i: (0, 0))
    tab = pl.BlockSpec((tm, LANES), lambda i: (i, 0))
    return pl.pallas_call(
        _rope_tables_kernel,
        grid=(n // tm,),
        in_specs=[pl.BlockSpec((tm, 1), lambda i: (i, 0)), row, row, row, row],
        out_specs=[tab, tab, tab, tab],
        out_shape=[jax.ShapeDtypeStruct((n, LANES), F32)] * 4,
        compiler_params=_params("parallel"),
        name="rope_tables",
    )(pos, f128, g128, f64, g64)


def _ada_kernel(c_ref, w_ref, b_ref, o_ref):
    c = c_ref[...]
    act = (c * jax.nn.sigmoid(c)).astype(BF16)
    o_ref[...] = jnp.dot(act, w_ref[...].astype(BF16), preferred_element_type=F32) + b_ref[...]


def _ada(c, w_ada, b_ada):
    depth, d, n6 = w_ada.shape
    b = c.shape[0]
    rows = 8
    c_pad = jnp.zeros((rows, d), F32).at[:b].set(c)
    tn = _pick(n6, 1024)
    out = pl.pallas_call(
        _ada_kernel,
        grid=(depth, n6 // tn),
        in_specs=[
            pl.BlockSpec((rows, d), lambda l, j: (0, 0)),
            pl.BlockSpec((None, d, tn), lambda l, j: (l, 0, j)),
            pl.BlockSpec((None, 1, tn), lambda l, j: (l, 0, j)),
        ],
        out_specs=pl.BlockSpec((None, rows, tn), lambda l, j: (l, 0, j)),
        out_shape=jax.ShapeDtypeStruct((depth, rows, n6), F32),
        compiler_params=_params("parallel", "parallel"),
        name="adaln",
    )(c_pad, w_ada, b_ada.reshape(depth, 1, n6))
    return out[:, :b].reshape(depth, b, N_MOD, 1, d)


def _mod_spec(d, which, tiles_per_batch):
    return pl.BlockSpec((None, None, 1, d), lambda i, *_: (i // tiles_per_batch, which, 0, 0))


def _normmod_kernel(x_ref, sc_ref, sh_ref, o_ref):
    x = x_ref[...]
    y = _rms(x, x.shape[-1])
    o_ref[...] = (y * (1.0 + sc_ref[...]) + sh_ref[...]).astype(o_ref.dtype)


def _normmod(x, mod, seq, which_scale, which_shift):
    n, d = x.shape
    tm = _pick(seq, 512)
    tpb = seq // tm
    return pl.pallas_call(
        _normmod_kernel,
        grid=(n // tm,),
        in_specs=[pl.BlockSpec((tm, d), lambda i: (i, 0)), _mod_spec(d, which_scale, tpb), _mod_spec(d, which_shift, tpb)],
        out_specs=pl.BlockSpec((tm, d), lambda i: (i, 0)),
        out_shape=jax.ShapeDtypeStruct((n, d), BF16),
        compiler_params=_params("parallel"),
        name="normmod",
    )(x, mod, mod)


def _proj_kernel(h_ref, w_ref, g_ref, c128_ref, s128_ref, c64_ref, s64_ref, o_ref, z_ref, *, epilogues, streams):
    h = h_ref[...]
    per = h.shape[0] // streams
    for t, kind in enumerate(epilogues):
        z = jnp.dot(h, w_ref[:, t * MXU_COLS:(t + 1) * MXU_COLS], preferred_element_type=F32)
        for c in range(MXU_COLS // LANES):
            slab = t * (MXU_COLS // LANES) + c
            cols = slice(slab * LANES, (slab + 1) * LANES)
            y = z[:, c * LANES:(c + 1) * LANES]
            if kind == EPI_QK:
                y = _rms(y, HEAD_DIM) * g_ref[:, cols]
                y = y * c128_ref[...] + pltpu.roll(y, HEAD_DIM // 2, 1) * s128_ref[...]
            elif kind == EPI_ROPE64:
                y = y * c64_ref[...] + _swap_half64(y) * s64_ref[...]
            if streams == 1:
                o_ref[..., cols] = y.astype(o_ref.dtype).reshape(o_ref.shape[:-1] + (LANES,))
            else:
                z_ref[slab] = y
                for p in range(streams):
                    o_ref[p, :, cols] = z_ref[slab, pl.ds(p, per, stride=streams), :].astype(o_ref.dtype)


def _proj_call(h, w, gain_cols, tabs, epilogues, out_spec, out_shape, tm, streams, name):
    d, width = w.shape
    assert width == len(epilogues) * MXU_COLS
    tab = pl.BlockSpec((tm, LANES), lambda i: (i, 0))
    slabs = width // LANES if streams > 1 else 1
    return pl.pallas_call(
        functools.partial(_proj_kernel, epilogues=epilogues, streams=streams),
        grid=(h.shape[0] // tm,),
        in_specs=[
            pl.BlockSpec((tm, d), lambda i: (i, 0)),
            pl.BlockSpec((d, width), lambda i: (0, 0)),
            pl.BlockSpec((1, width), lambda i: (0, 0)),
            tab, tab, tab, tab,
        ],
        out_specs=out_spec,
        out_shape=out_shape,
        scratch_shapes=[pltpu.VMEM((slabs, tm, LANES), F32)],
        compiler_params=_params("parallel"),
        name=name,
    )(h, w, gain_cols, *tabs)


def _bproj(h, w, gain_cols, tabs, epilogues, name):
    n = h.shape[0]
    tm = _pick(n, 512)
    width = w.shape[1]
    return _proj_call(h, w, gain_cols, tabs, epilogues, pl.BlockSpec((tm, width), lambda i: (i, 0)),
                      jax.ShapeDtypeStruct((n, width), BF16), tm, 1, name)


def _aproj(h, w_group, gain_cols, tabs, batch, seq, dilation):
    r = dilation
    tm = _pick(seq, 512)
    tpb = seq // tm
    assert tm % (r * 16) == 0
    return _proj_call(h, w_group, gain_cols, tabs, A_EPILOGUES,
                      pl.BlockSpec((None, r, tm // r, A_PACK_WIDTH), lambda i: (i // tpb, 0, i % tpb, 0)),
                      jax.ShapeDtypeStruct((batch, r, seq // r, A_PACK_WIDTH), BF16), tm, r, f"aproj_r{r}")


def _idxproj_kernel(h_ref, w_ref, g_ref, c64_ref, s64_ref, ik_ref, iw_ref):
    z = jnp.dot(h_ref[...], w_ref[...], preferred_element_type=F32)
    lane = lax.broadcasted_iota(jnp.int32, z.shape, 1)
    is_k = lane < IDX_DIM
    zk = jnp.where(is_k, z, 0.0)
    y = _rms(zk, IDX_DIM) * g_ref[...]
    y = y * c64_ref[...] + _swap_half64(y) * s64_ref[...]
    ik_ref[...] = y[:, :IDX_DIM].astype(ik_ref.dtype)
    iw_ref[...] = z.T[IDX_DIM:IDX_DIM + IDX_HEADS, :] * (IDX_HEADS ** -0.5 * IDX_DIM ** -0.5)


def _idxproj(h, w_idx, gain_row, c64, s64):
    n, d = h.shape
    tm = _pick(n, 512)
    tab = pl.BlockSpec((tm, LANES), lambda i: (i, 0))
    return pl.pallas_call(
        _idxproj_kernel,
        grid=(n // tm,),
        in_specs=[
            pl.BlockSpec((tm, d), lambda i: (i, 0)),
            pl.BlockSpec((d, LANES), lambda i: (0, 0)),
            pl.BlockSpec((1, LANES), lambda i: (0, 0)),
            tab, tab,
        ],
        out_specs=[pl.BlockSpec((tm, IDX_DIM), lambda i: (i, 0)), pl.BlockSpec((IDX_HEADS, tm), lambda i: (0, i))],
        out_shape=[jax.ShapeDtypeStruct((n, IDX_DIM), BF16), jax.ShapeDtypeStruct((IDX_HEADS, n), F32)],
        compiler_params=_params("parallel"),
        name="idxproj",
    )(h, w_idx, gain_row, c64, s64)


def _dilated_kernel(q_ref, kc_ref, kp_ref, vc_ref, vp_ref, o_ref, lse_ref, *, tq):
    i = pl.program_id(2)
    blk = LANES
    scale = HEAD_DIM ** -0.5
    nkeys = blk + tq
    row = lax.broadcasted_iota(jnp.int32, (tq, nkeys), 0)
    col = lax.broadcasted_iota(jnp.int32, (tq, nkeys), 1)
    dist = row + blk - col
    band = jnp.where(dist >= 0, jnp.where(dist <= blk, 0.0, -jnp.inf), -jnp.inf)
    first = jnp.where(col >= blk, 0.0, -jnp.inf)
    bias = band + jnp.where(i > 0, 0.0, first)
    lane = lax.broadcasted_iota(jnp.int32, (tq, LANES), 1)
    nt = (((1,), (1,)), ((), ()))

    def scores(hh):
        cols = slice(hh * HEAD_DIM, (hh + 1) * HEAD_DIM)
        keys = jnp.concatenate([kp_ref[:, cols], kc_ref[:, cols]], axis=0)
        return lax.dot_general(q_ref[:, cols], keys, nt, preferred_element_type=F32) * scale + bias

    lse_tile = jnp.zeros((tq, LANES), F32)
    s_next = scores(0)
    for hh in range(A_HEADS_PER_GROUP):
        cols = slice(hh * HEAD_DIM, (hh + 1) * HEAD_DIM)
        s = s_next
        if hh + 1 < A_HEADS_PER_GROUP:
            s_next = scores(hh + 1)
        m = jnp.max(s, axis=1, keepdims=True)
        e = jnp.exp(s - m)
        den = jnp.sum(e, axis=1, keepdims=True)
        values = jnp.concatenate([vp_ref[:, cols], vc_ref[:, cols]], axis=0)
        acc = jnp.dot(e.astype(BF16), values, preferred_element_type=F32)
        o_ref[:, cols] = acc / den
        lse_tile = jnp.where(lane == hh, m + jnp.log(den), lse_tile)
    lse_ref[...] = lse_tile


def _dilated(qkv, window):
    b, r, m, _ = qkv.shape
    assert window // r == LANES and m % LANES == 0
    tq = _pick(m, 512)
    nsub = tq // LANES

    def cur(tile):
        return pl.BlockSpec((None, None, tq, COL_TILE), lambda bb, p, i: (bb, p, i, tile))

    def prev(tile):
        return pl.BlockSpec((None, None, LANES, COL_TILE),
                            lambda bb, p, i: (bb, p, jnp.maximum(i * nsub - 1, 0), tile))

    return pl.pallas_call(
        functools.partial(_dilated_kernel, tq=tq),
        grid=(b, r, m // tq),
        in_specs=[cur(0), cur(1), prev(1), cur(2), prev(2)],
        out_specs=[
            pl.BlockSpec((None, None, tq, A_GROUP_WIDTH), lambda bb, p, i: (bb, p, i, 0)),
            pl.BlockSpec((None, None, tq, LANES), lambda bb, p, i: (bb, p, i, 0)),
        ],
        out_shape=[jax.ShapeDtypeStruct((b, r, m, A_GROUP_WIDTH), F32), jax.ShapeDtypeStruct((b, r, m, LANES), F32)],
        compiler_params=_params("parallel", "parallel", "parallel"),
        name=f"dilated_r{r}",
    )(qkv, qkv, qkv, qkv, qkv)


def _key_to_float(key):
    bits = jnp.where(key >= 0, key, key ^ 0x7FFFFFFF)
    return lax.bitcast_convert_type(bits, F32)


def _float_to_key(x):
    bits = lax.bitcast_convert_type(x, jnp.int32)
    return jnp.where(bits >= 0, bits, bits ^ 0x7FFFFFFF)


PACK16 = 16


def _select_kernel(iq_ref, wt_ref, k_ref, bias_ref, qt_ref, sc_ref, hi_ref, lo_ref, gm_ref, *, topk):
    i = pl.program_id(1)
    tq, tk = SEL_TQ, SEL_TK
    nk = sc_ref.shape[0]
    nkb = (i * tq + tq + tk - 1) // tk
    slabs = tk // PACK16

    q_t = iq_ref[...].astype(F32).T
    for h in range(IDX_HEADS):
        qt_ref[:, h * tq:(h + 1) * tq] = q_t[h * IDX_DIM:(h + 1) * IDX_DIM, :].astype(qt_ref.dtype)

    kpos = lax.broadcasted_iota(jnp.int32, (tk, tq), 0)
    qpos = i * tq + lax.broadcasted_iota(jnp.int32, (tk, tq), 1)

    def rows(x):
        return jnp.concatenate([x] * slabs, axis=0)

    def score_block(kb, carry):
        keys = k_ref[pl.ds(pl.multiple_of(kb * tk, tk), tk), :]
        logits = jnp.dot(keys, qt_ref[...], preferred_element_type=F32)
        acc = jnp.zeros((tk, tq), F32)
        for h in range(IDX_HEADS):
            acc = acc + wt_ref[h:h + 1, :] * jnp.maximum(logits[:, h * tq:(h + 1) * tq], 0.0)
        masked = jnp.where(kpos + kb * tk <= qpos, acc, -jnp.inf)
        sc_ref[kb] = masked
        hi_ref[kb] = (_float_to_key(masked) >> 16).astype(jnp.int16)
        gm_ref[...] = jnp.maximum(gm_ref[...], jnp.maximum(masked[:tk // 2], masked[tk // 2:]))
        return carry

    gm_ref[...] = jnp.full(gm_ref.shape, -jnp.inf, F32)
    lax.fori_loop(0, nkb, score_block, 0)

    def reps(x):
        return jnp.broadcast_to(x, (PACK16, tq))

    bound_lo = reps(_float_to_key(jnp.min(gm_ref[...], axis=0, keepdims=True)) >> 16)
    bound_hi = reps(_float_to_key(jnp.max(gm_ref[...], axis=0, keepdims=True)) >> 16) + 1

    ngr = (nkb + SEL_GROUP - 1) // SEL_GROUP
    nkp = ngr * SEL_GROUP

    def pad_block(kb, carry):
        neg_inf = jnp.full((tk, tq), -jnp.inf, F32)
        sc_ref[kb] = neg_inf
        hi_ref[kb] = (_float_to_key(neg_inf) >> 16).astype(jnp.int16)
        return carry

    lax.fori_loop(nkb, nkp, pad_block, 0)

    def bisect16(ref, need, lo0, hi0, high_digit, untested_steps):
        def count_ge(t):
            t16 = t.astype(jnp.int16)

            def body(g, accs):
                accs = list(accs)
                for j in range(SEL_GROUP):
                    for r in range(slabs):
                        blk = ref[g * SEL_GROUP + j, r * PACK16:(r + 1) * PACK16, :]
                        hit = jnp.where(blk >= t16, jnp.int16(1), jnp.int16(0))
                        accs[r % len(accs)] = accs[r % len(accs)] + hit
                return tuple(accs)

            zero = jnp.zeros((PACK16, tq), jnp.int16)
            accs = lax.fori_loop(0, ngr, body, (zero,) * 4)
            acc = (accs[0] + accs[1]) + (accs[2] + accs[3])
            cnt = jnp.sum(acc.astype(jnp.int32), axis=0, keepdims=True)
            return jnp.broadcast_to(cnt, (PACK16, tq))

        def open_brackets(carry):
            _, lo, hi, _, _ = carry
            return jnp.max(hi - lo) > 1

        def step(carry):
            it, lo, hi, below, above = carry
            mid = (lo + hi) >> 1
            if high_digit:
                log_lo = jnp.log(below.astype(F32))
                frac = (log_lo - math.log(need - 0.5)) / (log_lo - jnp.log(jnp.maximum(above.astype(F32), 0.5)))
                frac = jnp.minimum(jnp.maximum(frac, 0.05), 0.95)
                v_lo, v_hi = _key_to_float((lo << 16) | 0xFFFF), _key_to_float(hi << 16)
                by_value = _float_to_key(v_lo + (v_hi - v_lo) * frac) >> 16
                by_value = jnp.minimum(jnp.maximum(by_value, lo + 1), hi - 1)
                mid = jnp.where(it % 4 == 3, mid, by_value)
            mid = jnp.where(hi - lo > 1, mid, lo)
            cnt = count_ge(mid)
            ge = cnt >= need
            lo, hi = jnp.where(ge, mid, lo), jnp.where(ge, hi, mid)
            below, above = jnp.where(ge, cnt, below), jnp.where(ge, above, cnt)
            if not high_digit:
                hi = jnp.where(cnt == need, mid + 1, hi)
            return it + 1, lo, hi, below, above

        everything = jnp.zeros((PACK16, tq), jnp.int32) + nkp * tk
        carry = (jnp.int32(0), lo0, hi0, everything, jnp.zeros((PACK16, tq), jnp.int32))
        carry = lax.fori_loop(0, untested_steps, lambda _, c: step(c), carry)
        _, lo, _, _, above = lax.while_loop(open_brackets, step, carry)
        return lo, above

    key_hi, above = bisect16(hi_ref, topk, bound_lo, bound_hi, True, SEL_UNTESTED_STEPS[0])
    key_hi_rows = rows(key_hi)

    def low_digits(kb, carry):
        key = _float_to_key(sc_ref[kb])
        low = (key & 0xFFFF) - 2 ** 15
        lo_ref[kb] = jnp.where((key >> 16) == key_hi_rows, low, -(2 ** 15)).astype(jnp.int16)
        return carry

    lax.fori_loop(0, nkp, low_digits, 0)
    digit_lo = jnp.full((PACK16, tq), -(2 ** 15), jnp.int32)
    digit_hi = jnp.full((PACK16, tq), 2 ** 15, jnp.int32)
    key_lo, _ = bisect16(lo_ref, topk - above, digit_lo, digit_hi, False, SEL_UNTESTED_STEPS[1])
    thr_rows = rows(_key_to_float((key_hi << 16) | (key_lo + 2 ** 15)))
    sub = 8

    def rows8(x):
        return jnp.concatenate([x] * (tk // sub), axis=0)

    def key_sum(x):
        parts = [x[r * sub:(r + 1) * sub] for r in range(tk // sub)]
        while len(parts) > 1:
            parts = [a + b for a, b in zip(parts[::2], parts[1::2])]
        return parts[0]

    def per_query(partial):
        return jnp.broadcast_to(jnp.sum(partial, axis=0, keepdims=True), (sub, tq))

    def store_mask(kb, picked):
        picked = jnp.where(kpos + kb * tk <= qpos, picked, NEG)
        bias_ref[kb] = picked.T.astype(bias_ref.dtype)

    def write_block(kb, n_ge):
        hit = sc_ref[kb] >= thr_rows
        store_mask(kb, jnp.where(hit, 0.0, NEG))
        return n_ge + key_sum(jnp.where(hit, 1.0, 0.0))

    n_ge = per_query(lax.fori_loop(0, nkb, write_block, jnp.zeros((sub, tq), F32)))

    @pl.when(jnp.max(n_ge) > topk)
    def _():
        def count(indicator):
            def body(kb, acc):
                return acc + key_sum(indicator(kb, sc_ref[kb]))
            return per_query(lax.fori_loop(0, nkb, body, jnp.zeros((sub, tq), F32)))

        n_gt = count(lambda kb, x: jnp.where(x > thr_rows, 1.0, 0.0))
        need_eq = topk - n_gt

        def step(_, carry):
            lo, hi = carry
            mid = (lo + hi) >> 1
            mid_rows = rows8(mid)
            tied_upto = count(lambda kb, x: jnp.where(
                x == thr_rows, jnp.where(kpos + kb * tk <= mid_rows, 1.0, 0.0), 0.0))
            ok = tied_upto >= need_eq
            return jnp.where(ok, lo, mid), jnp.where(ok, mid, hi)

        lo0 = jnp.full((sub, tq), -1, jnp.int32)
        hi0 = jnp.zeros((sub, tq), jnp.int32) + (nkb * tk - 1)
        _, last_tied = lax.fori_loop(0, (nk * tk).bit_length(), step, (lo0, hi0))
        last_rows = rows8(last_tied)

        def rewrite_block(kb, carry):
            x = sc_ref[kb]
            tied = jnp.where(kpos + kb * tk <= last_rows, 0.0, NEG)
            store_mask(kb, jnp.where(x > thr_rows, 0.0, jnp.where(x == thr_rows, tied, NEG)))
            return carry

        lax.fori_loop(0, nkb, rewrite_block, 0)

    def fill_block(kb, carry):
        bias_ref[kb] = jnp.full((tq, tk), NEG, bias_ref.dtype)
        return carry

    lax.fori_loop(nkb, nk, fill_block, 0)


def _select(z_iv, iw_t, ik, topk):
    b, s, _ = z_iv.shape
    tq, tk = SEL_TQ, SEL_TK
    nq, nk = s // tq, s // tk
    return pl.pallas_call(
        functools.partial(_select_kernel, topk=topk),
        grid=(b, nq),
        in_specs=[
            pl.BlockSpec((None, tq, IDX_WIDTH), lambda bb, i: (bb, i, 0)),
            pl.BlockSpec((IDX_HEADS, tq), lambda bb, i: (0, bb * nq + i)),
            pl.BlockSpec((None, s, IDX_DIM), lambda bb, i: (bb, 0, 0)),
        ],
        out_specs=pl.BlockSpec((None, None, nk, tq, tk), lambda bb, i: (bb, i, 0, 0, 0)),
        out_shape=jax.ShapeDtypeStruct((b, nq, nk, tq, tk), BF16),
        scratch_shapes=[
            pltpu.VMEM((IDX_DIM, IDX_HEADS * tq), BF16),
            pltpu.VMEM((nk, tk, tq), F32),
            pltpu.VMEM((nk, tk, tq), jnp.int16),
            pltpu.VMEM((nk, tk, tq), jnp.int16),
            pltpu.VMEM((tk // 2, tq), F32),
        ],
        compiler_params=_params("parallel", "parallel"),
        name="dsa_select",
    )(z_iv, iw_t, ik)


def _dsa_kernel(q_ref, k_ref, v_ref, b_ref, o_ref, m_ref, l_ref, acc_ref, *, tq, tk):
    i = pl.program_id(1)
    kb = pl.program_id(2)
    last = (i * tq + tq - 1) // tk
    nt = (((1,), (1,)), ((), ()))

    @pl.when(kb == 0)
    def _():
        m_ref[...] = jnp.full(m_ref.shape, NEG, F32)
        l_ref[...] = jnp.zeros(l_ref.shape, F32)
        acc_ref[...] = jnp.zeros(acc_ref.shape, F32)

    @pl.when(kb <= last)
    def _():
        bias = b_ref[...].reshape(tq, tk).astype(F32)
        rep = tk // LANES

        def scores(h):
            cols = slice(h * HEAD_DIM, (h + 1) * HEAD_DIM)
            return lax.dot_general(q_ref[:, cols], k_ref[:, cols], nt, preferred_element_type=F32) + bias

        s_next = scores(0)
        for h in range(B_HEADS):
            cols = slice(h * HEAD_DIM, (h + 1) * HEAD_DIM)
            s = s_next
            if h + 1 < B_HEADS:
                s_next = scores(h + 1)
            m_prev = m_ref[h]
            m_new = jnp.maximum(m_prev, jnp.max(s, axis=1, keepdims=True))
            alpha = jnp.exp2(m_prev - m_new)
            p = jnp.exp2(s - jnp.concatenate([m_new] * rep, axis=1))
            l_ref[h] = alpha * l_ref[h] + jnp.sum(p, axis=1, keepdims=True)
            acc_ref[:, cols] = alpha * acc_ref[:, cols] + jnp.dot(p.astype(BF16), v_ref[:, cols],
                                                                  preferred_element_type=F32)
            m_ref[h] = m_new

    @pl.when(kb == last)
    def _():
        for h in range(B_HEADS):
            cols = slice(h * HEAD_DIM, (h + 1) * HEAD_DIM)
            o_ref[:, cols] = (acc_ref[:, cols] / l_ref[h]).astype(o_ref.dtype)


def _dsa_attention(z_qk, z_iv, bias5):
    b, s, _ = z_qk.shape
    tk = SEL_TK
    tq = _pick(s, 512)
    sub = tq // SEL_TQ
    assert IDX_WIDTH == B_WIDTH

    def last(i):
        return (i * tq + tq - 1) // tk

    return pl.pallas_call(
        functools.partial(_dsa_kernel, tq=tq, tk=tk),
        grid=(b, s // tq, s // tk),
        in_specs=[
            pl.BlockSpec((None, tq, B_WIDTH), lambda bb, i, kb: (bb, i, 0)),
            pl.BlockSpec((None, tk, B_WIDTH), lambda bb, i, kb: (bb, jnp.minimum(kb, last(i)), 1)),
            pl.BlockSpec((None, tk, B_WIDTH), lambda bb, i, kb: (bb, jnp.minimum(kb, last(i)), 1)),
            pl.BlockSpec((None, sub, None, SEL_TQ, tk), lambda bb, i, kb: (bb, i, jnp.minimum(kb, last(i)), 0, 0)),
        ],
        out_specs=pl.BlockSpec((None, tq, B_WIDTH), lambda bb, i, kb: (bb, i, 0)),
        out_shape=jax.ShapeDtypeStruct((b, s, B_WIDTH), BF16),
        scratch_shapes=[
            pltpu.VMEM((B_HEADS, tq, LANES), F32),
            pltpu.VMEM((B_HEADS, tq, LANES), F32),
            pltpu.VMEM((tq, B_WIDTH), F32),
        ],
        compiler_params=_params("parallel", "parallel", "arbitrary"),
        name="dsa_attention",
    )(z_qk, z_qk, z_iv, bias5)


def _to_token_order(dst_ref, first, src_ref):
    r, per, width = src_ref.shape
    for c in range(width // LANES):
        cols = slice(c * LANES, (c + 1) * LANES)
        if r == 1:
            dst_ref[first + c] = src_ref[0, :, cols]
        else:
            for p in range(r):
                dst_ref[first + c, pl.ds(p, per, stride=r), :] = src_ref[p, :, cols]


def _merge_kernel(o1_ref, o2_ref, o3_ref, l1_ref, l2_ref, l3_ref, ob_ref, h_ref,
                  wga_ref, wgb_ref, bga_ref, bgb_ref, wpa_ref, wpb_ref, out_ref, oa_ref, ot_ref, lt_ref):
    @pl.when(pl.program_id(1) == 0)
    def _():
        nh = A_HEADS_PER_GROUP
        for g, (o_ref, l_ref) in enumerate(((o1_ref, l1_ref), (o2_ref, l2_ref), (o3_ref, l3_ref))):
            _to_token_order(ot_ref, g * nh, o_ref)
            _to_token_order(lt_ref, g, l_ref)
        l1, l2, l3 = lt_ref[0], lt_ref[1], lt_ref[2]
        mx = jnp.maximum(jnp.maximum(l1, l2), l3)
        e1, e2, e3 = jnp.exp(l1 - mx), jnp.exp(l2 - mx), jnp.exp(l3 - mx)
        tot = e1 + e2 + e3
        w1, w2, w3 = e1 / tot, e2 / tot, e3 / tot
        for hh in range(nh):
            oa = (w1[:, hh:hh + 1] * ot_ref[hh] + w2[:, hh:hh + 1] * ot_ref[nh + hh]
                  + w3[:, hh:hh + 1] * ot_ref[2 * nh + hh])
            oa_ref[:, hh * HEAD_DIM:(hh + 1) * HEAD_DIM] = oa.astype(oa_ref.dtype)

    h = h_ref[...]
    ga = jax.nn.sigmoid(jnp.dot(h, wga_ref[...], preferred_element_type=F32) + bga_ref[...])
    gb = jax.nn.sigmoid(jnp.dot(h, wgb_ref[...], preferred_element_type=F32) + bgb_ref[...])
    pa = jnp.dot(oa_ref[...], wpa_ref[...], preferred_element_type=F32)
    pb = jnp.dot(ob_ref[...], wpb_ref[...], preferred_element_type=F32)
    out_ref[...] = (ga * pa + gb * pb).astype(out_ref.dtype)


def _merge(outs, lses, o_b, h, w_gate, b_gate, w_proj_a, w_proj_b, seq):
    n, d = h.shape
    tm = _pick(seq, 512)
    tpb = seq // tm
    tn = _pick(d, COL_TILE)
    nj = d // tn
    row = lambda width: pl.BlockSpec((tm, width), lambda i, j: (i, 0))

    def streams(arr):
        r, width = arr.shape[1], arr.shape[3]
        return pl.BlockSpec((None, r, tm // r, width), lambda i, j: (i // tpb, 0, i % tpb, 0))

    return pl.pallas_call(
        _merge_kernel,
        grid=(n // tm, nj),
        in_specs=[
            *[streams(a) for a in outs], *[streams(a) for a in lses],
            row(B_WIDTH), row(d),
            pl.BlockSpec((d, tn), lambda i, j: (0, j)),
            pl.BlockSpec((d, tn), lambda i, j: (0, nj + j)),
            pl.BlockSpec((1, tn), lambda i, j: (0, j)),
            pl.BlockSpec((1, tn), lambda i, j: (0, nj + j)),
            pl.BlockSpec((A_GROUP_WIDTH, tn), lambda i, j: (0, j)),
            pl.BlockSpec((B_WIDTH, tn), lambda i, j: (0, j)),
        ],
        out_specs=pl.BlockSpec((tm, tn), lambda i, j: (i, j)),
        out_shape=jax.ShapeDtypeStruct((n, d), BF16),
        scratch_shapes=[
            pltpu.VMEM((tm, A_GROUP_WIDTH), BF16),
            pltpu.VMEM((len(outs) * A_HEADS_PER_GROUP, tm, HEAD_DIM), F32),
            pltpu.VMEM((len(lses), tm, LANES), F32),
        ],
        compiler_params=_params("parallel", "arbitrary"),
        name="gated_merge",
    )(*outs, *lses, o_b, h, w_gate, w_gate, b_gate, b_gate, w_proj_a, w_proj_b)


def _outproj_kernel(mg_ref, w_ref, x_ref, g_ref, sc_ref, sh_ref, xo_ref, ho_ref):
    mix = jnp.dot(mg_ref[...], w_ref[...], preferred_element_type=F32)
    x = x_ref[...] + g_ref[...] * mix
    xo_ref[...] = x
    ho_ref[...] = (_rms(x, x.shape[-1]) * (1.0 + sc_ref[...]) + sh_ref[...]).astype(ho_ref.dtype)


def _outproj(merged, w_out, x, mod, seq):
    n, d = x.shape
    tm = _pick(seq, 256)
    tpb = seq // tm
    row = pl.BlockSpec((tm, d), lambda i: (i, 0))
    return pl.pallas_call(
        _outproj_kernel,
        grid=(n // tm,),
        in_specs=[row, pl.BlockSpec((d, d), lambda i: (0, 0)), row,
                  _mod_spec(d, 2, tpb), _mod_spec(d, 4, tpb), _mod_spec(d, 3, tpb)],
        out_specs=[row, row],
        out_shape=[jax.ShapeDtypeStruct((n, d), F32), jax.ShapeDtypeStruct((n, d), BF16)],
        compiler_params=_params("parallel"),
        name="outproj",
    )(merged, w_out, x, mod, mod, mod)


def _ffn_kernel(h_ref, wu_ref, wd_ref, x_ref, g_ref, o_ref, acc_ref):
    c = pl.program_id(1)

    @pl.when(c == 0)
    def _():
        acc_ref[...] = jnp.zeros(acc_ref.shape, F32)

    u = jnp.maximum(jnp.dot(h_ref[...], wu_ref[...], preferred_element_type=F32), 0.0)
    acc_ref[...] += jnp.dot((u * u).astype(BF16), wd_ref[...], preferred_element_type=F32)

    @pl.when(c == pl.num_programs(1) - 1)
    def _():
        o_ref[...] = x_ref[...] + g_ref[...] * acc_ref[...]


def _ffn(h2, w_up, w_down, x, mod, seq):
    n, d = x.shape
    hidden = w_up.shape[1]
    tm = _pick(seq, 512)
    tc = _pick(hidden, 512)
    tpb = seq // tm
    row = pl.BlockSpec((tm, d), lambda i, c: (i, 0))
    return pl.pallas_call(
        _ffn_kernel,
        grid=(n // tm, hidden // tc),
        in_specs=[row, pl.BlockSpec((d, tc), lambda i, c: (0, c)), pl.BlockSpec((tc, d), lambda i, c: (c, 0)),
                  row, _mod_spec(d, 5, tpb)],
        out_specs=row,
        out_shape=jax.ShapeDtypeStruct((n, d), F32),
        scratch_shapes=[pltpu.VMEM((tm, d), F32)],
        compiler_params=_params("parallel", "arbitrary"),
        name="ffn",
    )(h2, w_up, w_down, x, mod)


def _pack_in_weights(w_in, a_q_gain, a_k_gain, b_q_gain, b_k_gain, idx_k_gain):
    d = w_in.shape[0]
    sizes = (A_WIDTH, A_WIDTH, A_WIDTH, B_WIDTH, B_WIDTH, B_WIDTH, IDX_WIDTH, IDX_DIM, IDX_HEADS)
    parts, off = [], 0
    for sz in sizes:
        parts.append(w_in[:, off:off + sz])
        off += sz
    aq, ak, av, bq, bk, bv, iq, ik, iw = parts
    w_iv = jnp.concatenate([iq, bv], axis=1).astype(BF16)
    w_qk = jnp.concatenate([bq, bk], axis=1).astype(BF16)
    w_idx = jnp.concatenate([ik, iw, jnp.zeros((d, LANES - IDX_DIM - IDX_HEADS), w_in.dtype)], axis=1).astype(BF16)
    ones = lambda width: jnp.ones((width,), F32)
    iv_gain_cols = ones(IDX_WIDTH + B_WIDTH).reshape(1, -1)
    qk_gain_cols = jnp.concatenate([jnp.tile(b_q_gain * DSA_Q_SCALE, B_HEADS), jnp.tile(b_k_gain, B_HEADS)]).reshape(1, -1)
    idx_gain_row = jnp.concatenate([idx_k_gain, ones(LANES - IDX_DIM)]).reshape(1, LANES)
    w_groups = []
    for g in range(len(A_GROUPS)):
        sl = slice(g * A_GROUP_WIDTH, (g + 1) * A_GROUP_WIDTH)
        w_groups.append(jnp.concatenate([aq[:, sl], ak[:, sl], av[:, sl]], axis=1).astype(BF16))
    a_gain_cols = jnp.concatenate([
        jnp.tile(a_q_gain, A_HEADS_PER_GROUP), jnp.tile(a_k_gain, A_HEADS_PER_GROUP), ones(A_GROUP_WIDTH),
    ]).reshape(1, A_PACK_WIDTH)
    return w_iv, iv_gain_cols, w_qk, qk_gain_cols, w_idx, idx_gain_row, w_groups, a_gain_cols


def kernel(x, c, positions, w_ada, b_ada, w_in, a_q_gain, a_k_gain, b_q_gain, b_k_gain, idx_k_gain,
           w_gate, b_gate, w_proj_a, w_proj_b, w_out, w_up, w_down):
    b, s, d = x.shape
    depth = w_ada.shape[0]
    n = b * s
    topk = min(IDX_TOPK, s // 4)
    assert s % (SEL_TK * SEL_GROUP) == 0 and SEL_TK // 2 >= topk and d % COL_TILE == 0 and s // LANES < 2 ** 15

    tabs = _rope_tables(positions)
    c128, s128, c64, s64 = tabs
    mods = _ada(c, w_ada, b_ada)
    xf = x.reshape(n, d)

    for l in range(depth):
        mod = mods[l]
        w_iv, iv_gain_cols, w_qk, qk_gain_cols, w_idx, idx_gain_row, w_groups, a_gain_cols = _pack_in_weights(
            w_in[l], a_q_gain[l], a_k_gain[l], b_q_gain[l], b_k_gain[l], idx_k_gain[l])

        h = _normmod(xf, mod, s, 1, 0)
        z_iv = _bproj(h, w_iv, iv_gain_cols, tabs, IV_EPILOGUES, "proj_iq_bv").reshape(b, s, -1)
        z_qk = _bproj(h, w_qk, qk_gain_cols, tabs, QK_EPILOGUES, "proj_bq_bk").reshape(b, s, -1)
        ik, iw = _idxproj(h, w_idx, idx_gain_row, c64, s64)

        a_outs, a_lses = [], []
        for g, (window, dilation) in enumerate(A_GROUPS):
            qkv = _aproj(h, w_groups[g], a_gain_cols, tabs, b, s, dilation)
            o, lse = _dilated(qkv, window)
            a_outs.append(o)
            a_lses.append(lse)

        bias5 = _select(z_iv, iw, ik.reshape(b, s, IDX_DIM), topk)
        o_b = _dsa_attention(z_qk, z_iv, bias5).reshape(n, B_WIDTH)

        merged = _merge(a_outs, a_lses, o_b, h, w_gate[l].astype(BF16), b_gate[l].reshape(1, 2 * d),
                        w_proj_a[l].astype(BF16), w_proj_b[l].astype(BF16), s)
        xf, h2 = _outproj(merged, w_out[l].astype(BF16), xf, mod, s)
        xf = _ffn(h2, w_up[l].astype(BF16), w_down[l].astype(BF16), xf, mod, s)

    return xf.reshape(b, s, d)
```

```python
import functools
import math

import jax
import jax.numpy as jnp
from jax import lax
from jax.experimental import pallas as pl
from jax.experimental.pallas import tpu as pltpu

F32 = jnp.float32
BF16 = jnp.bfloat16

HEAD_DIM = 128
LANES = 128
A_GROUPS = ((128, 1), (512, 4), (2048, 16))
A_HEADS_PER_GROUP = 4
A_GROUP_WIDTH = A_HEADS_PER_GROUP * HEAD_DIM
A_WIDTH = len(A_GROUPS) * A_GROUP_WIDTH
B_HEADS = 8
B_WIDTH = B_HEADS * HEAD_DIM
IDX_HEADS = 16
IDX_DIM = 64
IDX_WIDTH = IDX_HEADS * IDX_DIM
IDX_TOPK = 256
ROPE_THETA = 10000.0
EPS = 1e-6
N_MOD = 6
NEG = -1e30
LOG2_E = 1.4426950408889634
DSA_Q_SCALE = HEAD_DIM ** -0.5 * LOG2_E

COL_TILE = 512
MXU_COLS = 256
EPI_ROPE64, EPI_QK, EPI_PLAIN = 0, 1, 2
IV_EPILOGUES = (EPI_ROPE64,) * (IDX_WIDTH // MXU_COLS) + (EPI_PLAIN,) * (B_WIDTH // MXU_COLS)
QK_EPILOGUES = (EPI_QK,) * (2 * B_WIDTH // MXU_COLS)
A_EPILOGUES = (EPI_QK,) * (2 * A_GROUP_WIDTH // MXU_COLS) + (EPI_PLAIN,) * (A_GROUP_WIDTH // MXU_COLS)
A_PACK_WIDTH = 3 * A_GROUP_WIDTH

SEL_TQ = 128
SEL_TK = 512
SEL_GROUP = 4
SEL_UNTESTED_STEPS = (10, 9)
VMEM_LIMIT = 52 * 1024 * 1024


def _params(*sem):
    return pltpu.CompilerParams(dimension_semantics=sem, vmem_limit_bytes=VMEM_LIMIT)


def _pick(n, pref):
    t = pref
    while n % t:
        t //= 2
    return t


def _rms(x, width):
    return x * lax.rsqrt(jnp.sum(x * x, axis=-1, keepdims=True) * (1.0 / width) + EPS)


def _swap_half64(y):
    lane = lax.broadcasted_iota(jnp.int32, y.shape, 1)
    return jnp.where((lane & 63) < 32, pltpu.roll(y, 96, 1), pltpu.roll(y, 32, 1))


def _rope_tables_kernel(pos_ref, f128_ref, g128_ref, f64_ref, g64_ref, c128_ref, s128_ref, c64_ref, s64_ref):
    pos = pos_ref[...]
    a = pos * f128_ref[...]
    c128_ref[...] = jnp.cos(a)
    s128_ref[...] = jnp.sin(a) * g128_ref[...]
    a = pos * f64_ref[...]
    c64_ref[...] = jnp.cos(a)
    s64_ref[...] = jnp.sin(a) * g64_ref[...]


def _rope_tables(positions):
    n = positions.size
    pos = positions.reshape(n, 1).astype(F32)

    def freq(d):
        half = d // 2
        inv = jnp.power(ROPE_THETA, -jnp.arange(half, dtype=F32) * 2.0 / d)
        f = jnp.tile(jnp.concatenate([inv, inv]), LANES // d)
        g = jnp.tile(jnp.concatenate([-jnp.ones((half,), F32), jnp.ones((half,), F32)]), LANES // d)
        return f.reshape(1, LANES), g.reshape(1, LANES)

    f128, g128 = freq(HEAD_DIM)
    f64, g64 = freq(IDX_DIM)
    tm = _pick(n, 1024)
    row = pl.BlockSpec((1, LANES), lambda i: (0, 0))
    tab = pl.BlockSpec((tm, LANES), lambda i: (i, 0))
    return pl.pallas_call(
        _rope_tables_kernel,
        grid=(n // tm,),
        in_specs=[pl.BlockSpec((tm, 1), lambda i: (i, 0)), row, row, row, row],
        out_specs=[tab, tab, tab, tab],
        out_shape=[jax.ShapeDtypeStruct((n, LANES), F32)] * 4,
        compiler_params=_params("parallel"),
        name="rope_tables",
    )(pos, f128, g128, f64, g64)


def _ada_kernel(c_ref, w_ref, b_ref, o_ref):
    c = c_ref[...]
    act = (c * jax.nn.sigmoid(c)).astype(BF16)
    o_ref[...] = jnp.dot(act, w_ref[...].astype(BF16), preferred_element_type=F32) + b_ref[...]


def _ada(c, w_ada, b_ada):
    depth, d, n6 = w_ada.shape
    b = c.shape[0]
    rows = 8
    c_pad = jnp.zeros((rows, d), F32).at[:b].set(c)
    tn = _pick(n6, 1024)
    out = pl.pallas_call(
        _ada_kernel,
        grid=(depth, n6 // tn),
        in_specs=[
            pl.BlockSpec((rows, d), lambda l, j: (0, 0)),
            pl.BlockSpec((None, d, tn), lambda l, j: (l, 0, j)),
            pl.BlockSpec((None, 1, tn), lambda l, j: (l, 0, j)),
        ],
        out_specs=pl.BlockSpec((None, rows, tn), lambda l, j: (l, 0, j)),
        out_shape=jax.ShapeDtypeStruct((depth, rows, n6), F32),
        compiler_params=_params("parallel", "parallel"),
        name="adaln",
    )(c_pad, w_ada, b_ada.reshape(depth, 1, n6))
    return out[:, :b].reshape(depth, b, N_MOD, 1, d)


def _mod_spec(d, which, tiles_per_batch):
    return pl.BlockSpec((None, None, 1, d), lambda i, *_: (i // tiles_per_batch, which, 0, 0))


def _normmod_kernel(x_ref, sc_ref, sh_ref, o_ref):
    x = x_ref[...]
    y = _rms(x, x.shape[-1])
    o_ref[...] = (y * (1.0 + sc_ref[...]) + sh_ref[...]).astype(o_ref.dtype)


def _normmod(x, mod, seq, which_scale, which_shift):
    n, d = x.shape
    tm = _pick(seq, 512)
    tpb = seq // tm
    return pl.pallas_call(
        _normmod_kernel,
        grid=(n // tm,),
        in_specs=[pl.BlockSpec((tm, d), lambda i: (i, 0)), _mod_spec(d, which_scale, tpb), _mod_spec(d, which_shift, tpb)],
        out_specs=pl.BlockSpec((tm, d), lambda i: (i, 0)),
        out_shape=jax.ShapeDtypeStruct((n, d), BF16),
        compiler_params=_params("parallel"),
        name="normmod",
    )(x, mod, mod)


def _proj_kernel(h_ref, w_ref, g_ref, c128_ref, s128_ref, c64_ref, s64_ref, o_ref, z_ref, *, epilogues, streams):
    h = h_ref[...]
    per = h.shape[0] // streams
    for t, kind in enumerate(epilogues):
        z = jnp.dot(h, w_ref[:, t * MXU_COLS:(t + 1) * MXU_COLS], preferred_element_type=F32)
        for c in range(MXU_COLS // LANES):
            slab = t * (MXU_COLS // LANES) + c
            cols = slice(slab * LANES, (slab + 1) * LANES)
            y = z[:, c * LANES:(c + 1) * LANES]
            if kind == EPI_QK:
                y = _rms(y, HEAD_DIM) * g_ref[:, cols]
                y = y * c128_ref[...] + pltpu.roll(y, HEAD_DIM // 2, 1) * s128_ref[...]
            elif kind == EPI_ROPE64:
                y = y * c64_ref[...] + _swap_half64(y) * s64_ref[...]
            if streams == 1:
                o_ref[..., cols] = y.astype(o_ref.dtype).reshape(o_ref.shape[:-1] + (LANES,))
            else:
                z_ref[slab] = y
                for p in range(streams):
                    o_ref[p, :, cols] = z_ref[slab, pl.ds(p, per, stride=streams), :].astype(o_ref.dtype)


def _proj_call(h, w, gain_cols, tabs, epilogues, out_spec, out_shape, tm, streams, name):
    d, width = w.shape
    assert width == len(epilogues) * MXU_COLS
    tab = pl.BlockSpec((tm, LANES), lambda i: (i, 0))
    slabs = width // LANES if streams > 1 else 1
    return pl.pallas_call(
        functools.partial(_proj_kernel, epilogues=epilogues, streams=streams),
        grid=(h.shape[0] // tm,),
        in_specs=[
            pl.BlockSpec((tm, d), lambda i: (i, 0)),
            pl.BlockSpec((d, width), lambda i: (0, 0)),
            pl.BlockSpec((1, width), lambda i: (0, 0)),
            tab, tab, tab, tab,
        ],
        out_specs=out_spec,
        out_shape=out_shape,
        scratch_shapes=[pltpu.VMEM((slabs, tm, LANES), F32)],
        compiler_params=_params("parallel"),
        name=name,
    )(h, w, gain_cols, *tabs)


def _bproj(h, w, gain_cols, tabs, epilogues, name):
    n = h.shape[0]
    tm = _pick(n, 512)
    width = w.shape[1]
    return _proj_call(h, w, gain_cols, tabs, epilogues, pl.BlockSpec((tm, width), lambda i: (i, 0)),
                      jax.ShapeDtypeStruct((n, width), BF16), tm, 1, name)


def _aproj(h, w_group, gain_cols, tabs, batch, seq, dilation):
    r = dilation
    tm = _pick(seq, 512)
    tpb = seq // tm
    assert tm % (r * 16) == 0
    return _proj_call(h, w_group, gain_cols, tabs, A_EPILOGUES,
                      pl.BlockSpec((None, r, tm // r, A_PACK_WIDTH), lambda i: (i // tpb, 0, i % tpb, 0)),
                      jax.ShapeDtypeStruct((batch, r, seq // r, A_PACK_WIDTH), BF16), tm, r, f"aproj_r{r}")


def _idxproj_kernel(h_ref, w_ref, g_ref, c64_ref, s64_ref, ik_ref, iw_ref):
    z = jnp.dot(h_ref[...], w_ref[...], preferred_element_type=F32)
    lane = lax.broadcasted_iota(jnp.int32, z.shape, 1)
    is_k = lane < IDX_DIM
    zk = jnp.where(is_k, z, 0.0)
    y = _rms(zk, IDX_DIM) * g_ref[...]
    y = y * c64_ref[...] + _swap_half64(y) * s64_ref[...]
    ik_ref[...] = y[:, :IDX_DIM].astype(ik_ref.dtype)
    iw_ref[...] = z.T[IDX_DIM:IDX_DIM + IDX_HEADS, :] * (IDX_HEADS ** -0.5 * IDX_DIM ** -0.5)


def _idxproj(h, w_idx, gain_row, c64, s64):
    n, d = h.shape
    tm = _pick(n, 512)
    tab = pl.BlockSpec((tm, LANES), lambda i: (i, 0))
    return pl.pallas_call(
        _idxproj_kernel,
        grid=(n // tm,),
        in_specs=[
            pl.BlockSpec((tm, d), lambda i: (i, 0)),
            pl.BlockSpec((d, LANES), lambda i: (0, 0)),
            pl.BlockSpec((1, LANES), lambda i: (0, 0)),
            tab, tab,
        ],
        out_specs=[pl.BlockSpec((tm, IDX_DIM), lambda i: (i, 0)), pl.BlockSpec((IDX_HEADS, tm), lambda i: (0, i))],
        out_shape=[jax.ShapeDtypeStruct((n, IDX_DIM), BF16), jax.ShapeDtypeStruct((IDX_HEADS, n), F32)],
        compiler_params=_params("parallel"),
        name="idxproj",
    )(h, w_idx, gain_row, c64, s64)


def _dilated_kernel(q_ref, kc_ref, kp_ref, vc_ref, vp_ref, o_ref, lse_ref, *, tq):
    i = pl.program_id(2)
    blk = LANES
    scale = HEAD_DIM ** -0.5
    nkeys = blk + tq
    row = lax.broadcasted_iota(jnp.int32, (tq, nkeys), 0)
    col = lax.broadcasted_iota(jnp.int32, (tq, nkeys), 1)
    dist = row + blk - col
    band = jnp.where(dist >= 0, jnp.where(dist <= blk, 0.0, -jnp.inf), -jnp.inf)
    first = jnp.where(col >= blk, 0.0, -jnp.inf)
    bias = band + jnp.where(i > 0, 0.0, first)
    lane = lax.broadcasted_iota(jnp.int32, (tq, LANES), 1)
    nt = (((1,), (1,)), ((), ()))

    def scores(hh):
        cols = slice(hh * HEAD_DIM, (hh + 1) * HEAD_DIM)
        keys = jnp.concatenate([kp_ref[:, cols], kc_ref[:, cols]], axis=0)
        return lax.dot_general(q_ref[:, cols], keys, nt, preferred_element_type=F32) * scale + bias

    lse_tile = jnp.zeros((tq, LANES), F32)
    s_next = scores(0)
    for hh in range(A_HEADS_PER_GROUP):
        cols = slice(hh * HEAD_DIM, (hh + 1) * HEAD_DIM)
        s = s_next
        if hh + 1 < A_HEADS_PER_GROUP:
            s_next = scores(hh + 1)
        m = jnp.max(s, axis=1, keepdims=True)
        e = jnp.exp(s - m)
        den = jnp.sum(e, axis=1, keepdims=True)
        values = jnp.concatenate([vp_ref[:, cols], vc_ref[:, cols]], axis=0)
        acc = jnp.dot(e.astype(BF16), values, preferred_element_type=F32)
        o_ref[:, cols] = acc / den
        lse_tile = jnp.where(lane == hh, m + jnp.log(den), lse_tile)
    lse_ref[...] = lse_tile


def _dilated(qkv, window):
    b, r, m, _ = qkv.shape
    assert window // r == LANES and m % LANES == 0
    tq = _pick(m, 512)
    nsub = tq // LANES

    def cur(tile):
        return pl.BlockSpec((None, None, tq, COL_TILE), lambda bb, p, i: (bb, p, i, tile))

    def prev(tile):
        return pl.BlockSpec((None, None, LANES, COL_TILE),
                            lambda bb, p, i: (bb, p, jnp.maximum(i * nsub - 1, 0), tile))

    return pl.pallas_call(
        functools.partial(_dilated_kernel, tq=tq),
        grid=(b, r, m // tq),
        in_specs=[cur(0), cur(1), prev(1), cur(2), prev(2)],
        out_specs=[
            pl.BlockSpec((None, None, tq, A_GROUP_WIDTH), lambda bb, p, i: (bb, p, i, 0)),
            pl.BlockSpec((None, None, tq, LANES), lambda bb, p, i: (bb, p, i, 0)),
        ],
        out_shape=[jax.ShapeDtypeStruct((b, r, m, A_GROUP_WIDTH), F32), jax.ShapeDtypeStruct((b, r, m, LANES), F32)],
        compiler_params=_params("parallel", "parallel", "parallel"),
        name=f"dilated_r{r}",
    )(qkv, qkv, qkv, qkv, qkv)


def _key_to_float(key):
    bits = jnp.where(key >= 0, key, key ^ 0x7FFFFFFF)
    return lax.bitcast_convert_type(bits, F32)


def _float_to_key(x):
    bits = lax.bitcast_convert_type(x, jnp.int32)
    return jnp.where(bits >= 0, bits, bits ^ 0x7FFFFFFF)


PACK16 = 16


def _select_kernel(iq_ref, wt_ref, k_ref, bias_ref, qt_ref, sc_ref, hi_ref, lo_ref, gm_ref, *, topk):
    i = pl.program_id(1)
    tq, tk = SEL_TQ, SEL_TK
    nk = sc_ref.shape[0]
    nkb = (i * tq + tq + tk - 1) // tk
    slabs = tk // PACK16

    q_t = iq_ref[...].astype(F32).T
    for h in range(IDX_HEADS):
        qt_ref[:, h * tq:(h + 1) * tq] = q_t[h * IDX_DIM:(h + 1) * IDX_DIM, :].astype(qt_ref.dtype)

    kpos = lax.broadcasted_iota(jnp.int32, (tk, tq), 0)
    qpos = i * tq + lax.broadcasted_iota(jnp.int32, (tk, tq), 1)

    def rows(x):
        return jnp.concatenate([x] * slabs, axis=0)

    def _store_digits(kb, scores):
        key = _float_to_key(scores)
        hi_ref[kb] = (key >> 16).astype(jnp.int16)
        lo_ref[kb] = ((key & 0xFFFF) - 2 ** 15).astype(jnp.int16)

    def score_block(kb, carry):
        keys = k_ref[pl.ds(pl.multiple_of(kb * tk, tk), tk), :]
        logits = jnp.dot(keys, qt_ref[...], preferred_element_type=F32)
        acc = jnp.zeros((tk, tq), F32)
        for h in range(IDX_HEADS):
            acc = acc + wt_ref[h:h + 1, :] * jnp.maximum(logits[:, h * tq:(h + 1) * tq], 0.0)
        masked = jnp.where(kpos + kb * tk <= qpos, acc, -jnp.inf)
        sc_ref[kb] = masked
        _store_digits(kb, masked)
        gm_ref[...] = jnp.maximum(gm_ref[...], jnp.maximum(masked[:tk // 2], masked[tk // 2:]))
        return carry

    gm_ref[...] = jnp.full(gm_ref.shape, -jnp.inf, F32)
    lax.fori_loop(0, nkb, score_block, 0)

    def reps(x):
        return jnp.broadcast_to(x, (PACK16, tq))

    bound_lo = reps(_float_to_key(jnp.min(gm_ref[...], axis=0, keepdims=True)) >> 16)
    bound_hi = reps(_float_to_key(jnp.max(gm_ref[...], axis=0, keepdims=True)) >> 16) + 1

    ngr = (nkb + SEL_GROUP - 1) // SEL_GROUP
    nkp = ngr * SEL_GROUP

    def pad_block(kb, carry):
        neg_inf = jnp.full((tk, tq), -jnp.inf, F32)
        sc_ref[kb] = neg_inf
        _store_digits(kb, neg_inf)
        return carry

    lax.fori_loop(nkb, nkp, pad_block, 0)

    def bisect16(ref, need, lo0, hi0, high_digit, untested_steps):
        def count_ge(t):
            t16 = t.astype(jnp.int16)

            def body(g, accs):
                accs = list(accs)
                for j in range(SEL_GROUP):
                    for r in range(slabs):
                        blk = ref[g * SEL_GROUP + j, r * PACK16:(r + 1) * PACK16, :]
                        hit = jnp.where(blk >= t16, jnp.int16(1), jnp.int16(0))
                        accs[r % len(accs)] = accs[r % len(accs)] + hit
                return tuple(accs)

            zero = jnp.zeros((PACK16, tq), jnp.int16)
            accs = lax.fori_loop(0, ngr, body, (zero,) * 4)
            acc = (accs[0] + accs[1]) + (accs[2] + accs[3])
            cnt = jnp.sum(acc.astype(jnp.int32), axis=0, keepdims=True)
            return jnp.broadcast_to(cnt, (PACK16, tq))

        def open_brackets(carry):
            _, lo, hi, _, _ = carry
            return jnp.max(hi - lo) > 1

        def step(carry):
            it, lo, hi, below, above = carry
            mid = (lo + hi) >> 1
            if high_digit:
                log_lo = jnp.log(below.astype(F32))
                frac = (log_lo - math.log(need - 0.5)) / (log_lo - jnp.log(jnp.maximum(above.astype(F32), 0.5)))
                frac = jnp.minimum(jnp.maximum(frac, 0.05), 0.95)
                v_lo, v_hi = _key_to_float((lo << 16) | 0xFFFF), _key_to_float(hi << 16)
                by_value = _float_to_key(v_lo + (v_hi - v_lo) * frac) >> 16
                by_value = jnp.minimum(jnp.maximum(by_value, lo + 1), hi - 1)
                mid = jnp.where(it % 4 == 3, mid, by_value)
            mid = jnp.where(hi - lo > 1, mid, lo)
            cnt = count_ge(mid)
            ge = cnt >= need
            lo, hi = jnp.where(ge, mid, lo), jnp.where(ge, hi, mid)
            below, above = jnp.where(ge, cnt, below), jnp.where(ge, above, cnt)
            if not high_digit:
                hi = jnp.where(cnt == need, mid + 1, hi)
            return it + 1, lo, hi, below, above

        everything = jnp.zeros((PACK16, tq), jnp.int32) + nkp * tk
        carry = (jnp.int32(0), lo0, hi0, everything, jnp.zeros((PACK16, tq), jnp.int32))
        carry = lax.fori_loop(0, untested_steps, lambda _, c: step(c), carry)
        _, lo, _, _, above = lax.while_loop(open_brackets, step, carry)
        return lo, above

    key_hi, above = bisect16(hi_ref, topk, bound_lo, bound_hi, True, SEL_UNTESTED_STEPS[0])
    key_hi16 = key_hi.astype(jnp.int16)

    def low_digits(kb, carry):
        for r in range(slabs):
            sl = slice(r * PACK16, (r + 1) * PACK16)
            lo_ref[kb, sl, :] = jnp.where(hi_ref[kb, sl, :] == key_hi16, lo_ref[kb, sl, :], jnp.int16(-(2 ** 15)))
        return carry

    lax.fori_loop(0, nkp, low_digits, 0)
    digit_lo = jnp.full((PACK16, tq), -(2 ** 15), jnp.int32)
    digit_hi = jnp.full((PACK16, tq), 2 ** 15, jnp.int32)
    key_lo, _ = bisect16(lo_ref, topk - above, digit_lo, digit_hi, False, SEL_UNTESTED_STEPS[1])
    thr_rows = rows(_key_to_float((key_hi << 16) | (key_lo + 2 ** 15)))
    sub = 8

    def rows8(x):
        return jnp.concatenate([x] * (tk // sub), axis=0)

    def key_sum(x):
        parts = [x[r * sub:(r + 1) * sub] for r in range(tk // sub)]
        while len(parts) > 1:
            parts = [a + b for a, b in zip(parts[::2], parts[1::2])]
        return parts[0]

    def per_query(partial):
        return jnp.broadcast_to(jnp.sum(partial, axis=0, keepdims=True), (sub, tq))

    def store_mask(kb, picked):
        picked = jnp.where(kpos + kb * tk <= qpos, picked, NEG)
        bias_ref[kb] = picked.astype(bias_ref.dtype).T

    def write_block(kb, n_ge):
        hit = sc_ref[kb] >= thr_rows
        store_mask(kb, jnp.where(hit, 0.0, NEG))
        return n_ge + key_sum(jnp.where(hit, 1.0, 0.0))

    n_ge = per_query(lax.fori_loop(0, nkb, write_block, jnp.zeros((sub, tq), F32)))

    @pl.when(jnp.max(n_ge) > topk)
    def _():
        def count(indicator):
            def body(kb, acc):
                return acc + key_sum(indicator(kb, sc_ref[kb]))
            return per_query(lax.fori_loop(0, nkb, body, jnp.zeros((sub, tq), F32)))

        n_gt = count(lambda kb, x: jnp.where(x > thr_rows, 1.0, 0.0))
        need_eq = topk - n_gt

        def step(_, carry):
            lo, hi = carry
            mid = (lo + hi) >> 1
            mid_rows = rows8(mid)
            tied_upto = count(lambda kb, x: jnp.where(
                x == thr_rows, jnp.where(kpos + kb * tk <= mid_rows, 1.0, 0.0), 0.0))
            ok = tied_upto >= need_eq
            return jnp.where(ok, lo, mid), jnp.where(ok, mid, hi)

        lo0 = jnp.full((sub, tq), -1, jnp.int32)
        hi0 = jnp.zeros((sub, tq), jnp.int32) + (nkb * tk - 1)
        _, last_tied = lax.fori_loop(0, (nk * tk).bit_length(), step, (lo0, hi0))
        last_rows = rows8(last_tied)

        def rewrite_block(kb, carry):
            x = sc_ref[kb]
            tied = jnp.where(kpos + kb * tk <= last_rows, 0.0, NEG)
            store_mask(kb, jnp.where(x > thr_rows, 0.0, jnp.where(x == thr_rows, tied, NEG)))
            return carry

        lax.fori_loop(0, nkb, rewrite_block, 0)

    def fill_block(kb, carry):
        bias_ref[kb] = jnp.full((tq, tk), NEG, bias_ref.dtype)
        return carry

    lax.fori_loop(nkb, nk, fill_block, 0)


def _select(z_iv, iw_t, ik, topk):
    b, s, _ = z_iv.shape
    tq, tk = SEL_TQ, SEL_TK
    nq, nk = s // tq, s // tk
    return pl.pallas_call(
        functools.partial(_select_kernel, topk=topk),
        grid=(b, nq),
        in_specs=[
            pl.BlockSpec((None, tq, IDX_WIDTH), lambda bb, i: (bb, i, 0)),
            pl.BlockSpec((IDX_HEADS, tq), lambda bb, i: (0, bb * nq + i)),
            pl.BlockSpec((None, s, IDX_DIM), lambda bb, i: (bb, 0, 0)),
        ],
        out_specs=pl.BlockSpec((None, None, nk, tq, tk), lambda bb, i: (bb, i, 0, 0, 0)),
        out_shape=jax.ShapeDtypeStruct((b, nq, nk, tq, tk), BF16),
        scratch_shapes=[
            pltpu.VMEM((IDX_DIM, IDX_HEADS * tq), BF16),
            pltpu.VMEM((nk, tk, tq), F32),
            pltpu.VMEM((nk, tk, tq), jnp.int16),
            pltpu.VMEM((nk, tk, tq), jnp.int16),
            pltpu.VMEM((tk // 2, tq), F32),
        ],
        compiler_params=_params("parallel", "parallel"),
        name="dsa_select",
    )(z_iv, iw_t, ik)


def _dsa_kernel(q_ref, k_ref, v_ref, b_ref, o_ref, m_ref, l_ref, acc_ref, *, tq, tk):
    i = pl.program_id(1)
    kb = pl.program_id(2)
    last = (i * tq + tq - 1) // tk
    nt = (((1,), (1,)), ((), ()))

    @pl.when(kb == 0)
    def _():
        m_ref[...] = jnp.full(m_ref.shape, NEG, F32)
        l_ref[...] = jnp.zeros(l_ref.shape, F32)
        acc_ref[...] = jnp.zeros(acc_ref.shape, F32)

    @pl.when(kb <= last)
    def _():
        bias = b_ref[...].reshape(tq, tk).astype(F32)
        rep = tk // LANES

        def scores(h):
            cols = slice(h * HEAD_DIM, (h + 1) * HEAD_DIM)
            return lax.dot_general(q_ref[:, cols], k_ref[:, cols], nt, preferred_element_type=F32) + bias

        s_next = scores(0)
        for h in range(B_HEADS):
            cols = slice(h * HEAD_DIM, (h + 1) * HEAD_DIM)
            s = s_next
            if h + 1 < B_HEADS:
                s_next = scores(h + 1)
            m_prev = m_ref[h]
            m_new = jnp.maximum(m_prev, jnp.max(s, axis=1, keepdims=True))
            alpha = jnp.exp2(m_prev - m_new)
            p = jnp.exp2(s - jnp.concatenate([m_new] * rep, axis=1))
            l_ref[h] = alpha * l_ref[h] + jnp.sum(p, axis=1, keepdims=True)
            acc_ref[:, cols] = alpha * acc_ref[:, cols] + jnp.dot(p.astype(BF16), v_ref[:, cols],
                                                                  preferred_element_type=F32)
            m_ref[h] = m_new

    @pl.when(kb == last)
    def _():
        for h in range(B_HEADS):
            cols = slice(h * HEAD_DIM, (h + 1) * HEAD_DIM)
            o_ref[:, cols] = (acc_ref[:, cols] / l_ref[h]).astype(o_ref.dtype)


def _dsa_attention(z_qk, z_iv, bias5):
    b, s, _ = z_qk.shape
    tk = SEL_TK
    tq = _pick(s, 512)
    sub = tq // SEL_TQ
    assert IDX_WIDTH == B_WIDTH

    def last(i):
        return (i * tq + tq - 1) // tk

    return pl.pallas_call(
        functools.partial(_dsa_kernel, tq=tq, tk=tk),
        grid=(b, s // tq, s // tk),
        in_specs=[
            pl.BlockSpec((None, tq, B_WIDTH), lambda bb, i, kb: (bb, i, 0)),
            pl.BlockSpec((None, tk, B_WIDTH), lambda bb, i, kb: (bb, jnp.minimum(kb, last(i)), 1)),
            pl.BlockSpec((None, tk, B_WIDTH), lambda bb, i, kb: (bb, jnp.minimum(kb, last(i)), 1)),
            pl.BlockSpec((None, sub, None, SEL_TQ, tk), lambda bb, i, kb: (bb, i, jnp.minimum(kb, last(i)), 0, 0)),
        ],
        out_specs=pl.BlockSpec((None, tq, B_WIDTH), lambda bb, i, kb: (bb, i, 0)),
        out_shape=jax.ShapeDtypeStruct((b, s, B_WIDTH), BF16),
        scratch_shapes=[
            pltpu.VMEM((B_HEADS, tq, LANES), F32),
            pltpu.VMEM((B_HEADS, tq, LANES), F32),
            pltpu.VMEM((tq, B_WIDTH), F32),
        ],
        compiler_params=_params("parallel", "parallel", "arbitrary"),
        name="dsa_attention",
    )(z_qk, z_qk, z_iv, bias5)


def _to_token_order(dst_ref, first, src_ref):
    r, per, width = src_ref.shape
    for c in range(width // LANES):
        cols = slice(c * LANES, (c + 1) * LANES)
        if r == 1:
            dst_ref[first + c] = src_ref[0, :, cols]
        else:
            for p in range(r):
                dst_ref[first + c, pl.ds(p, per, stride=r), :] = src_ref[p, :, cols]


def _merge_kernel(o1_ref, o2_ref, o3_ref, l1_ref, l2_ref, l3_ref, ob_ref, h_ref,
                  wga_ref, wgb_ref, bga_ref, bgb_ref, wpa_ref, wpb_ref, out_ref, oa_ref, ot_ref, lt_ref):
    @pl.when(pl.program_id(1) == 0)
    def _():
        nh = A_HEADS_PER_GROUP
        for g, (o_ref, l_ref) in enumerate(((o1_ref, l1_ref), (o2_ref, l2_ref), (o3_ref, l3_ref))):
            _to_token_order(ot_ref, g * nh, o_ref)
            _to_token_order(lt_ref, g, l_ref)
        l1, l2, l3 = lt_ref[0], lt_ref[1], lt_ref[2]
        mx = jnp.maximum(jnp.maximum(l1, l2), l3)
        e1, e2, e3 = jnp.exp(l1 - mx), jnp.exp(l2 - mx), jnp.exp(l3 - mx)
        tot = e1 + e2 + e3
        w1, w2, w3 = e1 / tot, e2 / tot, e3 / tot
        for hh in range(nh):
            oa = (w1[:, hh:hh + 1] * ot_ref[hh] + w2[:, hh:hh + 1] * ot_ref[nh + hh]
                  + w3[:, hh:hh + 1] * ot_ref[2 * nh + hh])
            oa_ref[:, hh * HEAD_DIM:(hh + 1) * HEAD_DIM] = oa.astype(oa_ref.dtype)

    h = h_ref[...]
    ga = jax.nn.sigmoid(jnp.dot(h, wga_ref[...], preferred_element_type=F32) + bga_ref[...])
    gb = jax.nn.sigmoid(jnp.dot(h, wgb_ref[...], preferred_element_type=F32) + bgb_ref[...])
    pa = jnp.dot(oa_ref[...], wpa_ref[...], preferred_element_type=F32)
    pb = jnp.dot(ob_ref[...], wpb_ref[...], preferred_element_type=F32)
    out_ref[...] = (ga * pa + gb * pb).astype(out_ref.dtype)


def _merge(outs, lses, o_b, h, w_gate, b_gate, w_proj_a, w_proj_b, seq):
    n, d = h.shape
    tm = _pick(seq, 512)
    tpb = seq // tm
    tn = _pick(d, COL_TILE)
    nj = d // tn
    row = lambda width: pl.BlockSpec((tm, width), lambda i, j: (i, 0))

    def streams(arr):
        r, width = arr.shape[1], arr.shape[3]
        return pl.BlockSpec((None, r, tm // r, width), lambda i, j: (i // tpb, 0, i % tpb, 0))

    return pl.pallas_call(
        _merge_kernel,
        grid=(n // tm, nj),
        in_specs=[
            *[streams(a) for a in outs], *[streams(a) for a in lses],
            row(B_WIDTH), row(d),
            pl.BlockSpec((d, tn), lambda i, j: (0, j)),
            pl.BlockSpec((d, tn), lambda i, j: (0, nj + j)),
            pl.BlockSpec((1, tn), lambda i, j: (0, j)),
            pl.BlockSpec((1, tn), lambda i, j: (0, nj + j)),
            pl.BlockSpec((A_GROUP_WIDTH, tn), lambda i, j: (0, j)),
            pl.BlockSpec((B_WIDTH, tn), lambda i, j: (0, j)),
        ],
        out_specs=pl.BlockSpec((tm, tn), lambda i, j: (i, j)),
        out_shape=jax.ShapeDtypeStruct((n, d), BF16),
        scratch_shapes=[
            pltpu.VMEM((tm, A_GROUP_WIDTH), BF16),
            pltpu.VMEM((len(outs) * A_HEADS_PER_GROUP, tm, HEAD_DIM), F32),
            pltpu.VMEM((len(lses), tm, LANES), F32),
        ],
        compiler_params=_params("parallel", "arbitrary"),
        name="gated_merge",
    )(*outs, *lses, o_b, h, w_gate, w_gate, b_gate, b_gate, w_proj_a, w_proj_b)


def _outproj_kernel(mg_ref, w_ref, x_ref, g_ref, sc_ref, sh_ref, xo_ref, ho_ref):
    mix = jnp.dot(mg_ref[...], w_ref[...], preferred_element_type=F32)
    x = x_ref[...] + g_ref[...] * mix
    xo_ref[...] = x
    ho_ref[...] = (_rms(x, x.shape[-1]) * (1.0 + sc_ref[...]) + sh_ref[...]).astype(ho_ref.dtype)


def _outproj(merged, w_out, x, mod, seq):
    n, d = x.shape
    tm = _pick(seq, 256)
    tpb = seq // tm
    row = pl.BlockSpec((tm, d), lambda i: (i, 0))
    return pl.pallas_call(
        _outproj_kernel,
        grid=(n // tm,),
        in_specs=[row, pl.BlockSpec((d, d), lambda i: (0, 0)), row,
                  _mod_spec(d, 2, tpb), _mod_spec(d, 4, tpb), _mod_spec(d, 3, tpb)],
        out_specs=[row, row],
        out_shape=[jax.ShapeDtypeStruct((n, d), F32), jax.ShapeDtypeStruct((n, d), BF16)],
        compiler_params=_params("parallel"),
        name="outproj",
    )(merged, w_out, x, mod, mod, mod)


def _ffn_kernel(h_ref, wu_ref, wd_ref, x_ref, g_ref, o_ref, acc_ref):
    c = pl.program_id(1)

    @pl.when(c == 0)
    def _():
        acc_ref[...] = jnp.zeros(acc_ref.shape, F32)

    u = jnp.maximum(jnp.dot(h_ref[...], wu_ref[...], preferred_element_type=F32), 0.0)
    acc_ref[...] += jnp.dot((u * u).astype(BF16), wd_ref[...], preferred_element_type=F32)

    @pl.when(c == pl.num_programs(1) - 1)
    def _():
        o_ref[...] = x_ref[...] + g_ref[...] * acc_ref[...]


def _ffn(h2, w_up, w_down, x, mod, seq):
    n, d = x.shape
    hidden = w_up.shape[1]
    tm = _pick(seq, 512)
    tc = _pick(hidden, 512)
    tpb = seq // tm
    row = pl.BlockSpec((tm, d), lambda i, c: (i, 0))
    return pl.pallas_call(
        _ffn_kernel,
        grid=(n // tm, hidden // tc),
        in_specs=[row, pl.BlockSpec((d, tc), lambda i, c: (0, c)), pl.BlockSpec((tc, d), lambda i, c: (c, 0)),
                  row, _mod_spec(d, 5, tpb)],
        out_specs=row,
        out_shape=jax.ShapeDtypeStruct((n, d), F32),
        scratch_shapes=[pltpu.VMEM((tm, d), F32)],
        compiler_params=_params("parallel", "arbitrary"),
        name="ffn",
    )(h2, w_up, w_down, x, mod)


def _pack_in_weights(w_in, a_q_gain, a_k_gain, b_q_gain, b_k_gain, idx_k_gain):
    d = w_in.shape[0]
    sizes = (A_WIDTH, A_WIDTH, A_WIDTH, B_WIDTH, B_WIDTH, B_WIDTH, IDX_WIDTH, IDX_DIM, IDX_HEADS)
    parts, off = [], 0
    for sz in sizes:
        parts.append(w_in[:, off:off + sz])
        off += sz
    aq, ak, av, bq, bk, bv, iq, ik, iw = parts
    w_iv = jnp.concatenate([iq, bv], axis=1).astype(BF16)
    w_qk = jnp.concatenate([bq, bk], axis=1).astype(BF16)
    w_idx = jnp.concatenate([ik, iw, jnp.zeros((d, LANES - IDX_DIM - IDX_HEADS), w_in.dtype)], axis=1).astype(BF16)
    ones = lambda width: jnp.ones((width,), F32)
    iv_gain_cols = ones(IDX_WIDTH + B_WIDTH).reshape(1, -1)
    qk_gain_cols = jnp.concatenate([jnp.tile(b_q_gain * DSA_Q_SCALE, B_HEADS), jnp.tile(b_k_gain, B_HEADS)]).reshape(1, -1)
    idx_gain_row = jnp.concatenate([idx_k_gain, ones(LANES - IDX_DIM)]).reshape(1, LANES)
    w_groups = []
    for g in range(len(A_GROUPS)):
        sl = slice(g * A_GROUP_WIDTH, (g + 1) * A_GROUP_WIDTH)
        w_groups.append(jnp.concatenate([aq[:, sl], ak[:, sl], av[:, sl]], axis=1).astype(BF16))
    a_gain_cols = jnp.concatenate([
        jnp.tile(a_q_gain, A_HEADS_PER_GROUP), jnp.tile(a_k_gain, A_HEADS_PER_GROUP), ones(A_GROUP_WIDTH),
    ]).reshape(1, A_PACK_WIDTH)
    return w_iv, iv_gain_cols, w_qk, qk_gain_cols, w_idx, idx_gain_row, w_groups, a_gain_cols


def kernel(x, c, positions, w_ada, b_ada, w_in, a_q_gain, a_k_gain, b_q_gain, b_k_gain, idx_k_gain,
           w_gate, b_gate, w_proj_a, w_proj_b, w_out, w_up, w_down):
    b, s, d = x.shape
    depth = w_ada.shape[0]
    n = b * s
    topk = min(IDX_TOPK, s // 4)
    assert s % (SEL_TK * SEL_GROUP) == 0 and SEL_TK // 2 >= topk and d % COL_TILE == 0 and s // LANES < 2 ** 15

    tabs = _rope_tables(positions)
    c128, s128, c64, s64 = tabs
    mods = _ada(c, w_ada, b_ada)
    xf = x.reshape(n, d)

    for l in range(depth):
        mod = mods[l]
        w_iv, iv_gain_cols, w_qk, qk_gain_cols, w_idx, idx_gain_row, w_groups, a_gain_cols = _pack_in_weights(
            w_in[l], a_q_gain[l], a_k_gain[l], b_q_gain[l], b_k_gain[l], idx_k_gain[l])

        h = _normmod(xf, mod, s, 1, 0)
        z_iv = _bproj(h, w_iv, iv_gain_cols, tabs, IV_EPILOGUES, "proj_iq_bv").reshape(b, s, -1)
        z_qk = _bproj(h, w_qk, qk_gain_cols, tabs, QK_EPILOGUES, "proj_bq_bk").reshape(b, s, -1)
        ik, iw = _idxproj(h, w_idx, idx_gain_row, c64, s64)

        a_outs, a_lses = [], []
        for g, (window, dilation) in enumerate(A_GROUPS):
            qkv = _aproj(h, w_groups[g], a_gain_cols, tabs, b, s, dilation)
            o, lse = _dilated(qkv, window)
            a_outs.append(o)
            a_lses.append(lse)

        bias5 = _select(z_iv, iw, ik.reshape(b, s, IDX_DIM), topk)
        o_b = _dsa_attention(z_qk, z_iv, bias5).reshape(n, B_WIDTH)

        merged = _merge(a_outs, a_lses, o_b, h, w_gate[l].astype(BF16), b_gate[l].reshape(1, 2 * d),
                        w_proj_a[l].astype(BF16), w_proj_b[l].astype(BF16), s)
        xf, h2 = _outproj(merged, w_out[l].astype(BF16), xf, mod, s)
        xf = _ffn(h2, w_up[l].astype(BF16), w_down[l].astype(BF16), xf, mod, s)

    return xf.reshape(b, s, d)
```

```python
import functools
import math

import jax
import jax.numpy as jnp
from jax import lax
from jax.experimental import pallas as pl
from jax.experimental.pallas import tpu as pltpu

F32 = jnp.float32
BF16 = jnp.bfloat16

HEAD_DIM = 128
LANES = 128
A_GROUPS = ((128, 1), (512, 4), (2048, 16))
A_HEADS_PER_GROUP = 4
A_GROUP_WIDTH = A_HEADS_PER_GROUP * HEAD_DIM
A_WIDTH = len(A_GROUPS) * A_GROUP_WIDTH
B_HEADS = 8
B_WIDTH = B_HEADS * HEAD_DIM
IDX_HEADS = 16
IDX_DIM = 64
IDX_WIDTH = IDX_HEADS * IDX_DIM
IDX_TOPK = 256
ROPE_THETA = 10000.0
EPS = 1e-6
N_MOD = 6
NEG = -1e30
LOG2_E = 1.4426950408889634
DSA_Q_SCALE = HEAD_DIM ** -0.5 * LOG2_E

COL_TILE = 512
MXU_COLS = 256
EPI_ROPE64, EPI_QK, EPI_PLAIN = 0, 1, 2
IV_EPILOGUES = (EPI_ROPE64,) * (IDX_WIDTH // MXU_COLS) + (EPI_PLAIN,) * (B_WIDTH // MXU_COLS)
QK_EPILOGUES = (EPI_QK,) * (2 * B_WIDTH // MXU_COLS)
A_EPILOGUES = (EPI_QK,) * (2 * A_GROUP_WIDTH // MXU_COLS) + (EPI_PLAIN,) * (A_GROUP_WIDTH // MXU_COLS)
A_PACK_WIDTH = 3 * A_GROUP_WIDTH

SEL_TQ = 128
SEL_TK = 512
SEL_GROUP = 4
SEL_UNTESTED_STEPS = (10, 9)
VMEM_LIMIT = 52 * 1024 * 1024


def _params(*sem):
    return pltpu.CompilerParams(dimension_semantics=sem, vmem_limit_bytes=VMEM_LIMIT)


def _pick(n, pref):
    t = pref
    while n % t:
        t //= 2
    return t


def _rms(x, width):
    return x * lax.rsqrt(jnp.sum(x * x, axis=-1, keepdims=True) * (1.0 / width) + EPS)


def _swap_half64(y):
    lane = lax.broadcasted_iota(jnp.int32, y.shape, 1)
    return jnp.where((lane & 63) < 32, pltpu.roll(y, 96, 1), pltpu.roll(y, 32, 1))


def _rope_tables_kernel(pos_ref, f128_ref, g128_ref, f64_ref, g64_ref, c128_ref, s128_ref, c64_ref, s64_ref):
    pos = pos_ref[...]
    a = pos * f128_ref[...]
    c128_ref[...] = jnp.cos(a)
    s128_ref[...] = jnp.sin(a) * g128_ref[...]
    a = pos * f64_ref[...]
    c64_ref[...] = jnp.cos(a)
    s64_ref[...] = jnp.sin(a) * g64_ref[...]


def _rope_tables(positions):
    n = positions.size
    pos = positions.reshape(n, 1).astype(F32)

    def freq(d):
        half = d // 2
        inv = jnp.power(ROPE_THETA, -jnp.arange(half, dtype=F32) * 2.0 / d)
        f = jnp.tile(jnp.concatenate([inv, inv]), LANES // d)
        g = jnp.tile(jnp.concatenate([-jnp.ones((half,), F32), jnp.ones((half,), F32)]), LANES // d)
        return f.reshape(1, LANES), g.reshape(1, LANES)

    f128, g128 = freq(HEAD_DIM)
    f64, g64 = freq(IDX_DIM)
    tm = _pick(n, 1024)
    row = pl.BlockSpec((1, LANES), lambda i: (0, 0))
    tab = pl.BlockSpec((tm, LANES), lambda i: (i, 0))
    return pl.pallas_call(
        _rope_tables_kernel,
        grid=(n // tm,),
        in_specs=[pl.BlockSpec((tm, 1), lambda i: (i, 0)), row, row, row, row],
        out_specs=[tab, tab, tab, tab],
        out_shape=[jax.ShapeDtypeStruct((n, LANES), F32)] * 4,
        compiler_params=_params("parallel"),
        name="rope_tables",
    )(pos, f128, g128, f64, g64)


def _ada_kernel(c_ref, w_ref, b_ref, o_ref):
    c = c_ref[...]
    act = (c * jax.nn.sigmoid(c)).astype(BF16)
    o_ref[...] = jnp.dot(act, w_ref[...].astype(BF16), preferred_element_type=F32) + b_ref[...]


def _ada(c, w_ada, b_ada):
    depth, d, n6 = w_ada.shape
    b = c.shape[0]
    rows = 8
    c_pad = jnp.zeros((rows, d), F32).at[:b].set(c)
    tn = _pick(n6, 1024)
    out = pl.pallas_call(
        _ada_kernel,
        grid=(depth, n6 // tn),
        in_specs=[
            pl.BlockSpec((rows, d), lambda l, j: (0, 0)),
            pl.BlockSpec((None, d, tn), lambda l, j: (l, 0, j)),
            pl.BlockSpec((None, 1, tn), lambda l, j: (l, 0, j)),
        ],
        out_specs=pl.BlockSpec((None, rows, tn), lambda l, j: (l, 0, j)),
        out_shape=jax.ShapeDtypeStruct((depth, rows, n6), F32),
        compiler_params=_params("parallel", "parallel"),
        name="adaln",
    )(c_pad, w_ada, b_ada.reshape(depth, 1, n6))
    return out[:, :b].reshape(depth, b, N_MOD, 1, d)


def _mod_spec(d, which, tiles_per_batch):
    return pl.BlockSpec((None, None, 1, d), lambda i, *_: (i // tiles_per_batch, which, 0, 0))


def _normmod_kernel(x_ref, sc_ref, sh_ref, o_ref):
    x = x_ref[...]
    y = _rms(x, x.shape[-1])
    o_ref[...] = (y * (1.0 + sc_ref[...]) + sh_ref[...]).astype(o_ref.dtype)


def _normmod(x, mod, seq, which_scale, which_shift):
    n, d = x.shape
    tm = _pick(seq, 512)
    tpb = seq // tm
    return pl.pallas_call(
        _normmod_kernel,
        grid=(n // tm,),
        in_specs=[pl.BlockSpec((tm, d), lambda i: (i, 0)), _mod_spec(d, which_scale, tpb), _mod_spec(d, which_shift, tpb)],
        out_specs=pl.BlockSpec((tm, d), lambda i: (i, 0)),
        out_shape=jax.ShapeDtypeStruct((n, d), BF16),
        compiler_params=_params("parallel"),
        name="normmod",
    )(x, mod, mod)


def _proj_kernel(h_ref, w_ref, g_ref, c128_ref, s128_ref, c64_ref, s64_ref, o_ref, z_ref, *, epilogues, streams):
    h = h_ref[...]
    per = h.shape[0] // streams
    for t, kind in enumerate(epilogues):
        z = jnp.dot(h, w_ref[:, t * MXU_COLS:(t + 1) * MXU_COLS], preferred_element_type=F32)
        for c in range(MXU_COLS // LANES):
            slab = t * (MXU_COLS // LANES) + c
            cols = slice(slab * LANES, (slab + 1) * LANES)
            y = z[:, c * LANES:(c + 1) * LANES]
            if kind == EPI_QK:
                y = _rms(y, HEAD_DIM) * g_ref[:, cols]
                y = y * c128_ref[...] + pltpu.roll(y, HEAD_DIM // 2, 1) * s128_ref[...]
            elif kind == EPI_ROPE64:
                y = y * c64_ref[...] + _swap_half64(y) * s64_ref[...]
            if streams == 1:
                o_ref[..., cols] = y.astype(o_ref.dtype).reshape(o_ref.shape[:-1] + (LANES,))
            else:
                z_ref[slab] = y
                for p in range(streams):
                    o_ref[p, :, cols] = z_ref[slab, pl.ds(p, per, stride=streams), :].astype(o_ref.dtype)


def _proj_call(h, w, gain_cols, tabs, epilogues, out_spec, out_shape, tm, streams, name):
    d, width = w.shape
    assert width == len(epilogues) * MXU_COLS
    tab = pl.BlockSpec((tm, LANES), lambda i: (i, 0))
    slabs = width // LANES if streams > 1 else 1
    return pl.pallas_call(
        functools.partial(_proj_kernel, epilogues=epilogues, streams=streams),
        grid=(h.shape[0] // tm,),
        in_specs=[
            pl.BlockSpec((tm, d), lambda i: (i, 0)),
            pl.BlockSpec((d, width), lambda i: (0, 0)),
            pl.BlockSpec((1, width), lambda i: (0, 0)),
            tab, tab, tab, tab,
        ],
        out_specs=out_spec,
        out_shape=out_shape,
        scratch_shapes=[pltpu.VMEM((slabs, tm, LANES), F32)],
        compiler_params=_params("parallel"),
        name=name,
    )(h, w, gain_cols, *tabs)


def _bproj(h, w, gain_cols, tabs, epilogues, name):
    n = h.shape[0]
    tm = _pick(n, 512)
    width = w.shape[1]
    return _proj_call(h, w, gain_cols, tabs, epilogues, pl.BlockSpec((tm, width), lambda i: (i, 0)),
                      jax.ShapeDtypeStruct((n, width), BF16), tm, 1, name)


def _aproj(h, w_group, gain_cols, tabs, batch, seq, dilation):
    r = dilation
    tm = _pick(seq, 512)
    tpb = seq // tm
    assert tm % (r * 16) == 0
    return _proj_call(h, w_group, gain_cols, tabs, A_EPILOGUES,
                      pl.BlockSpec((None, r, tm // r, A_PACK_WIDTH), lambda i: (i // tpb, 0, i % tpb, 0)),
                      jax.ShapeDtypeStruct((batch, r, seq // r, A_PACK_WIDTH), BF16), tm, r, f"aproj_r{r}")


def _idxproj_kernel(h_ref, w_ref, g_ref, c64_ref, s64_ref, ik_ref, iw_ref):
    z = jnp.dot(h_ref[...], w_ref[...], preferred_element_type=F32)
    lane = lax.broadcasted_iota(jnp.int32, z.shape, 1)
    is_k = lane < IDX_DIM
    zk = jnp.where(is_k, z, 0.0)
    y = _rms(zk, IDX_DIM) * g_ref[...]
    y = y * c64_ref[...] + _swap_half64(y) * s64_ref[...]
    ik_ref[...] = y[:, :IDX_DIM].astype(ik_ref.dtype)
    iw_ref[...] = z.T[IDX_DIM:IDX_DIM + IDX_HEADS, :] * (IDX_HEADS ** -0.5 * IDX_DIM ** -0.5)


def _idxproj(h, w_idx, gain_row, c64, s64):
    n, d = h.shape
    tm = _pick(n, 512)
    tab = pl.BlockSpec((tm, LANES), lambda i: (i, 0))
    return pl.pallas_call(
        _idxproj_kernel,
        grid=(n // tm,),
        in_specs=[
            pl.BlockSpec((tm, d), lambda i: (i, 0)),
            pl.BlockSpec((d, LANES), lambda i: (0, 0)),
            pl.BlockSpec((1, LANES), lambda i: (0, 0)),
            tab, tab,
        ],
        out_specs=[pl.BlockSpec((tm, IDX_DIM), lambda i: (i, 0)), pl.BlockSpec((IDX_HEADS, tm), lambda i: (0, i))],
        out_shape=[jax.ShapeDtypeStruct((n, IDX_DIM), BF16), jax.ShapeDtypeStruct((IDX_HEADS, n), F32)],
        compiler_params=_params("parallel"),
        name="idxproj",
    )(h, w_idx, gain_row, c64, s64)


def _dilated_kernel(q_ref, kc_ref, kp_ref, vc_ref, vp_ref, o_ref, lse_ref, *, tq):
    i = pl.program_id(2)
    blk = LANES
    scale = HEAD_DIM ** -0.5
    nkeys = blk + tq
    row = lax.broadcasted_iota(jnp.int32, (tq, nkeys), 0)
    col = lax.broadcasted_iota(jnp.int32, (tq, nkeys), 1)
    dist = row + blk - col
    band = jnp.where(dist >= 0, jnp.where(dist <= blk, 0.0, -jnp.inf), -jnp.inf)
    first = jnp.where(col >= blk, 0.0, -jnp.inf)
    bias = band + jnp.where(i > 0, 0.0, first)
    lane = lax.broadcasted_iota(jnp.int32, (tq, LANES), 1)
    nt = (((1,), (1,)), ((), ()))

    def scores(hh):
        cols = slice(hh * HEAD_DIM, (hh + 1) * HEAD_DIM)
        keys = jnp.concatenate([kp_ref[:, cols], kc_ref[:, cols]], axis=0)
        return lax.dot_general(q_ref[:, cols], keys, nt, preferred_element_type=F32) * scale + bias

    lse_tile = jnp.zeros((tq, LANES), F32)
    s_next = scores(0)
    for hh in range(A_HEADS_PER_GROUP):
        cols = slice(hh * HEAD_DIM, (hh + 1) * HEAD_DIM)
        s = s_next
        if hh + 1 < A_HEADS_PER_GROUP:
            s_next = scores(hh + 1)
        m = jnp.max(s, axis=1, keepdims=True)
        e = jnp.exp(s - m)
        den = jnp.sum(e, axis=1, keepdims=True)
        values = jnp.concatenate([vp_ref[:, cols], vc_ref[:, cols]], axis=0)
        acc = jnp.dot(e.astype(BF16), values, preferred_element_type=F32)
        o_ref[:, cols] = acc / den
        lse_tile = jnp.where(lane == hh, m + jnp.log(den), lse_tile)
    lse_ref[...] = lse_tile


def _dilated(qkv, window):
    b, r, m, _ = qkv.shape
    assert window // r == LANES and m % LANES == 0
    tq = _pick(m, 512)
    nsub = tq // LANES

    def cur(tile):
        return pl.BlockSpec((None, None, tq, COL_TILE), lambda bb, p, i: (bb, p, i, tile))

    def prev(tile):
        return pl.BlockSpec((None, None, LANES, COL_TILE),
                            lambda bb, p, i: (bb, p, jnp.maximum(i * nsub - 1, 0), tile))

    return pl.pallas_call(
        functools.partial(_dilated_kernel, tq=tq),
        grid=(b, r, m // tq),
        in_specs=[cur(0), cur(1), prev(1), cur(2), prev(2)],
        out_specs=[
            pl.BlockSpec((None, None, tq, A_GROUP_WIDTH), lambda bb, p, i: (bb, p, i, 0)),
            pl.BlockSpec((None, None, tq, LANES), lambda bb, p, i: (bb, p, i, 0)),
        ],
        out_shape=[jax.ShapeDtypeStruct((b, r, m, A_GROUP_WIDTH), F32), jax.ShapeDtypeStruct((b, r, m, LANES), F32)],
        compiler_params=_params("parallel", "parallel", "parallel"),
        name=f"dilated_r{r}",
    )(qkv, qkv, qkv, qkv, qkv)


def _key_to_float(key):
    bits = jnp.where(key >= 0, key, key ^ 0x7FFFFFFF)
    return lax.bitcast_convert_type(bits, F32)


def _float_to_key(x):
    bits = lax.bitcast_convert_type(x, jnp.int32)
    return jnp.where(bits >= 0, bits, bits ^ 0x7FFFFFFF)


PACK16 = 16


def _select_kernel(iq_ref, wt_ref, k_ref, bias_ref, qt_ref, sc_ref, hi_ref, lo_ref, gm_ref, *, topk):
    i = pl.program_id(1)
    tq, tk = SEL_TQ, SEL_TK
    nk = sc_ref.shape[0]
    nkb = (i * tq + tq + tk - 1) // tk
    slabs = tk // PACK16

    q_t = iq_ref[...].astype(F32).T
    for h in range(IDX_HEADS):
        qt_ref[:, h * tq:(h + 1) * tq] = q_t[h * IDX_DIM:(h + 1) * IDX_DIM, :].astype(qt_ref.dtype)

    kpos = lax.broadcasted_iota(jnp.int32, (tk, tq), 0)
    qpos = i * tq + lax.broadcasted_iota(jnp.int32, (tk, tq), 1)

    def rows(x):
        return jnp.concatenate([x] * slabs, axis=0)

    def _store_digits(kb, scores):
        key = _float_to_key(scores)
        hi_ref[kb] = (key >> 16).astype(jnp.int16)
        lo_ref[kb] = ((key & 0xFFFF) - 2 ** 15).astype(jnp.int16)

    nfull = (i * tq) // tk

    def score_block(kb, carry, diagonal):
        keys = k_ref[pl.ds(pl.multiple_of(kb * tk, tk), tk), :]
        logits = jnp.dot(keys, qt_ref[...], preferred_element_type=F32)
        acc = jnp.zeros((tk, tq), F32)
        for h in range(IDX_HEADS):
            acc = acc + wt_ref[h:h + 1, :] * jnp.maximum(logits[:, h * tq:(h + 1) * tq], 0.0)
        masked = jnp.where(kpos + kb * tk <= qpos, acc, -jnp.inf) if diagonal else acc
        sc_ref[kb] = masked
        _store_digits(kb, masked)
        gm_ref[...] = jnp.maximum(gm_ref[...], jnp.maximum(masked[:tk // 2], masked[tk // 2:]))
        return carry

    gm_ref[...] = jnp.full(gm_ref.shape, -jnp.inf, F32)
    lax.fori_loop(0, nfull, functools.partial(score_block, diagonal=False), 0)
    lax.fori_loop(nfull, nkb, functools.partial(score_block, diagonal=True), 0)

    def reps(x):
        return jnp.broadcast_to(x, (PACK16, tq))

    bound_lo = reps(_float_to_key(jnp.min(gm_ref[...], axis=0, keepdims=True)) >> 16)
    bound_hi = reps(_float_to_key(jnp.max(gm_ref[...], axis=0, keepdims=True)) >> 16) + 1

    ngr = (nkb + SEL_GROUP - 1) // SEL_GROUP
    nkp = ngr * SEL_GROUP

    def pad_block(kb, carry):
        neg_inf = jnp.full((tk, tq), -jnp.inf, F32)
        sc_ref[kb] = neg_inf
        _store_digits(kb, neg_inf)
        return carry

    lax.fori_loop(nkb, nkp, pad_block, 0)

    def bisect16(ref, need, lo0, hi0, high_digit, untested_steps):
        def count_ge(t):
            t16 = t.astype(jnp.int16)

            def body(g, accs):
                accs = list(accs)
                for j in range(SEL_GROUP):
                    for r in range(slabs):
                        blk = ref[g * SEL_GROUP + j, r * PACK16:(r + 1) * PACK16, :]
                        hit = jnp.where(blk >= t16, jnp.int16(1), jnp.int16(0))
                        accs[r % len(accs)] = accs[r % len(accs)] + hit
                return tuple(accs)

            zero = jnp.zeros((PACK16, tq), jnp.int16)
            accs = lax.fori_loop(0, ngr, body, (zero,) * 4)
            acc = (accs[0] + accs[1]) + (accs[2] + accs[3])
            cnt = jnp.sum(acc.astype(jnp.int32), axis=0, keepdims=True)
            return jnp.broadcast_to(cnt, (PACK16, tq))

        def open_brackets(carry):
            _, lo, hi, _, _ = carry
            return jnp.max(hi - lo) > 1

        def step(carry):
            it, lo, hi, below, above = carry
            mid = (lo + hi) >> 1
            if high_digit:
                log_lo = jnp.log(below.astype(F32))
                frac = (log_lo - math.log(need - 0.5)) / (log_lo - jnp.log(jnp.maximum(above.astype(F32), 0.5)))
                frac = jnp.minimum(jnp.maximum(frac, 0.05), 0.95)
                v_lo, v_hi = _key_to_float((lo << 16) | 0xFFFF), _key_to_float(hi << 16)
                by_value = _float_to_key(v_lo + (v_hi - v_lo) * frac) >> 16
                by_value = jnp.minimum(jnp.maximum(by_value, lo + 1), hi - 1)
                mid = jnp.where(it % 4 == 3, mid, by_value)
            mid = jnp.where(hi - lo > 1, mid, lo)
            cnt = count_ge(mid)
            ge = cnt >= need
            lo, hi = jnp.where(ge, mid, lo), jnp.where(ge, hi, mid)
            below, above = jnp.where(ge, cnt, below), jnp.where(ge, above, cnt)
            if not high_digit:
                hi = jnp.where(cnt == need, mid + 1, hi)
            return it + 1, lo, hi, below, above

        everything = jnp.zeros((PACK16, tq), jnp.int32) + nkp * tk
        carry = (jnp.int32(0), lo0, hi0, everything, jnp.zeros((PACK16, tq), jnp.int32))
        carry = lax.fori_loop(0, untested_steps, lambda _, c: step(c), carry)
        _, lo, _, _, above = lax.while_loop(open_brackets, step, carry)
        return lo, above

    key_hi, above = bisect16(hi_ref, topk, bound_lo, bound_hi, True, SEL_UNTESTED_STEPS[0])
    key_hi16 = key_hi.astype(jnp.int16)

    def low_digits(kb, carry):
        for r in range(slabs):
            sl = slice(r * PACK16, (r + 1) * PACK16)
            lo_ref[kb, sl, :] = jnp.where(hi_ref[kb, sl, :] == key_hi16, lo_ref[kb, sl, :], jnp.int16(-(2 ** 15)))
        return carry

    lax.fori_loop(0, nkp, low_digits, 0)
    digit_lo = jnp.full((PACK16, tq), -(2 ** 15), jnp.int32)
    digit_hi = jnp.full((PACK16, tq), 2 ** 15, jnp.int32)
    key_lo, _ = bisect16(lo_ref, topk - above, digit_lo, digit_hi, False, SEL_UNTESTED_STEPS[1])
    thr_rows = rows(_key_to_float((key_hi << 16) | (key_lo + 2 ** 15)))
    sub = 8

    def rows8(x):
        return jnp.concatenate([x] * (tk // sub), axis=0)

    def key_sum(x):
        parts = [x[r * sub:(r + 1) * sub] for r in range(tk // sub)]
        while len(parts) > 1:
            parts = [a + b for a, b in zip(parts[::2], parts[1::2])]
        return parts[0]

    def per_query(partial):
        return jnp.broadcast_to(jnp.sum(partial, axis=0, keepdims=True), (sub, tq))

    def store_mask(kb, picked, diagonal=True):
        if diagonal:
            picked = jnp.where(kpos + kb * tk <= qpos, picked, NEG)
        bias_ref[kb] = picked.astype(bias_ref.dtype).T

    def write_block(kb, n_ge, diagonal):
        hit = sc_ref[kb] >= thr_rows
        store_mask(kb, jnp.where(hit, 0.0, NEG), diagonal)
        return n_ge + key_sum(jnp.where(hit, 1.0, 0.0))

    n_ge = lax.fori_loop(0, nfull, functools.partial(write_block, diagonal=False), jnp.zeros((sub, tq), F32))
    n_ge = per_query(lax.fori_loop(nfull, nkb, functools.partial(write_block, diagonal=True), n_ge))

    @pl.when(jnp.max(n_ge) > topk)
    def _():
        def count(indicator):
            def body(kb, acc):
                return acc + key_sum(indicator(kb, sc_ref[kb]))
            return per_query(lax.fori_loop(0, nkb, body, jnp.zeros((sub, tq), F32)))

        n_gt = count(lambda kb, x: jnp.where(x > thr_rows, 1.0, 0.0))
        need_eq = topk - n_gt

        def step(_, carry):
            lo, hi = carry
            mid = (lo + hi) >> 1
            mid_rows = rows8(mid)
            tied_upto = count(lambda kb, x: jnp.where(
                x == thr_rows, jnp.where(kpos + kb * tk <= mid_rows, 1.0, 0.0), 0.0))
            ok = tied_upto >= need_eq
            return jnp.where(ok, lo, mid), jnp.where(ok, mid, hi)

        lo0 = jnp.full((sub, tq), -1, jnp.int32)
        hi0 = jnp.zeros((sub, tq), jnp.int32) + (nkb * tk - 1)
        _, last_tied = lax.fori_loop(0, (nk * tk).bit_length(), step, (lo0, hi0))
        last_rows = rows8(last_tied)

        def rewrite_block(kb, carry):
            x = sc_ref[kb]
            tied = jnp.where(kpos + kb * tk <= last_rows, 0.0, NEG)
            store_mask(kb, jnp.where(x > thr_rows, 0.0, jnp.where(x == thr_rows, tied, NEG)))
            return carry

        lax.fori_loop(0, nkb, rewrite_block, 0)

    def fill_block(kb, carry):
        bias_ref[kb] = jnp.full((tq, tk), NEG, bias_ref.dtype)
        return carry

    lax.fori_loop(nkb, nk, fill_block, 0)


def _select(z_iv, iw_t, ik, topk):
    b, s, _ = z_iv.shape
    tq, tk = SEL_TQ, SEL_TK
    nq, nk = s // tq, s // tk
    return pl.pallas_call(
        functools.partial(_select_kernel, topk=topk),
        grid=(b, nq),
        in_specs=[
            pl.BlockSpec((None, tq, IDX_WIDTH), lambda bb, i: (bb, i, 0)),
            pl.BlockSpec((IDX_HEADS, tq), lambda bb, i: (0, bb * nq + i)),
            pl.BlockSpec((None, s, IDX_DIM), lambda bb, i: (bb, 0, 0)),
        ],
        out_specs=pl.BlockSpec((None, None, nk, tq, tk), lambda bb, i: (bb, i, 0, 0, 0)),
        out_shape=jax.ShapeDtypeStruct((b, nq, nk, tq, tk), BF16),
        scratch_shapes=[
            pltpu.VMEM((IDX_DIM, IDX_HEADS * tq), BF16),
            pltpu.VMEM((nk, tk, tq), F32),
            pltpu.VMEM((nk, tk, tq), jnp.int16),
            pltpu.VMEM((nk, tk, tq), jnp.int16),
            pltpu.VMEM((tk // 2, tq), F32),
        ],
        compiler_params=_params("parallel", "parallel"),
        name="dsa_select",
    )(z_iv, iw_t, ik)


def _dsa_kernel(q_ref, k_ref, v_ref, b_ref, o_ref, m_ref, l_ref, acc_ref, *, tq, tk):
    i = pl.program_id(1)
    kb = pl.program_id(2)
    last = (i * tq + tq - 1) // tk
    nt = (((1,), (1,)), ((), ()))

    @pl.when(kb == 0)
    def _():
        m_ref[...] = jnp.full(m_ref.shape, NEG, F32)
        l_ref[...] = jnp.zeros(l_ref.shape, F32)
        acc_ref[...] = jnp.zeros(acc_ref.shape, F32)

    @pl.when(kb <= last)
    def _():
        bias = b_ref[...].reshape(tq, tk).astype(F32)
        rep = tk // LANES

        def scores(h):
            cols = slice(h * HEAD_DIM, (h + 1) * HEAD_DIM)
            return lax.dot_general(q_ref[:, cols], k_ref[:, cols], nt, preferred_element_type=F32) + bias

        s_next = scores(0)
        for h in range(B_HEADS):
            cols = slice(h * HEAD_DIM, (h + 1) * HEAD_DIM)
            s = s_next
            if h + 1 < B_HEADS:
                s_next = scores(h + 1)
            m_prev = m_ref[h]
            m_new = jnp.maximum(m_prev, jnp.max(s, axis=1, keepdims=True))
            alpha = jnp.exp2(m_prev - m_new)
            p = jnp.exp2(s - jnp.concatenate([m_new] * rep, axis=1))
            l_ref[h] = alpha * l_ref[h] + jnp.sum(p, axis=1, keepdims=True)
            acc_ref[:, cols] = alpha * acc_ref[:, cols] + jnp.dot(p.astype(BF16), v_ref[:, cols],
                                                                  preferred_element_type=F32)
            m_ref[h] = m_new

    @pl.when(kb == last)
    def _():
        for h in range(B_HEADS):
            cols = slice(h * HEAD_DIM, (h + 1) * HEAD_DIM)
            o_ref[:, cols] = (acc_ref[:, cols] / l_ref[h]).astype(o_ref.dtype)


def _dsa_attention(z_qk, z_iv, bias5):
    b, s, _ = z_qk.shape
    tk = SEL_TK
    tq = _pick(s, 512)
    sub = tq // SEL_TQ
    assert IDX_WIDTH == B_WIDTH

    def last(i):
        return (i * tq + tq - 1) // tk

    return pl.pallas_call(
        functools.partial(_dsa_kernel, tq=tq, tk=tk),
        grid=(b, s // tq, s // tk),
        in_specs=[
            pl.BlockSpec((None, tq, B_WIDTH), lambda bb, i, kb: (bb, i, 0)),
            pl.BlockSpec((None, tk, B_WIDTH), lambda bb, i, kb: (bb, jnp.minimum(kb, last(i)), 1)),
            pl.BlockSpec((None, tk, B_WIDTH), lambda bb, i, kb: (bb, jnp.minimum(kb, last(i)), 1)),
            pl.BlockSpec((None, sub, None, SEL_TQ, tk), lambda bb, i, kb: (bb, i, jnp.minimum(kb, last(i)), 0, 0)),
        ],
        out_specs=pl.BlockSpec((None, tq, B_WIDTH), lambda bb, i, kb: (bb, i, 0)),
        out_shape=jax.ShapeDtypeStruct((b, s, B_WIDTH), BF16),
        scratch_shapes=[
            pltpu.VMEM((B_HEADS, tq, LANES), F32),
            pltpu.VMEM((B_HEADS, tq, LANES), F32),
            pltpu.VMEM((tq, B_WIDTH), F32),
        ],
        compiler_params=_params("parallel", "parallel", "arbitrary"),
        name="dsa_attention",
    )(z_qk, z_qk, z_iv, bias5)


def _to_token_order(dst_ref, first, src_ref):
    r, per, width = src_ref.shape
    for c in range(width // LANES):
        cols = slice(c * LANES, (c + 1) * LANES)
        if r == 1:
            dst_ref[first + c] = src_ref[0, :, cols]
        else:
            for p in range(r):
                dst_ref[first + c, pl.ds(p, per, stride=r), :] = src_ref[p, :, cols]


def _merge_kernel(o1_ref, o2_ref, o3_ref, l1_ref, l2_ref, l3_ref, ob_ref, h_ref,
                  wga_ref, wgb_ref, bga_ref, bgb_ref, wpa_ref, wpb_ref, out_ref, oa_ref, ot_ref, lt_ref):
    @pl.when(pl.program_id(1) == 0)
    def _():
        nh = A_HEADS_PER_GROUP
        for g, (o_ref, l_ref) in enumerate(((o1_ref, l1_ref), (o2_ref, l2_ref), (o3_ref, l3_ref))):
            _to_token_order(ot_ref, g * nh, o_ref)
            _to_token_order(lt_ref, g, l_ref)
        l1, l2, l3 = lt_ref[0], lt_ref[1], lt_ref[2]
        mx = jnp.maximum(jnp.maximum(l1, l2), l3)
        e1, e2, e3 = jnp.exp(l1 - mx), jnp.exp(l2 - mx), jnp.exp(l3 - mx)
        tot = e1 + e2 + e3
        w1, w2, w3 = e1 / tot, e2 / tot, e3 / tot
        for hh in range(nh):
            oa = (w1[:, hh:hh + 1] * ot_ref[hh] + w2[:, hh:hh + 1] * ot_ref[nh + hh]
                  + w3[:, hh:hh + 1] * ot_ref[2 * nh + hh])
            oa_ref[:, hh * HEAD_DIM:(hh + 1) * HEAD_DIM] = oa.astype(oa_ref.dtype)

    h = h_ref[...]
    ga = jax.nn.sigmoid(jnp.dot(h, wga_ref[...], preferred_element_type=F32) + bga_ref[...])
    gb = jax.nn.sigmoid(jnp.dot(h, wgb_ref[...], preferred_element_type=F32) + bgb_ref[...])
    pa = jnp.dot(oa_ref[...], wpa_ref[...], preferred_element_type=F32)
    pb = jnp.dot(ob_ref[...], wpb_ref[...], preferred_element_type=F32)
    out_ref[...] = (ga * pa + gb * pb).astype(out_ref.dtype)


def _merge(outs, lses, o_b, h, w_gate, b_gate, w_proj_a, w_proj_b, seq):
    n, d = h.shape
    tm = _pick(seq, 512)
    tpb = seq // tm
    tn = _pick(d, COL_TILE)
    nj = d // tn
    row = lambda width: pl.BlockSpec((tm, width), lambda i, j: (i, 0))

    def streams(arr):
        r, width = arr.shape[1], arr.shape[3]
        return pl.BlockSpec((None, r, tm // r, width), lambda i, j: (i // tpb, 0, i % tpb, 0))

    return pl.pallas_call(
        _merge_kernel,
        grid=(n // tm, nj),
        in_specs=[
            *[streams(a) for a in outs], *[streams(a) for a in lses],
            row(B_WIDTH), row(d),
            pl.BlockSpec((d, tn), lambda i, j: (0, j)),
            pl.BlockSpec((d, tn), lambda i, j: (0, nj + j)),
            pl.BlockSpec((1, tn), lambda i, j: (0, j)),
            pl.BlockSpec((1, tn), lambda i, j: (0, nj + j)),
            pl.BlockSpec((A_GROUP_WIDTH, tn), lambda i, j: (0, j)),
            pl.BlockSpec((B_WIDTH, tn), lambda i, j: (0, j)),
        ],
        out_specs=pl.BlockSpec((tm, tn), lambda i, j: (i, j)),
        out_shape=jax.ShapeDtypeStruct((n, d), BF16),
        scratch_shapes=[
            pltpu.VMEM((tm, A_GROUP_WIDTH), BF16),
            pltpu.VMEM((len(outs) * A_HEADS_PER_GROUP, tm, HEAD_DIM), F32),
            pltpu.VMEM((len(lses), tm, LANES), F32),
        ],
        compiler_params=_params("parallel", "arbitrary"),
        name="gated_merge",
    )(*outs, *lses, o_b, h, w_gate, w_gate, b_gate, b_gate, w_proj_a, w_proj_b)


def _outproj_kernel(mg_ref, w_ref, x_ref, g_ref, sc_ref, sh_ref, xo_ref, ho_ref):
    mix = jnp.dot(mg_ref[...], w_ref[...], preferred_element_type=F32)
    x = x_ref[...] + g_ref[...] * mix
    xo_ref[...] = x
    ho_ref[...] = (_rms(x, x.shape[-1]) * (1.0 + sc_ref[...]) + sh_ref[...]).astype(ho_ref.dtype)


def _outproj(merged, w_out, x, mod, seq):
    n, d = x.shape
    tm = _pick(seq, 256)
    tpb = seq // tm
    row = pl.BlockSpec((tm, d), lambda i: (i, 0))
    return pl.pallas_call(
        _outproj_kernel,
        grid=(n // tm,),
        in_specs=[row, pl.BlockSpec((d, d), lambda i: (0, 0)), row,
                  _mod_spec(d, 2, tpb), _mod_spec(d, 4, tpb), _mod_spec(d, 3, tpb)],
        out_specs=[row, row],
        out_shape=[jax.ShapeDtypeStruct((n, d), F32), jax.ShapeDtypeStruct((n, d), BF16)],
        compiler_params=_params("parallel"),
        name="outproj",
    )(merged, w_out, x, mod, mod, mod)


def _ffn_kernel(h_ref, wu_ref, wd_ref, x_ref, g_ref, o_ref, acc_ref):
    c = pl.program_id(1)

    @pl.when(c == 0)
    def _():
        acc_ref[...] = jnp.zeros(acc_ref.shape, F32)

    u = jnp.maximum(jnp.dot(h_ref[...], wu_ref[...], preferred_element_type=F32), 0.0)
    acc_ref[...] += jnp.dot((u * u).astype(BF16), wd_ref[...], preferred_element_type=F32)

    @pl.when(c == pl.num_programs(1) - 1)
    def _():
        o_ref[...] = x_ref[...] + g_ref[...] * acc_ref[...]


def _ffn(h2, w_up, w_down, x, mod, seq):
    n, d = x.shape
    hidden = w_up.shape[1]
    tm = _pick(seq, 512)
    tc = _pick(hidden, 512)
    tpb = seq // tm
    row = pl.BlockSpec((tm, d), lambda i, c: (i, 0))
    return pl.pallas_call(
        _ffn_kernel,
        grid=(n // tm, hidden // tc),
        in_specs=[row, pl.BlockSpec((d, tc), lambda i, c: (0, c)), pl.BlockSpec((tc, d), lambda i, c: (c, 0)),
                  row, _mod_spec(d, 5, tpb)],
        out_specs=row,
        out_shape=jax.ShapeDtypeStruct((n, d), F32),
        scratch_shapes=[pltpu.VMEM((tm, d), F32)],
        compiler_params=_params("parallel", "arbitrary"),
        name="ffn",
    )(h2, w_up, w_down, x, mod)


def _pack_in_weights(w_in, a_q_gain, a_k_gain, b_q_gain, b_k_gain, idx_k_gain):
    d = w_in.shape[0]
    sizes = (A_WIDTH, A_WIDTH, A_WIDTH, B_WIDTH, B_WIDTH, B_WIDTH, IDX_WIDTH, IDX_DIM, IDX_HEADS)
    parts, off = [], 0
    for sz in sizes:
        parts.append(w_in[:, off:off + sz])
        off += sz
    aq, ak, av, bq, bk, bv, iq, ik, iw = parts
    w_iv = jnp.concatenate([iq, bv], axis=1).astype(BF16)
    w_qk = jnp.concatenate([bq, bk], axis=1).astype(BF16)
    w_idx = jnp.concatenate([ik, iw, jnp.zeros((d, LANES - IDX_DIM - IDX_HEADS), w_in.dtype)], axis=1).astype(BF16)
    ones = lambda width: jnp.ones((width,), F32)
    iv_gain_cols = ones(IDX_WIDTH + B_WIDTH).reshape(1, -1)
    qk_gain_cols = jnp.concatenate([jnp.tile(b_q_gain * DSA_Q_SCALE, B_HEADS), jnp.tile(b_k_gain, B_HEADS)]).reshape(1, -1)
    idx_gain_row = jnp.concatenate([idx_k_gain, ones(LANES - IDX_DIM)]).reshape(1, LANES)
    w_groups = []
    for g in range(len(A_GROUPS)):
        sl = slice(g * A_GROUP_WIDTH, (g + 1) * A_GROUP_WIDTH)
        w_groups.append(jnp.concatenate([aq[:, sl], ak[:, sl], av[:, sl]], axis=1).astype(BF16))
    a_gain_cols = jnp.concatenate([
        jnp.tile(a_q_gain, A_HEADS_PER_GROUP), jnp.tile(a_k_gain, A_HEADS_PER_GROUP), ones(A_GROUP_WIDTH),
    ]).reshape(1, A_PACK_WIDTH)
    return w_iv, iv_gain_cols, w_qk, qk_gain_cols, w_idx, idx_gain_row, w_groups, a_gain_cols


def kernel(x, c, positions, w_ada, b_ada, w_in, a_q_gain, a_k_gain, b_q_gain, b_k_gain, idx_k_gain,
           w_gate, b_gate, w_proj_a, w_proj_b, w_out, w_up, w_down):
    b, s, d = x.shape
    depth = w_ada.shape[0]
    n = b * s
    topk = min(IDX_TOPK, s // 4)
    assert s % (SEL_TK * SEL_GROUP) == 0 and SEL_TK // 2 >= topk and d % COL_TILE == 0 and s // LANES < 2 ** 15

    tabs = _rope_tables(positions)
    c128, s128, c64, s64 = tabs
    mods = _ada(c, w_ada, b_ada)
    xf = x.reshape(n, d)

    for l in range(depth):
        mod = mods[l]
        w_iv, iv_gain_cols, w_qk, qk_gain_cols, w_idx, idx_gain_row, w_groups, a_gain_cols = _pack_in_weights(
            w_in[l], a_q_gain[l], a_k_gain[l], b_q_gain[l], b_k_gain[l], idx_k_gain[l])

        h = _normmod(xf, mod, s, 1, 0)
        z_iv = _bproj(h, w_iv, iv_gain_cols, tabs, IV_EPILOGUES, "proj_iq_bv").reshape(b, s, -1)
        z_qk = _bproj(h, w_qk, qk_gain_cols, tabs, QK_EPILOGUES, "proj_bq_bk").reshape(b, s, -1)
        ik, iw = _idxproj(h, w_idx, idx_gain_row, c64, s64)

        a_outs, a_lses = [], []
        for g, (window, dilation) in enumerate(A_GROUPS):
            qkv = _aproj(h, w_groups[g], a_gain_cols, tabs, b, s, dilation)
            o, lse = _dilated(qkv, window)
            a_outs.append(o)
            a_lses.append(lse)

        bias5 = _select(z_iv, iw, ik.reshape(b, s, IDX_DIM), topk)
        o_b = _dsa_attention(z_qk, z_iv, bias5).reshape(n, B_WIDTH)

        merged = _merge(a_outs, a_lses, o_b, h, w_gate[l].astype(BF16), b_gate[l].reshape(1, 2 * d),
                        w_proj_a[l].astype(BF16), w_proj_b[l].astype(BF16), s)
        xf, h2 = _outproj(merged, w_out[l].astype(BF16), xf, mod, s)
        xf = _ffn(h2, w_up[l].astype(BF16), w_down[l].astype(BF16), xf, mod, s)

    return xf.reshape(b, s, d)
```

```python
import functools
import math

import jax
import jax.numpy as jnp
from jax import lax
from jax.experimental import pallas as pl
from jax.experimental.pallas import tpu as pltpu

F32 = jnp.float32
BF16 = jnp.bfloat16

HEAD_DIM = 128
LANES = 128
A_GROUPS = ((128, 1), (512, 4), (2048, 16))
A_HEADS_PER_GROUP = 4
A_GROUP_WIDTH = A_HEADS_PER_GROUP * HEAD_DIM
A_WIDTH = len(A_GROUPS) * A_GROUP_WIDTH
B_HEADS = 8
B_WIDTH = B_HEADS * HEAD_DIM
IDX_HEADS = 16
IDX_DIM = 64
IDX_WIDTH = IDX_HEADS * IDX_DIM
IDX_TOPK = 256
ROPE_THETA = 10000.0
EPS = 1e-6
N_MOD = 6
NEG = -1e30
LOG2_E = 1.4426950408889634
DSA_Q_SCALE = HEAD_DIM ** -0.5 * LOG2_E

COL_TILE = 512
MXU_COLS = 256
EPI_ROPE64, EPI_QK, EPI_PLAIN = 0, 1, 2
IV_EPILOGUES = (EPI_ROPE64,) * (IDX_WIDTH // MXU_COLS) + (EPI_PLAIN,) * (B_WIDTH // MXU_COLS)
QK_EPILOGUES = (EPI_QK,) * (2 * B_WIDTH // MXU_COLS)
A_EPILOGUES = (EPI_QK,) * (2 * A_GROUP_WIDTH // MXU_COLS) + (EPI_PLAIN,) * (A_GROUP_WIDTH // MXU_COLS)
A_PACK_WIDTH = 3 * A_GROUP_WIDTH

SEL_TQ = 128
SEL_TK = 512
SEL_GROUP = 4
SEL_UNTESTED_STEPS = (10, 9)
VMEM_LIMIT = 52 * 1024 * 1024


def _params(*sem):
    return pltpu.CompilerParams(dimension_semantics=sem, vmem_limit_bytes=VMEM_LIMIT)


def _pick(n, pref):
    t = pref
    while n % t:
        t //= 2
    return t


def _rms(x, width):
    return x * lax.rsqrt(jnp.sum(x * x, axis=-1, keepdims=True) * (1.0 / width) + EPS)


def _swap_half64(y):
    lane = lax.broadcasted_iota(jnp.int32, y.shape, 1)
    return jnp.where((lane & 63) < 32, pltpu.roll(y, 96, 1), pltpu.roll(y, 32, 1))


def _rope_tables_kernel(pos_ref, f128_ref, g128_ref, f64_ref, g64_ref, c128_ref, s128_ref, c64_ref, s64_ref):
    pos = pos_ref[...]
    a = pos * f128_ref[...]
    c128_ref[...] = jnp.cos(a)
    s128_ref[...] = jnp.sin(a) * g128_ref[...]
    a = pos * f64_ref[...]
    c64_ref[...] = jnp.cos(a)
    s64_ref[...] = jnp.sin(a) * g64_ref[...]


def _rope_tables(positions):
    n = positions.size
    pos = positions.reshape(n, 1).astype(F32)

    def freq(d):
        half = d // 2
        inv = jnp.power(ROPE_THETA, -jnp.arange(half, dtype=F32) * 2.0 / d)
        f = jnp.tile(jnp.concatenate([inv, inv]), LANES // d)
        g = jnp.tile(jnp.concatenate([-jnp.ones((half,), F32), jnp.ones((half,), F32)]), LANES // d)
        return f.reshape(1, LANES), g.reshape(1, LANES)

    f128, g128 = freq(HEAD_DIM)
    f64, g64 = freq(IDX_DIM)
    tm = _pick(n, 1024)
    row = pl.BlockSpec((1, LANES), lambda i: (0, 0))
    tab = pl.BlockSpec((tm, LANES), lambda i: (i, 0))
    return pl.pallas_call(
        _rope_tables_kernel,
        grid=(n // tm,),
        in_specs=[pl.BlockSpec((tm, 1), lambda i: (i, 0)), row, row, row, row],
        out_specs=[tab, tab, tab, tab],
        out_shape=[jax.ShapeDtypeStruct((n, LANES), F32)] * 4,
        compiler_params=_params("parallel"),
        name="rope_tables",
    )(pos, f128, g128, f64, g64)


def _ada_kernel(c_ref, w_ref, b_ref, o_ref):
    c = c_ref[...]
    act = (c * jax.nn.sigmoid(c)).astype(BF16)
    o_ref[...] = jnp.dot(act, w_ref[...].astype(BF16), preferred_element_type=F32) + b_ref[...]


def _ada(c, w_ada, b_ada):
    depth, d, n6 = w_ada.shape
    b = c.shape[0]
    rows = 8
    c_pad = jnp.zeros((rows, d), F32).at[:b].set(c)
    tn = _pick(n6, 1024)
    out = pl.pallas_call(
        _ada_kernel,
        grid=(depth, n6 // tn),
        in_specs=[
            pl.BlockSpec((rows, d), lambda l, j: (0, 0)),
            pl.BlockSpec((None, d, tn), lambda l, j: (l, 0, j)),
            pl.BlockSpec((None, 1, tn), lambda l, j: (l, 0, j)),
        ],
        out_specs=pl.BlockSpec((None, rows, tn), lambda l, j: (l, 0, j)),
        out_shape=jax.ShapeDtypeStruct((depth, rows, n6), F32),
        compiler_params=_params("parallel", "parallel"),
        name="adaln",
    )(c_pad, w_ada, b_ada.reshape(depth, 1, n6))
    return out[:, :b].reshape(depth, b, N_MOD, 1, d)


def _mod_spec(d, which, tiles_per_batch):
    return pl.BlockSpec((None, None, 1, d), lambda i, *_: (i // tiles_per_batch, which, 0, 0))


def _normmod_kernel(x_ref, sc_ref, sh_ref, o_ref):
    x = x_ref[...]
    y = _rms(x, x.shape[-1])
    o_ref[...] = (y * (1.0 + sc_ref[...]) + sh_ref[...]).astype(o_ref.dtype)


def _normmod(x, mod, seq, which_scale, which_shift):
    n, d = x.shape
    tm = _pick(seq, 512)
    tpb = seq // tm
    return pl.pallas_call(
        _normmod_kernel,
        grid=(n // tm,),
        in_specs=[pl.BlockSpec((tm, d), lambda i: (i, 0)), _mod_spec(d, which_scale, tpb), _mod_spec(d, which_shift, tpb)],
        out_specs=pl.BlockSpec((tm, d), lambda i: (i, 0)),
        out_shape=jax.ShapeDtypeStruct((n, d), BF16),
        compiler_params=_params("parallel"),
        name="normmod",
    )(x, mod, mod)


def _proj_kernel(h_ref, w_ref, g_ref, c128_ref, s128_ref, c64_ref, s64_ref, o_ref, z_ref, *, epilogues, streams):
    h = h_ref[...]
    per = h.shape[0] // streams
    for t, kind in enumerate(epilogues):
        z = jnp.dot(h, w_ref[:, t * MXU_COLS:(t + 1) * MXU_COLS], preferred_element_type=F32)
        for c in range(MXU_COLS // LANES):
            slab = t * (MXU_COLS // LANES) + c
            cols = slice(slab * LANES, (slab + 1) * LANES)
            y = z[:, c * LANES:(c + 1) * LANES]
            if kind == EPI_QK:
                y = _rms(y, HEAD_DIM) * g_ref[:, cols]
                y = y * c128_ref[...] + pltpu.roll(y, HEAD_DIM // 2, 1) * s128_ref[...]
            elif kind == EPI_ROPE64:
                y = y * c64_ref[...] + _swap_half64(y) * s64_ref[...]
            if streams == 1:
                o_ref[..., cols] = y.astype(o_ref.dtype).reshape(o_ref.shape[:-1] + (LANES,))
            else:
                z_ref[slab] = y
                for p in range(streams):
                    o_ref[p, :, cols] = z_ref[slab, pl.ds(p, per, stride=streams), :].astype(o_ref.dtype)


def _proj_call(h, w, gain_cols, tabs, epilogues, out_spec, out_shape, tm, streams, name):
    d, width = w.shape
    assert width == len(epilogues) * MXU_COLS
    tab = pl.BlockSpec((tm, LANES), lambda i: (i, 0))
    slabs = width // LANES if streams > 1 else 1
    return pl.pallas_call(
        functools.partial(_proj_kernel, epilogues=epilogues, streams=streams),
        grid=(h.shape[0] // tm,),
        in_specs=[
            pl.BlockSpec((tm, d), lambda i: (i, 0)),
            pl.BlockSpec((d, width), lambda i: (0, 0)),
            pl.BlockSpec((1, width), lambda i: (0, 0)),
            tab, tab, tab, tab,
        ],
        out_specs=out_spec,
        out_shape=out_shape,
        scratch_shapes=[pltpu.VMEM((slabs, tm, LANES), F32)],
        compiler_params=_params("parallel"),
        name=name,
    )(h, w, gain_cols, *tabs)


def _bproj(h, w, gain_cols, tabs, epilogues, name):
    n = h.shape[0]
    tm = _pick(n, 512)
    width = w.shape[1]
    return _proj_call(h, w, gain_cols, tabs, epilogues, pl.BlockSpec((tm, width), lambda i: (i, 0)),
                      jax.ShapeDtypeStruct((n, width), BF16), tm, 1, name)


def _aproj(h, w_group, gain_cols, tabs, batch, seq, dilation):
    r = dilation
    tm = _pick(seq, 512)
    tpb = seq // tm
    assert tm % (r * 16) == 0
    return _proj_call(h, w_group, gain_cols, tabs, A_EPILOGUES,
                      pl.BlockSpec((None, r, tm // r, A_PACK_WIDTH), lambda i: (i // tpb, 0, i % tpb, 0)),
                      jax.ShapeDtypeStruct((batch, r, seq // r, A_PACK_WIDTH), BF16), tm, r, f"aproj_r{r}")


def _idxproj_kernel(h_ref, w_ref, g_ref, c64_ref, s64_ref, ik_ref, iw_ref):
    z = jnp.dot(h_ref[...], w_ref[...], preferred_element_type=F32)
    lane = lax.broadcasted_iota(jnp.int32, z.shape, 1)
    is_k = lane < IDX_DIM
    zk = jnp.where(is_k, z, 0.0)
    y = _rms(zk, IDX_DIM) * g_ref[...]
    y = y * c64_ref[...] + _swap_half64(y) * s64_ref[...]
    ik_ref[...] = y[:, :IDX_DIM].astype(ik_ref.dtype)
    iw_ref[...] = z.T[IDX_DIM:IDX_DIM + IDX_HEADS, :] * (IDX_HEADS ** -0.5 * IDX_DIM ** -0.5)


def _idxproj(h, w_idx, gain_row, c64, s64):
    n, d = h.shape
    tm = _pick(n, 512)
    tab = pl.BlockSpec((tm, LANES), lambda i: (i, 0))
    return pl.pallas_call(
        _idxproj_kernel,
        grid=(n // tm,),
        in_specs=[
            pl.BlockSpec((tm, d), lambda i: (i, 0)),
            pl.BlockSpec((d, LANES), lambda i: (0, 0)),
            pl.BlockSpec((1, LANES), lambda i: (0, 0)),
            tab, tab,
        ],
        out_specs=[pl.BlockSpec((tm, IDX_DIM), lambda i: (i, 0)), pl.BlockSpec((IDX_HEADS, tm), lambda i: (0, i))],
        out_shape=[jax.ShapeDtypeStruct((n, IDX_DIM), BF16), jax.ShapeDtypeStruct((IDX_HEADS, n), F32)],
        compiler_params=_params("parallel"),
        name="idxproj",
    )(h, w_idx, gain_row, c64, s64)


def _dilated_kernel(q_ref, kc_ref, kp_ref, vc_ref, vp_ref, o_ref, lse_ref, *, tq):
    i = pl.program_id(2)
    blk = LANES
    scale = HEAD_DIM ** -0.5
    nkeys = blk + tq
    row = lax.broadcasted_iota(jnp.int32, (tq, nkeys), 0)
    col = lax.broadcasted_iota(jnp.int32, (tq, nkeys), 1)
    dist = row + blk - col
    band = jnp.where(dist >= 0, jnp.where(dist <= blk, 0.0, -jnp.inf), -jnp.inf)
    first = jnp.where(col >= blk, 0.0, -jnp.inf)
    bias = band + jnp.where(i > 0, 0.0, first)
    lane = lax.broadcasted_iota(jnp.int32, (tq, LANES), 1)
    nt = (((1,), (1,)), ((), ()))

    def scores(hh):
        cols = slice(hh * HEAD_DIM, (hh + 1) * HEAD_DIM)
        keys = jnp.concatenate([kp_ref[:, cols], kc_ref[:, cols]], axis=0)
        return lax.dot_general(q_ref[:, cols], keys, nt, preferred_element_type=F32) * scale + bias

    lse_tile = jnp.zeros((tq, LANES), F32)
    s_next = scores(0)
    for hh in range(A_HEADS_PER_GROUP):
        cols = slice(hh * HEAD_DIM, (hh + 1) * HEAD_DIM)
        s = s_next
        if hh + 1 < A_HEADS_PER_GROUP:
            s_next = scores(hh + 1)
        m = jnp.max(s, axis=1, keepdims=True)
        e = jnp.exp(s - m)
        den = jnp.sum(e, axis=1, keepdims=True)
        values = jnp.concatenate([vp_ref[:, cols], vc_ref[:, cols]], axis=0)
        acc = jnp.dot(e.astype(BF16), values, preferred_element_type=F32)
        o_ref[:, cols] = acc / den
        lse_tile = jnp.where(lane == hh, m + jnp.log(den), lse_tile)
    lse_ref[...] = lse_tile


def _dilated(qkv, window):
    b, r, m, _ = qkv.shape
    assert window // r == LANES and m % LANES == 0
    tq = _pick(m, 512)
    nsub = tq // LANES

    def cur(tile):
        return pl.BlockSpec((None, None, tq, COL_TILE), lambda bb, p, i: (bb, p, i, tile))

    def prev(tile):
        return pl.BlockSpec((None, None, LANES, COL_TILE),
                            lambda bb, p, i: (bb, p, jnp.maximum(i * nsub - 1, 0), tile))

    return pl.pallas_call(
        functools.partial(_dilated_kernel, tq=tq),
        grid=(b, r, m // tq),
        in_specs=[cur(0), cur(1), prev(1), cur(2), prev(2)],
        out_specs=[
            pl.BlockSpec((None, None, tq, A_GROUP_WIDTH), lambda bb, p, i: (bb, p, i, 0)),
            pl.BlockSpec((None, None, tq, LANES), lambda bb, p, i: (bb, p, i, 0)),
        ],
        out_shape=[jax.ShapeDtypeStruct((b, r, m, A_GROUP_WIDTH), F32), jax.ShapeDtypeStruct((b, r, m, LANES), F32)],
        compiler_params=_params("parallel", "parallel", "parallel"),
        name=f"dilated_r{r}",
    )(qkv, qkv, qkv, qkv, qkv)


def _key_to_float(key):
    bits = jnp.where(key >= 0, key, key ^ 0x7FFFFFFF)
    return lax.bitcast_convert_type(bits, F32)


def _float_to_key(x):
    bits = lax.bitcast_convert_type(x, jnp.int32)
    return jnp.where(bits >= 0, bits, bits ^ 0x7FFFFFFF)


PACK16 = 16


def _select_kernel(iq_ref, wt_ref, k_ref, bias_ref, qt_ref, sc_ref, hi_ref, lo_ref, gm_ref, lg_ref, *, topk):
    i = pl.program_id(1)
    tq, tk = SEL_TQ, SEL_TK
    nk = sc_ref.shape[0]
    nkb = (i * tq + tq + tk - 1) // tk
    slabs = tk // PACK16

    q_t = iq_ref[...].astype(F32).T
    for h in range(IDX_HEADS):
        qt_ref[:, h * tq:(h + 1) * tq] = q_t[h * IDX_DIM:(h + 1) * IDX_DIM, :].astype(qt_ref.dtype)

    kpos = lax.broadcasted_iota(jnp.int32, (tk, tq), 0)
    qpos = i * tq + lax.broadcasted_iota(jnp.int32, (tk, tq), 1)

    def rows(x):
        return jnp.concatenate([x] * slabs, axis=0)

    def _store_digits(kb, scores):
        key = _float_to_key(scores)
        hi_ref[kb] = (key >> 16).astype(jnp.int16)
        lo_ref[kb] = ((key & 0xFFFF) - 2 ** 15).astype(jnp.int16)

    nfull = (i * tq) // tk

    def logits_into(slot, kb):
        keys = k_ref[pl.ds(pl.multiple_of(kb * tk, tk), tk), :]
        lg_ref[slot] = jnp.dot(keys, qt_ref[...], preferred_element_type=F32)

    def score_block(slot, kb, diagonal):
        acc = jnp.zeros((tk, tq), F32)
        for h in range(IDX_HEADS):
            acc = acc + wt_ref[h:h + 1, :] * jnp.maximum(lg_ref[slot, :, h * tq:(h + 1) * tq], 0.0)
        masked = jnp.where(kpos + kb * tk <= qpos, acc, -jnp.inf) if diagonal else acc
        sc_ref[kb] = masked
        _store_digits(kb, masked)
        gm_ref[...] = jnp.maximum(gm_ref[...], jnp.maximum(masked[:tk // 2], masked[tk // 2:]))

    def score_pair(g, carry, diagonal):
        kb = 2 * g
        logits_into(1, kb + 1)
        score_block(0, kb, diagonal)
        logits_into(0, jnp.minimum(kb + 2, nk - 1))
        score_block(1, kb + 1, diagonal)
        return carry

    gm_ref[...] = jnp.full(gm_ref.shape, -jnp.inf, F32)
    logits_into(0, 0)
    lax.fori_loop(0, nfull // 2, functools.partial(score_pair, diagonal=False), 0)
    lax.fori_loop(nfull // 2, (nkb + 1) // 2, functools.partial(score_pair, diagonal=True), 0)

    def reps(x):
        return jnp.broadcast_to(x, (PACK16, tq))

    bound_lo = reps(_float_to_key(jnp.min(gm_ref[...], axis=0, keepdims=True)) >> 16)
    bound_hi = reps(_float_to_key(jnp.max(gm_ref[...], axis=0, keepdims=True)) >> 16) + 1

    ngr = (nkb + SEL_GROUP - 1) // SEL_GROUP
    nkp = ngr * SEL_GROUP

    def pad_block(kb, carry):
        neg_inf = jnp.full((tk, tq), -jnp.inf, F32)
        sc_ref[kb] = neg_inf
        _store_digits(kb, neg_inf)
        return carry

    lax.fori_loop(nkb, nkp, pad_block, 0)

    def bisect16(ref, need, lo0, hi0, high_digit, untested_steps):
        def count_ge(t):
            t16 = t.astype(jnp.int16)

            def body(g, accs):
                accs = list(accs)
                for j in range(SEL_GROUP):
                    for r in range(slabs):
                        blk = ref[g * SEL_GROUP + j, r * PACK16:(r + 1) * PACK16, :]
                        hit = jnp.where(blk >= t16, jnp.int16(1), jnp.int16(0))
                        accs[r % len(accs)] = accs[r % len(accs)] + hit
                return tuple(accs)

            zero = jnp.zeros((PACK16, tq), jnp.int16)
            accs = lax.fori_loop(0, ngr, body, (zero,) * 4)
            acc = (accs[0] + accs[1]) + (accs[2] + accs[3])
            cnt = jnp.sum(acc.astype(jnp.int32), axis=0, keepdims=True)
            return jnp.broadcast_to(cnt, (PACK16, tq))

        def open_brackets(carry):
            _, lo, hi, _, _ = carry
            return jnp.max(hi - lo) > 1

        def step(carry):
            it, lo, hi, below, above = carry
            mid = (lo + hi) >> 1
            if high_digit:
                log_lo = jnp.log(below.astype(F32))
                frac = (log_lo - math.log(need - 0.5)) / (log_lo - jnp.log(jnp.maximum(above.astype(F32), 0.5)))
                frac = jnp.minimum(jnp.maximum(frac, 0.05), 0.95)
                v_lo, v_hi = _key_to_float((lo << 16) | 0xFFFF), _key_to_float(hi << 16)
                by_value = _float_to_key(v_lo + (v_hi - v_lo) * frac) >> 16
                by_value = jnp.minimum(jnp.maximum(by_value, lo + 1), hi - 1)
                mid = jnp.where(it % 4 == 3, mid, by_value)
            mid = jnp.where(hi - lo > 1, mid, lo)
            cnt = count_ge(mid)
            ge = cnt >= need
            lo, hi = jnp.where(ge, mid, lo), jnp.where(ge, hi, mid)
            below, above = jnp.where(ge, cnt, below), jnp.where(ge, above, cnt)
            if not high_digit:
                hi = jnp.where(cnt == need, mid + 1, hi)
            return it + 1, lo, hi, below, above

        everything = jnp.zeros((PACK16, tq), jnp.int32) + nkp * tk
        carry = (jnp.int32(0), lo0, hi0, everything, jnp.zeros((PACK16, tq), jnp.int32))
        carry = lax.fori_loop(0, untested_steps, lambda _, c: step(c), carry)
        _, lo, _, _, above = lax.while_loop(open_brackets, step, carry)
        return lo, above

    key_hi, above = bisect16(hi_ref, topk, bound_lo, bound_hi, True, SEL_UNTESTED_STEPS[0])
    key_hi16 = key_hi.astype(jnp.int16)

    def low_digits(kb, carry):
        for r in range(slabs):
            sl = slice(r * PACK16, (r + 1) * PACK16)
            lo_ref[kb, sl, :] = jnp.where(hi_ref[kb, sl, :] == key_hi16, lo_ref[kb, sl, :], jnp.int16(-(2 ** 15)))
        return carry

    lax.fori_loop(0, nkp, low_digits, 0)
    digit_lo = jnp.full((PACK16, tq), -(2 ** 15), jnp.int32)
    digit_hi = jnp.full((PACK16, tq), 2 ** 15, jnp.int32)
    key_lo, _ = bisect16(lo_ref, topk - above, digit_lo, digit_hi, False, SEL_UNTESTED_STEPS[1])
    thr_rows = rows(_key_to_float((key_hi << 16) | (key_lo + 2 ** 15)))
    sub = 8

    def rows8(x):
        return jnp.concatenate([x] * (tk // sub), axis=0)

    def key_sum(x):
        parts = [x[r * sub:(r + 1) * sub] for r in range(tk // sub)]
        while len(parts) > 1:
            parts = [a + b for a, b in zip(parts[::2], parts[1::2])]
        return parts[0]

    def per_query(partial):
        return jnp.broadcast_to(jnp.sum(partial, axis=0, keepdims=True), (sub, tq))

    def store_mask(kb, picked, diagonal=True):
        if diagonal:
            picked = jnp.where(kpos + kb * tk <= qpos, picked, NEG)
        bias_ref[kb] = picked.astype(bias_ref.dtype).T

    def write_block(kb, n_ge, diagonal):
        hit = sc_ref[kb] >= thr_rows
        store_mask(kb, jnp.where(hit, 0.0, NEG), diagonal)
        return n_ge + key_sum(jnp.where(hit, 1.0, 0.0))

    n_ge = lax.fori_loop(0, nfull, functools.partial(write_block, diagonal=False), jnp.zeros((sub, tq), F32))
    n_ge = per_query(lax.fori_loop(nfull, nkb, functools.partial(write_block, diagonal=True), n_ge))

    @pl.when(jnp.max(n_ge) > topk)
    def _():
        def count(indicator):
            def body(kb, acc):
                return acc + key_sum(indicator(kb, sc_ref[kb]))
            return per_query(lax.fori_loop(0, nkb, body, jnp.zeros((sub, tq), F32)))

        n_gt = count(lambda kb, x: jnp.where(x > thr_rows, 1.0, 0.0))
        need_eq = topk - n_gt

        def step(_, carry):
            lo, hi = carry
            mid = (lo + hi) >> 1
            mid_rows = rows8(mid)
            tied_upto = count(lambda kb, x: jnp.where(
                x == thr_rows, jnp.where(kpos + kb * tk <= mid_rows, 1.0, 0.0), 0.0))
            ok = tied_upto >= need_eq
            return jnp.where(ok, lo, mid), jnp.where(ok, mid, hi)

        lo0 = jnp.full((sub, tq), -1, jnp.int32)
        hi0 = jnp.zeros((sub, tq), jnp.int32) + (nkb * tk - 1)
        _, last_tied = lax.fori_loop(0, (nk * tk).bit_length(), step, (lo0, hi0))
        last_rows = rows8(last_tied)

        def rewrite_block(kb, carry):
            x = sc_ref[kb]
            tied = jnp.where(kpos + kb * tk <= last_rows, 0.0, NEG)
            store_mask(kb, jnp.where(x > thr_rows, 0.0, jnp.where(x == thr_rows, tied, NEG)))
            return carry

        lax.fori_loop(0, nkb, rewrite_block, 0)

    def fill_block(kb, carry):
        bias_ref[kb] = jnp.full((tq, tk), NEG, bias_ref.dtype)
        return carry

    lax.fori_loop(nkb, nk, fill_block, 0)


def _select(z_iv, iw_t, ik, topk):
    b, s, _ = z_iv.shape
    tq, tk = SEL_TQ, SEL_TK
    nq, nk = s // tq, s // tk
    return pl.pallas_call(
        functools.partial(_select_kernel, topk=topk),
        grid=(b, nq),
        in_specs=[
            pl.BlockSpec((None, tq, IDX_WIDTH), lambda bb, i: (bb, i, 0)),
            pl.BlockSpec((IDX_HEADS, tq), lambda bb, i: (0, bb * nq + i)),
            pl.BlockSpec((None, s, IDX_DIM), lambda bb, i: (bb, 0, 0)),
        ],
        out_specs=pl.BlockSpec((None, None, nk, tq, tk), lambda bb, i: (bb, i, 0, 0, 0)),
        out_shape=jax.ShapeDtypeStruct((b, nq, nk, tq, tk), BF16),
        scratch_shapes=[
            pltpu.VMEM((IDX_DIM, IDX_HEADS * tq), BF16),
            pltpu.VMEM((nk, tk, tq), F32),
            pltpu.VMEM((nk, tk, tq), jnp.int16),
            pltpu.VMEM((nk, tk, tq), jnp.int16),
            pltpu.VMEM((tk // 2, tq), F32),
            pltpu.VMEM((2, tk, IDX_HEADS * tq), F32),
        ],
        compiler_params=_params("parallel", "parallel"),
        name="dsa_select",
    )(z_iv, iw_t, ik)


def _dsa_kernel(q_ref, k_ref, v_ref, b_ref, o_ref, m_ref, l_ref, acc_ref, *, tq, tk):
    i = pl.program_id(1)
    kb = pl.program_id(2)
    last = (i * tq + tq - 1) // tk
    nt = (((1,), (1,)), ((), ()))

    @pl.when(kb == 0)
    def _():
        m_ref[...] = jnp.full(m_ref.shape, NEG, F32)
        l_ref[...] = jnp.zeros(l_ref.shape, F32)
        acc_ref[...] = jnp.zeros(acc_ref.shape, F32)

    @pl.when(kb <= last)
    def _():
        bias = b_ref[...].reshape(tq, tk).astype(F32)
        rep = tk // LANES

        def scores(h):
            cols = slice(h * HEAD_DIM, (h + 1) * HEAD_DIM)
            return lax.dot_general(q_ref[:, cols], k_ref[:, cols], nt, preferred_element_type=F32) + bias

        s_next = scores(0)
        for h in range(B_HEADS):
            cols = slice(h * HEAD_DIM, (h + 1) * HEAD_DIM)
            s = s_next
            if h + 1 < B_HEADS:
                s_next = scores(h + 1)
            m_prev = m_ref[h]
            m_new = jnp.maximum(m_prev, jnp.max(s, axis=1, keepdims=True))
            alpha = jnp.exp2(m_prev - m_new)
            p = jnp.exp2(s - jnp.concatenate([m_new] * rep, axis=1))
            l_ref[h] = alpha * l_ref[h] + jnp.sum(p, axis=1, keepdims=True)
            acc_ref[:, cols] = alpha * acc_ref[:, cols] + jnp.dot(p.astype(BF16), v_ref[:, cols],
                                                                  preferred_element_type=F32)
            m_ref[h] = m_new

    @pl.when(kb == last)
    def _():
        for h in range(B_HEADS):
            cols = slice(h * HEAD_DIM, (h + 1) * HEAD_DIM)
            o_ref[:, cols] = (acc_ref[:, cols] / l_ref[h]).astype(o_ref.dtype)


def _dsa_attention(z_qk, z_iv, bias5):
    b, s, _ = z_qk.shape
    tk = SEL_TK
    tq = _pick(s, 512)
    sub = tq // SEL_TQ
    assert IDX_WIDTH == B_WIDTH

    def last(i):
        return (i * tq + tq - 1) // tk

    return pl.pallas_call(
        functools.partial(_dsa_kernel, tq=tq, tk=tk),
        grid=(b, s // tq, s // tk),
        in_specs=[
            pl.BlockSpec((None, tq, B_WIDTH), lambda bb, i, kb: (bb, i, 0)),
            pl.BlockSpec((None, tk, B_WIDTH), lambda bb, i, kb: (bb, jnp.minimum(kb, last(i)), 1)),
            pl.BlockSpec((None, tk, B_WIDTH), lambda bb, i, kb: (bb, jnp.minimum(kb, last(i)), 1)),
            pl.BlockSpec((None, sub, None, SEL_TQ, tk), lambda bb, i, kb: (bb, i, jnp.minimum(kb, last(i)), 0, 0)),
        ],
        out_specs=pl.BlockSpec((None, tq, B_WIDTH), lambda bb, i, kb: (bb, i, 0)),
        out_shape=jax.ShapeDtypeStruct((b, s, B_WIDTH), BF16),
        scratch_shapes=[
            pltpu.VMEM((B_HEADS, tq, LANES), F32),
            pltpu.VMEM((B_HEADS, tq, LANES), F32),
            pltpu.VMEM((tq, B_WIDTH), F32),
        ],
        compiler_params=_params("parallel", "parallel", "arbitrary"),
        name="dsa_attention",
    )(z_qk, z_qk, z_iv, bias5)


def _to_token_order(dst_ref, first, src_ref):
    r, per, width = src_ref.shape
    for c in range(width // LANES):
        cols = slice(c * LANES, (c + 1) * LANES)
        if r == 1:
            dst_ref[first + c] = src_ref[0, :, cols]
        else:
            for p in range(r):
                dst_ref[first + c, pl.ds(p, per, stride=r), :] = src_ref[p, :, cols]


def _merge_kernel(o1_ref, o2_ref, o3_ref, l1_ref, l2_ref, l3_ref, ob_ref, h_ref,
                  wga_ref, wgb_ref, bga_ref, bgb_ref, wpa_ref, wpb_ref, out_ref, oa_ref, ot_ref, lt_ref):
    @pl.when(pl.program_id(1) == 0)
    def _():
        nh = A_HEADS_PER_GROUP
        for g, (o_ref, l_ref) in enumerate(((o1_ref, l1_ref), (o2_ref, l2_ref), (o3_ref, l3_ref))):
            _to_token_order(ot_ref, g * nh, o_ref)
            _to_token_order(lt_ref, g, l_ref)
        l1, l2, l3 = lt_ref[0], lt_ref[1], lt_ref[2]
        mx = jnp.maximum(jnp.maximum(l1, l2), l3)
        e1, e2, e3 = jnp.exp(l1 - mx), jnp.exp(l2 - mx), jnp.exp(l3 - mx)
        tot = e1 + e2 + e3
        w1, w2, w3 = e1 / tot, e2 / tot, e3 / tot
        for hh in range(nh):
            oa = (w1[:, hh:hh + 1] * ot_ref[hh] + w2[:, hh:hh + 1] * ot_ref[nh + hh]
                  + w3[:, hh:hh + 1] * ot_ref[2 * nh + hh])
            oa_ref[:, hh * HEAD_DIM:(hh + 1) * HEAD_DIM] = oa.astype(oa_ref.dtype)

    h = h_ref[...]
    ga = jax.nn.sigmoid(jnp.dot(h, wga_ref[...], preferred_element_type=F32) + bga_ref[...])
    gb = jax.nn.sigmoid(jnp.dot(h, wgb_ref[...], preferred_element_type=F32) + bgb_ref[...])
    pa = jnp.dot(oa_ref[...], wpa_ref[...], preferred_element_type=F32)
    pb = jnp.dot(ob_ref[...], wpb_ref[...], preferred_element_type=F32)
    out_ref[...] = (ga * pa + gb * pb).astype(out_ref.dtype)


def _merge(outs, lses, o_b, h, w_gate, b_gate, w_proj_a, w_proj_b, seq):
    n, d = h.shape
    tm = _pick(seq, 512)
    tpb = seq // tm
    tn = _pick(d, COL_TILE)
    nj = d // tn
    row = lambda width: pl.BlockSpec((tm, width), lambda i, j: (i, 0))

    def streams(arr):
        r, width = arr.shape[1], arr.shape[3]
        return pl.BlockSpec((None, r, tm // r, width), lambda i, j: (i // tpb, 0, i % tpb, 0))

    return pl.pallas_call(
        _merge_kernel,
        grid=(n // tm, nj),
        in_specs=[
            *[streams(a) for a in outs], *[streams(a) for a in lses],
            row(B_WIDTH), row(d),
            pl.BlockSpec((d, tn), lambda i, j: (0, j)),
            pl.BlockSpec((d, tn), lambda i, j: (0, nj + j)),
            pl.BlockSpec((1, tn), lambda i, j: (0, j)),
            pl.BlockSpec((1, tn), lambda i, j: (0, nj + j)),
            pl.BlockSpec((A_GROUP_WIDTH, tn), lambda i, j: (0, j)),
            pl.BlockSpec((B_WIDTH, tn), lambda i, j: (0, j)),
        ],
        out_specs=pl.BlockSpec((tm, tn), lambda i, j: (i, j)),
        out_shape=jax.ShapeDtypeStruct((n, d), BF16),
        scratch_shapes=[
            pltpu.VMEM((tm, A_GROUP_WIDTH), BF16),
            pltpu.VMEM((len(outs) * A_HEADS_PER_GROUP, tm, HEAD_DIM), F32),
            pltpu.VMEM((len(lses), tm, LANES), F32),
        ],
        compiler_params=_params("parallel", "arbitrary"),
        name="gated_merge",
    )(*outs, *lses, o_b, h, w_gate, w_gate, b_gate, b_gate, w_proj_a, w_proj_b)


def _outproj_kernel(mg_ref, w_ref, x_ref, g_ref, sc_ref, sh_ref, xo_ref, ho_ref):
    mix = jnp.dot(mg_ref[...], w_ref[...], preferred_element_type=F32)
    x = x_ref[...] + g_ref[...] * mix
    xo_ref[...] = x
    ho_ref[...] = (_rms(x, x.shape[-1]) * (1.0 + sc_ref[...]) + sh_ref[...]).astype(ho_ref.dtype)


def _outproj(merged, w_out, x, mod, seq):
    n, d = x.shape
    tm = _pick(seq, 256)
    tpb = seq // tm
    row = pl.BlockSpec((tm, d), lambda i: (i, 0))
    return pl.pallas_call(
        _outproj_kernel,
        grid=(n // tm,),
        in_specs=[row, pl.BlockSpec((d, d), lambda i: (0, 0)), row,
                  _mod_spec(d, 2, tpb), _mod_spec(d, 4, tpb), _mod_spec(d, 3, tpb)],
        out_specs=[row, row],
        out_shape=[jax.ShapeDtypeStruct((n, d), F32), jax.ShapeDtypeStruct((n, d), BF16)],
        compiler_params=_params("parallel"),
        name="outproj",
    )(merged, w_out, x, mod, mod, mod)


def _ffn_kernel(h_ref, wu_ref, wd_ref, x_ref, g_ref, o_ref, acc_ref):
    c = pl.program_id(1)

    @pl.when(c == 0)
    def _():
        acc_ref[...] = jnp.zeros(acc_ref.shape, F32)

    u = jnp.maximum(jnp.dot(h_ref[...], wu_ref[...], preferred_element_type=F32), 0.0)
    acc_ref[...] += jnp.dot((u * u).astype(BF16), wd_ref[...], preferred_element_type=F32)

    @pl.when(c == pl.num_programs(1) - 1)
    def _():
        o_ref[...] = x_ref[...] + g_ref[...] * acc_ref[...]


def _ffn(h2, w_up, w_down, x, mod, seq):
    n, d = x.shape
    hidden = w_up.shape[1]
    tm = _pick(seq, 512)
    tc = _pick(hidden, 512)
    tpb = seq // tm
    row = pl.BlockSpec((tm, d), lambda i, c: (i, 0))
    return pl.pallas_call(
        _ffn_kernel,
        grid=(n // tm, hidden // tc),
        in_specs=[row, pl.BlockSpec((d, tc), lambda i, c: (0, c)), pl.BlockSpec((tc, d), lambda i, c: (c, 0)),
                  row, _mod_spec(d, 5, tpb)],
        out_specs=row,
        out_shape=jax.ShapeDtypeStruct((n, d), F32),
        scratch_shapes=[pltpu.VMEM((tm, d), F32)],
        compiler_params=_params("parallel", "arbitrary"),
        name="ffn",
    )(h2, w_up, w_down, x, mod)


def _pack_in_weights(w_in, a_q_gain, a_k_gain, b_q_gain, b_k_gain, idx_k_gain):
    d = w_in.shape[0]
    sizes = (A_WIDTH, A_WIDTH, A_WIDTH, B_WIDTH, B_WIDTH, B_WIDTH, IDX_WIDTH, IDX_DIM, IDX_HEADS)
    parts, off = [], 0
    for sz in sizes:
        parts.append(w_in[:, off:off + sz])
        off += sz
    aq, ak, av, bq, bk, bv, iq, ik, iw = parts
    w_iv = jnp.concatenate([iq, bv], axis=1).astype(BF16)
    w_qk = jnp.concatenate([bq, bk], axis=1).astype(BF16)
    w_idx = jnp.concatenate([ik, iw, jnp.zeros((d, LANES - IDX_DIM - IDX_HEADS), w_in.dtype)], axis=1).astype(BF16)
    ones = lambda width: jnp.ones((width,), F32)
    iv_gain_cols = ones(IDX_WIDTH + B_WIDTH).reshape(1, -1)
    qk_gain_cols = jnp.concatenate([jnp.tile(b_q_gain * DSA_Q_SCALE, B_HEADS), jnp.tile(b_k_gain, B_HEADS)]).reshape(1, -1)
    idx_gain_row = jnp.concatenate([idx_k_gain, ones(LANES - IDX_DIM)]).reshape(1, LANES)
    w_groups = []
    for g in range(len(A_GROUPS)):
        sl = slice(g * A_GROUP_WIDTH, (g + 1) * A_GROUP_WIDTH)
        w_groups.append(jnp.concatenate([aq[:, sl], ak[:, sl], av[:, sl]], axis=1).astype(BF16))
    a_gain_cols = jnp.concatenate([
        jnp.tile(a_q_gain, A_HEADS_PER_GROUP), jnp.tile(a_k_gain, A_HEADS_PER_GROUP), ones(A_GROUP_WIDTH),
    ]).reshape(1, A_PACK_WIDTH)
    return w_iv, iv_gain_cols, w_qk, qk_gain_cols, w_idx, idx_gain_row, w_groups, a_gain_cols


def kernel(x, c, positions, w_ada, b_ada, w_in, a_q_gain, a_k_gain, b_q_gain, b_k_gain, idx_k_gain,
           w_gate, b_gate, w_proj_a, w_proj_b, w_out, w_up, w_down):
    b, s, d = x.shape
    depth = w_ada.shape[0]
    n = b * s
    topk = min(IDX_TOPK, s // 4)
    assert s % (SEL_TK * SEL_GROUP) == 0 and SEL_TK // 2 >= topk and d % COL_TILE == 0 and s // LANES < 2 ** 15

    tabs = _rope_tables(positions)
    c128, s128, c64, s64 = tabs
    mods = _ada(c, w_ada, b_ada)
    xf = x.reshape(n, d)

    for l in range(depth):
        mod = mods[l]
        w_iv, iv_gain_cols, w_qk, qk_gain_cols, w_idx, idx_gain_row, w_groups, a_gain_cols = _pack_in_weights(
            w_in[l], a_q_gain[l], a_k_gain[l], b_q_gain[l], b_k_gain[l], idx_k_gain[l])

        h = _normmod(xf, mod, s, 1, 0)
        z_iv = _bproj(h, w_iv, iv_gain_cols, tabs, IV_EPILOGUES, "proj_iq_bv").reshape(b, s, -1)
        z_qk = _bproj(h, w_qk, qk_gain_cols, tabs, QK_EPILOGUES, "proj_bq_bk").reshape(b, s, -1)
        ik, iw = _idxproj(h, w_idx, idx_gain_row, c64, s64)

        a_outs, a_lses = [], []
        for g, (window, dilation) in enumerate(A_GROUPS):
            qkv = _aproj(h, w_groups[g], a_gain_cols, tabs, b, s, dilation)
            o, lse = _dilated(qkv, window)
            a_outs.append(o)
            a_lses.append(lse)

        bias5 = _select(z_iv, iw, ik.reshape(b, s, IDX_DIM), topk)
        o_b = _dsa_attention(z_qk, z_iv, bias5).reshape(n, B_WIDTH)

        merged = _merge(a_outs, a_lses, o_b, h, w_gate[l].astype(BF16), b_gate[l].reshape(1, 2 * d),
                        w_proj_a[l].astype(BF16), w_proj_b[l].astype(BF16), s)
        xf, h2 = _outproj(merged, w_out[l].astype(BF16), xf, mod, s)
        xf = _ffn(h2, w_up[l].astype(BF16), w_down[l].astype(BF16), xf, mod, s)

    return xf.reshape(b, s, d)
```

```python
import functools
import math

import jax
import jax.numpy as jnp
from jax import lax
from jax.experimental import pallas as pl
from jax.experimental.pallas import tpu as pltpu

F32 = jnp.float32
BF16 = jnp.bfloat16

HEAD_DIM = 128
LANES = 128
A_GROUPS = ((128, 1), (512, 4), (2048, 16))
A_HEADS_PER_GROUP = 4
A_GROUP_WIDTH = A_HEADS_PER_GROUP * HEAD_DIM
A_WIDTH = len(A_GROUPS) * A_GROUP_WIDTH
B_HEADS = 8
B_WIDTH = B_HEADS * HEAD_DIM
IDX_HEADS = 16
IDX_DIM = 64
IDX_WIDTH = IDX_HEADS * IDX_DIM
IDX_TOPK = 256
ROPE_THETA = 10000.0
EPS = 1e-6
N_MOD = 6
NEG = -1e30
LOG2_E = 1.4426950408889634
DSA_Q_SCALE = HEAD_DIM ** -0.5 * LOG2_E

COL_TILE = 512
MXU_COLS = 256
EPI_ROPE64, EPI_QK, EPI_PLAIN = 0, 1, 2
B_EPILOGUES = ((EPI_QK,) * (2 * B_WIDTH // MXU_COLS) + (EPI_ROPE64,) * (IDX_WIDTH // MXU_COLS)
               + (EPI_PLAIN,) * (B_WIDTH // MXU_COLS))
COL_BQ, COL_BK, COL_IQ, COL_BV = 0, 1, 2, 3
assert IDX_WIDTH == B_WIDTH
A_EPILOGUES = (EPI_QK,) * (2 * A_GROUP_WIDTH // MXU_COLS) + (EPI_PLAIN,) * (A_GROUP_WIDTH // MXU_COLS)
A_PACK_WIDTH = 3 * A_GROUP_WIDTH

SEL_TQ = 128
SEL_TK = 512
SEL_GROUP = 4
SEL_UNTESTED_STEPS = (10, 9)
VMEM_LIMIT = 52 * 1024 * 1024


def _params(*sem):
    return pltpu.CompilerParams(dimension_semantics=sem, vmem_limit_bytes=VMEM_LIMIT)


def _pick(n, pref):
    t = pref
    while n % t:
        t //= 2
    return t


def _rms(x, width):
    return x * lax.rsqrt(jnp.sum(x * x, axis=-1, keepdims=True) * (1.0 / width) + EPS)


def _swap_half64(y):
    lane = lax.broadcasted_iota(jnp.int32, y.shape, 1)
    return jnp.where((lane & 63) < 32, pltpu.roll(y, 96, 1), pltpu.roll(y, 32, 1))


def _rope_tables_kernel(pos_ref, f128_ref, g128_ref, f64_ref, g64_ref, c128_ref, s128_ref, c64_ref, s64_ref):
    pos = pos_ref[...]
    a = pos * f128_ref[...]
    c128_ref[...] = jnp.cos(a)
    s128_ref[...] = jnp.sin(a) * g128_ref[...]
    a = pos * f64_ref[...]
    c64_ref[...] = jnp.cos(a)
    s64_ref[...] = jnp.sin(a) * g64_ref[...]


def _rope_tables(positions):
    n = positions.size
    pos = positions.reshape(n, 1).astype(F32)

    def freq(d):
        half = d // 2
        inv = jnp.power(ROPE_THETA, -jnp.arange(half, dtype=F32) * 2.0 / d)
        f = jnp.tile(jnp.concatenate([inv, inv]), LANES // d)
        g = jnp.tile(jnp.concatenate([-jnp.ones((half,), F32), jnp.ones((half,), F32)]), LANES // d)
        return f.reshape(1, LANES), g.reshape(1, LANES)

    f128, g128 = freq(HEAD_DIM)
    f64, g64 = freq(IDX_DIM)
    tm = _pick(n, 1024)
    row = pl.BlockSpec((1, LANES), lambda i: (0, 0))
    tab = pl.BlockSpec((tm, LANES), lambda i: (i, 0))
    return pl.pallas_call(
        _rope_tables_kernel,
        grid=(n // tm,),
        in_specs=[pl.BlockSpec((tm, 1), lambda i: (i, 0)), row, row, row, row],
        out_specs=[tab, tab, tab, tab],
        out_shape=[jax.ShapeDtypeStruct((n, LANES), F32)] * 4,
        compiler_params=_params("parallel"),
        name="rope_tables",
    )(pos, f128, g128, f64, g64)


def _ada_kernel(c_ref, w_ref, b_ref, o_ref):
    c = c_ref[...]
    act = (c * jax.nn.sigmoid(c)).astype(BF16)
    o_ref[...] = jnp.dot(act, w_ref[...].astype(BF16), preferred_element_type=F32) + b_ref[...]


def _ada(c, w_ada, b_ada):
    depth, d, n6 = w_ada.shape
    b = c.shape[0]
    rows = 8
    c_pad = jnp.zeros((rows, d), F32).at[:b].set(c)
    tn = _pick(n6, 1024)
    out = pl.pallas_call(
        _ada_kernel,
        grid=(depth, n6 // tn),
        in_specs=[
            pl.BlockSpec((rows, d), lambda l, j: (0, 0)),
            pl.BlockSpec((None, d, tn), lambda l, j: (l, 0, j)),
            pl.BlockSpec((None, 1, tn), lambda l, j: (l, 0, j)),
        ],
        out_specs=pl.BlockSpec((None, rows, tn), lambda l, j: (l, 0, j)),
        out_shape=jax.ShapeDtypeStruct((depth, rows, n6), F32),
        compiler_params=_params("parallel", "parallel"),
        name="adaln",
    )(c_pad, w_ada, b_ada.reshape(depth, 1, n6))
    return out[:, :b].reshape(depth, b, N_MOD, 1, d)


def _mod_spec(d, which, tiles_per_batch):
    return pl.BlockSpec((None, None, 1, d), lambda i, *_: (i // tiles_per_batch, which, 0, 0))


def _normmod_kernel(x_ref, sc_ref, sh_ref, o_ref):
    x = x_ref[...]
    y = _rms(x, x.shape[-1])
    o_ref[...] = (y * (1.0 + sc_ref[...]) + sh_ref[...]).astype(o_ref.dtype)


def _normmod(x, mod, seq, which_scale, which_shift):
    n, d = x.shape
    tm = _pick(seq, 512)
    tpb = seq // tm
    return pl.pallas_call(
        _normmod_kernel,
        grid=(n // tm,),
        in_specs=[pl.BlockSpec((tm, d), lambda i: (i, 0)), _mod_spec(d, which_scale, tpb), _mod_spec(d, which_shift, tpb)],
        out_specs=pl.BlockSpec((tm, d), lambda i: (i, 0)),
        out_shape=jax.ShapeDtypeStruct((n, d), BF16),
        compiler_params=_params("parallel"),
        name="normmod",
    )(x, mod, mod)


def _proj_kernel(h_ref, w_ref, g_ref, c128_ref, s128_ref, c64_ref, s64_ref, o_ref, z_ref, *, epilogues, streams):
    h = h_ref[...]
    per = h.shape[0] // streams
    for t, kind in enumerate(epilogues):
        z = jnp.dot(h, w_ref[:, t * MXU_COLS:(t + 1) * MXU_COLS], preferred_element_type=F32)
        for c in range(MXU_COLS // LANES):
            slab = t * (MXU_COLS // LANES) + c
            cols = slice(slab * LANES, (slab + 1) * LANES)
            y = z[:, c * LANES:(c + 1) * LANES]
            if kind == EPI_QK:
                y = _rms(y, HEAD_DIM) * g_ref[:, cols]
                y = y * c128_ref[...] + pltpu.roll(y, HEAD_DIM // 2, 1) * s128_ref[...]
            elif kind == EPI_ROPE64:
                y = y * c64_ref[...] + _swap_half64(y) * s64_ref[...]
            if streams == 1:
                o_ref[..., cols] = y.astype(o_ref.dtype).reshape(o_ref.shape[:-1] + (LANES,))
            else:
                z_ref[slab] = y
                for p in range(streams):
                    o_ref[p, :, cols] = z_ref[slab, pl.ds(p, per, stride=streams), :].astype(o_ref.dtype)


def _proj_call(h, w, gain_cols, tabs, epilogues, out_spec, out_shape, tm, streams, name):
    d, width = w.shape
    assert width == len(epilogues) * MXU_COLS
    tab = pl.BlockSpec((tm, LANES), lambda i: (i, 0))
    slabs = width // LANES if streams > 1 else 1
    return pl.pallas_call(
        functools.partial(_proj_kernel, epilogues=epilogues, streams=streams),
        grid=(h.shape[0] // tm,),
        in_specs=[
            pl.BlockSpec((tm, d), lambda i: (i, 0)),
            pl.BlockSpec((d, width), lambda i: (0, 0), pipeline_mode=pl.Buffered(1)),
            pl.BlockSpec((1, width), lambda i: (0, 0)),
            tab, tab, tab, tab,
        ],
        out_specs=out_spec,
        out_shape=out_shape,
        scratch_shapes=[pltpu.VMEM((slabs, tm, LANES), F32)],
        compiler_params=_params("parallel"),
        name=name,
    )(h, w, gain_cols, *tabs)


def _bproj(h, w, gain_cols, tabs, epilogues, name):
    n = h.shape[0]
    tm = _pick(n, 512)
    width = w.shape[1]
    return _proj_call(h, w, gain_cols, tabs, epilogues, pl.BlockSpec((tm, width), lambda i: (i, 0)),
                      jax.ShapeDtypeStruct((n, width), BF16), tm, 1, name)


def _aproj(h, w_group, gain_cols, tabs, batch, seq, dilation):
    r = dilation
    tm = _pick(seq, 512)
    tpb = seq // tm
    assert tm % (r * 16) == 0
    return _proj_call(h, w_group, gain_cols, tabs, A_EPILOGUES,
                      pl.BlockSpec((None, r, tm // r, A_PACK_WIDTH), lambda i: (i // tpb, 0, i % tpb, 0)),
                      jax.ShapeDtypeStruct((batch, r, seq // r, A_PACK_WIDTH), BF16), tm, r, f"aproj_r{r}")


def _idxproj_kernel(h_ref, w_ref, g_ref, c64_ref, s64_ref, ik_ref, iw_ref):
    z = jnp.dot(h_ref[...], w_ref[...], preferred_element_type=F32)
    lane = lax.broadcasted_iota(jnp.int32, z.shape, 1)
    is_k = lane < IDX_DIM
    zk = jnp.where(is_k, z, 0.0)
    y = _rms(zk, IDX_DIM) * g_ref[...]
    y = y * c64_ref[...] + _swap_half64(y) * s64_ref[...]
    ik_ref[...] = y[:, :IDX_DIM].astype(ik_ref.dtype)
    iw_ref[...] = z.T[IDX_DIM:IDX_DIM + IDX_HEADS, :] * (IDX_HEADS ** -0.5 * IDX_DIM ** -0.5)


def _idxproj(h, w_idx, gain_row, c64, s64):
    n, d = h.shape
    tm = _pick(n, 512)
    tab = pl.BlockSpec((tm, LANES), lambda i: (i, 0))
    return pl.pallas_call(
        _idxproj_kernel,
        grid=(n // tm,),
        in_specs=[
            pl.BlockSpec((tm, d), lambda i: (i, 0)),
            pl.BlockSpec((d, LANES), lambda i: (0, 0)),
            pl.BlockSpec((1, LANES), lambda i: (0, 0)),
            tab, tab,
        ],
        out_specs=[pl.BlockSpec((tm, IDX_DIM), lambda i: (i, 0)), pl.BlockSpec((IDX_HEADS, tm), lambda i: (0, i))],
        out_shape=[jax.ShapeDtypeStruct((n, IDX_DIM), BF16), jax.ShapeDtypeStruct((IDX_HEADS, n), F32)],
        compiler_params=_params("parallel"),
        name="idxproj",
    )(h, w_idx, gain_row, c64, s64)


def _dilated_kernel(q_ref, kc_ref, kp_ref, vc_ref, vp_ref, o_ref, lse_ref, *, tq):
    i = pl.program_id(2)
    blk = LANES
    scale = HEAD_DIM ** -0.5
    nkeys = blk + tq
    row = lax.broadcasted_iota(jnp.int32, (tq, nkeys), 0)
    col = lax.broadcasted_iota(jnp.int32, (tq, nkeys), 1)
    dist = row + blk - col
    band = jnp.where(dist >= 0, jnp.where(dist <= blk, 0.0, -jnp.inf), -jnp.inf)
    first = jnp.where(col >= blk, 0.0, -jnp.inf)
    bias = band + jnp.where(i > 0, 0.0, first)
    lane = lax.broadcasted_iota(jnp.int32, (tq, LANES), 1)
    nt = (((1,), (1,)), ((), ()))

    def scores(hh):
        cols = slice(hh * HEAD_DIM, (hh + 1) * HEAD_DIM)
        keys = jnp.concatenate([kp_ref[:, cols], kc_ref[:, cols]], axis=0)
        return lax.dot_general(q_ref[:, cols], keys, nt, preferred_element_type=F32) * scale + bias

    lse_tile = jnp.zeros((tq, LANES), F32)
    s_next = scores(0)
    for hh in range(A_HEADS_PER_GROUP):
        cols = slice(hh * HEAD_DIM, (hh + 1) * HEAD_DIM)
        s = s_next
        if hh + 1 < A_HEADS_PER_GROUP:
            s_next = scores(hh + 1)
        m = jnp.max(s, axis=1, keepdims=True)
        e = jnp.exp(s - m)
        den = jnp.sum(e, axis=1, keepdims=True)
        values = jnp.concatenate([vp_ref[:, cols], vc_ref[:, cols]], axis=0)
        acc = jnp.dot(e.astype(BF16), values, preferred_element_type=F32)
        o_ref[:, cols] = acc / den
        lse_tile = jnp.where(lane == hh, m + jnp.log(den), lse_tile)
    lse_ref[...] = lse_tile


def _dilated(qkv, window):
    b, r, m, _ = qkv.shape
    assert window // r == LANES and m % LANES == 0
    tq = _pick(m, 512)
    nsub = tq // LANES

    def cur(tile):
        return pl.BlockSpec((None, None, tq, COL_TILE), lambda bb, p, i: (bb, p, i, tile))

    def prev(tile):
        return pl.BlockSpec((None, None, LANES, COL_TILE),
                            lambda bb, p, i: (bb, p, jnp.maximum(i * nsub - 1, 0), tile))

    return pl.pallas_call(
        functools.partial(_dilated_kernel, tq=tq),
        grid=(b, r, m // tq),
        in_specs=[cur(0), cur(1), prev(1), cur(2), prev(2)],
        out_specs=[
            pl.BlockSpec((None, None, tq, A_GROUP_WIDTH), lambda bb, p, i: (bb, p, i, 0)),
            pl.BlockSpec((None, None, tq, LANES), lambda bb, p, i: (bb, p, i, 0)),
        ],
        out_shape=[jax.ShapeDtypeStruct((b, r, m, A_GROUP_WIDTH), F32), jax.ShapeDtypeStruct((b, r, m, LANES), F32)],
        compiler_params=_params("parallel", "parallel", "parallel"),
        name=f"dilated_r{r}",
    )(qkv, qkv, qkv, qkv, qkv)


def _key_to_float(key):
    bits = jnp.where(key >= 0, key, key ^ 0x7FFFFFFF)
    return lax.bitcast_convert_type(bits, F32)


def _float_to_key(x):
    bits = lax.bitcast_convert_type(x, jnp.int32)
    return jnp.where(bits >= 0, bits, bits ^ 0x7FFFFFFF)


PACK16 = 16


def _select_kernel(iq_ref, wt_ref, k_ref, bias_ref, qt_ref, sc_ref, hi_ref, lo_ref, gm_ref, lg_ref, *, topk):
    i = pl.program_id(1)
    tq, tk = SEL_TQ, SEL_TK
    nk = sc_ref.shape[0]
    nkb = (i * tq + tq + tk - 1) // tk
    slabs = tk // PACK16

    q_t = iq_ref[...].astype(F32).T
    for h in range(IDX_HEADS):
        qt_ref[:, h * tq:(h + 1) * tq] = q_t[h * IDX_DIM:(h + 1) * IDX_DIM, :].astype(qt_ref.dtype)

    kpos = lax.broadcasted_iota(jnp.int32, (tk, tq), 0)
    qpos = i * tq + lax.broadcasted_iota(jnp.int32, (tk, tq), 1)

    def rows(x):
        return jnp.concatenate([x] * slabs, axis=0)

    def _store_digits(kb, scores):
        key = _float_to_key(scores)
        hi_ref[kb] = (key >> 16).astype(jnp.int16)
        lo_ref[kb] = ((key & 0xFFFF) - 2 ** 15).astype(jnp.int16)

    nfull = (i * tq) // tk

    def logits_into(slot, kb):
        keys = k_ref[pl.ds(pl.multiple_of(kb * tk, tk), tk), :]
        lg_ref[slot] = jnp.dot(keys, qt_ref[...], preferred_element_type=F32)

    def score_block(slot, kb, diagonal):
        acc = jnp.zeros((tk, tq), F32)
        for h in range(IDX_HEADS):
            acc = acc + wt_ref[h:h + 1, :] * jnp.maximum(lg_ref[slot, :, h * tq:(h + 1) * tq], 0.0)
        masked = jnp.where(kpos + kb * tk <= qpos, acc, -jnp.inf) if diagonal else acc
        sc_ref[kb] = masked
        _store_digits(kb, masked)
        gm_ref[...] = jnp.maximum(gm_ref[...], jnp.maximum(masked[:tk // 2], masked[tk // 2:]))

    def score_pair(g, carry, diagonal):
        kb = 2 * g
        logits_into(1, kb + 1)
        score_block(0, kb, diagonal)
        logits_into(0, jnp.minimum(kb + 2, nk - 1))
        score_block(1, kb + 1, diagonal)
        return carry

    gm_ref[...] = jnp.full(gm_ref.shape, -jnp.inf, F32)
    logits_into(0, 0)
    lax.fori_loop(0, nfull // 2, functools.partial(score_pair, diagonal=False), 0)
    lax.fori_loop(nfull // 2, (nkb + 1) // 2, functools.partial(score_pair, diagonal=True), 0)

    def reps(x):
        return jnp.broadcast_to(x, (PACK16, tq))

    bound_lo = reps(_float_to_key(jnp.min(gm_ref[...], axis=0, keepdims=True)) >> 16)
    bound_hi = reps(_float_to_key(jnp.max(gm_ref[...], axis=0, keepdims=True)) >> 16) + 1

    ngr = (nkb + SEL_GROUP - 1) // SEL_GROUP
    nkp = ngr * SEL_GROUP

    def pad_block(kb, carry):
        neg_inf = jnp.full((tk, tq), -jnp.inf, F32)
        sc_ref[kb] = neg_inf
        _store_digits(kb, neg_inf)
        return carry

    lax.fori_loop(nkb, nkp, pad_block, 0)

    def bisect16(ref, need, lo0, hi0, high_digit, untested_steps):
        def count_ge(t):
            t16 = t.astype(jnp.int16)

            def body(g, accs):
                accs = list(accs)
                for j in range(SEL_GROUP):
                    for r in range(slabs):
                        blk = ref[g * SEL_GROUP + j, r * PACK16:(r + 1) * PACK16, :]
                        hit = jnp.where(blk >= t16, jnp.int16(1), jnp.int16(0))
                        accs[r % len(accs)] = accs[r % len(accs)] + hit
                return tuple(accs)

            zero = jnp.zeros((PACK16, tq), jnp.int16)
            accs = lax.fori_loop(0, ngr, body, (zero,) * 4)
            acc = (accs[0] + accs[1]) + (accs[2] + accs[3])
            cnt = jnp.sum(acc.astype(jnp.int32), axis=0, keepdims=True)
            return jnp.broadcast_to(cnt, (PACK16, tq))

        def open_brackets(carry):
            _, lo, hi, _, _ = carry
            return jnp.max(hi - lo) > 1

        def step(carry):
            it, lo, hi, below, above = carry
            mid = (lo + hi) >> 1
            if high_digit:
                log_lo = jnp.log(below.astype(F32))
                frac = (log_lo - math.log(need - 0.5)) / (log_lo - jnp.log(jnp.maximum(above.astype(F32), 0.5)))
                frac = jnp.minimum(jnp.maximum(frac, 0.05), 0.95)
                v_lo, v_hi = _key_to_float((lo << 16) | 0xFFFF), _key_to_float(hi << 16)
                by_value = _float_to_key(v_lo + (v_hi - v_lo) * frac) >> 16
                by_value = jnp.minimum(jnp.maximum(by_value, lo + 1), hi - 1)
                mid = jnp.where(it % 4 == 3, mid, by_value)
            mid = jnp.where(hi - lo > 1, mid, lo)
            cnt = count_ge(mid)
            ge = cnt >= need
            lo, hi = jnp.where(ge, mid, lo), jnp.where(ge, hi, mid)
            below, above = jnp.where(ge, cnt, below), jnp.where(ge, above, cnt)
            if not high_digit:
                hi = jnp.where(cnt == need, mid + 1, hi)
            return it + 1, lo, hi, below, above

        everything = jnp.zeros((PACK16, tq), jnp.int32) + nkp * tk
        carry = (jnp.int32(0), lo0, hi0, everything, jnp.zeros((PACK16, tq), jnp.int32))
        carry = lax.fori_loop(0, untested_steps, lambda _, c: step(c), carry)
        _, lo, _, _, above = lax.while_loop(open_brackets, step, carry)
        return lo, above

    key_hi, above = bisect16(hi_ref, topk, bound_lo, bound_hi, True, SEL_UNTESTED_STEPS[0])
    key_hi16 = key_hi.astype(jnp.int16)

    def low_digits(kb, carry):
        for r in range(slabs):
            sl = slice(r * PACK16, (r + 1) * PACK16)
            lo_ref[kb, sl, :] = jnp.where(hi_ref[kb, sl, :] == key_hi16, lo_ref[kb, sl, :], jnp.int16(-(2 ** 15)))
        return carry

    lax.fori_loop(0, nkp, low_digits, 0)
    digit_lo = jnp.full((PACK16, tq), -(2 ** 15), jnp.int32)
    digit_hi = jnp.full((PACK16, tq), 2 ** 15, jnp.int32)
    key_lo, _ = bisect16(lo_ref, topk - above, digit_lo, digit_hi, False, SEL_UNTESTED_STEPS[1])
    thr_rows = rows(_key_to_float((key_hi << 16) | (key_lo + 2 ** 15)))
    sub = 8

    def rows8(x):
        return jnp.concatenate([x] * (tk // sub), axis=0)

    def key_sum(x):
        parts = [x[r * sub:(r + 1) * sub] for r in range(tk // sub)]
        while len(parts) > 1:
            parts = [a + b for a, b in zip(parts[::2], parts[1::2])]
        return parts[0]

    def per_query(partial):
        return jnp.broadcast_to(jnp.sum(partial, axis=0, keepdims=True), (sub, tq))

    def store_mask(kb, picked, diagonal=True):
        if diagonal:
            picked = jnp.where(kpos + kb * tk <= qpos, picked, NEG)
        bias_ref[kb] = picked.astype(bias_ref.dtype).T

    def write_block(kb, n_ge, diagonal):
        hit = sc_ref[kb] >= thr_rows
        store_mask(kb, jnp.where(hit, 0.0, NEG), diagonal)
        return n_ge + key_sum(jnp.where(hit, 1.0, 0.0))

    n_ge = lax.fori_loop(0, nfull, functools.partial(write_block, diagonal=False), jnp.zeros((sub, tq), F32))
    n_ge = per_query(lax.fori_loop(nfull, nkb, functools.partial(write_block, diagonal=True), n_ge))

    @pl.when(jnp.max(n_ge) > topk)
    def _():
        def count(indicator):
            def body(kb, acc):
                return acc + key_sum(indicator(kb, sc_ref[kb]))
            return per_query(lax.fori_loop(0, nkb, body, jnp.zeros((sub, tq), F32)))

        n_gt = count(lambda kb, x: jnp.where(x > thr_rows, 1.0, 0.0))
        need_eq = topk - n_gt

        def step(_, carry):
            lo, hi = carry
            mid = (lo + hi) >> 1
            mid_rows = rows8(mid)
            tied_upto = count(lambda kb, x: jnp.where(
                x == thr_rows, jnp.where(kpos + kb * tk <= mid_rows, 1.0, 0.0), 0.0))
            ok = tied_upto >= need_eq
            return jnp.where(ok, lo, mid), jnp.where(ok, mid, hi)

        lo0 = jnp.full((sub, tq), -1, jnp.int32)
        hi0 = jnp.zeros((sub, tq), jnp.int32) + (nkb * tk - 1)
        _, last_tied = lax.fori_loop(0, (nk * tk).bit_length(), step, (lo0, hi0))
        last_rows = rows8(last_tied)

        def rewrite_block(kb, carry):
            x = sc_ref[kb]
            tied = jnp.where(kpos + kb * tk <= last_rows, 0.0, NEG)
            store_mask(kb, jnp.where(x > thr_rows, 0.0, jnp.where(x == thr_rows, tied, NEG)))
            return carry

        lax.fori_loop(0, nkb, rewrite_block, 0)

    def fill_block(kb, carry):
        bias_ref[kb] = jnp.full((tq, tk), NEG, bias_ref.dtype)
        return carry

    lax.fori_loop(nkb, nk, fill_block, 0)


def _select(z_iv, iw_t, ik, topk):
    b, s, _ = z_iv.shape
    tq, tk = SEL_TQ, SEL_TK
    nq, nk = s // tq, s // tk
    return pl.pallas_call(
        functools.partial(_select_kernel, topk=topk),
        grid=(b, nq),
        in_specs=[
            pl.BlockSpec((None, tq, IDX_WIDTH), lambda bb, i: (bb, i, COL_IQ)),
            pl.BlockSpec((IDX_HEADS, tq), lambda bb, i: (0, bb * nq + i)),
            pl.BlockSpec((None, s, IDX_DIM), lambda bb, i: (bb, 0, 0)),
        ],
        out_specs=pl.BlockSpec((None, None, nk, tq, tk), lambda bb, i: (bb, i, 0, 0, 0)),
        out_shape=jax.ShapeDtypeStruct((b, nq, nk, tq, tk), BF16),
        scratch_shapes=[
            pltpu.VMEM((IDX_DIM, IDX_HEADS * tq), BF16),
            pltpu.VMEM((nk, tk, tq), F32),
            pltpu.VMEM((nk, tk, tq), jnp.int16),
            pltpu.VMEM((nk, tk, tq), jnp.int16),
            pltpu.VMEM((tk // 2, tq), F32),
            pltpu.VMEM((2, tk, IDX_HEADS * tq), F32),
        ],
        compiler_params=_params("parallel", "parallel"),
        name="dsa_select",
    )(z_iv, iw_t, ik)


def _dsa_kernel(q_ref, k_ref, v_ref, b_ref, o_ref, m_ref, l_ref, acc_ref, *, tq, tk):
    i = pl.program_id(1)
    kb = pl.program_id(2)
    last = (i * tq + tq - 1) // tk
    nt = (((1,), (1,)), ((), ()))

    @pl.when(kb == 0)
    def _():
        m_ref[...] = jnp.full(m_ref.shape, NEG, F32)
        l_ref[...] = jnp.zeros(l_ref.shape, F32)
        acc_ref[...] = jnp.zeros(acc_ref.shape, F32)

    @pl.when(kb <= last)
    def _():
        bias = b_ref[...].reshape(tq, tk).astype(F32)
        rep = tk // LANES

        def scores(h):
            cols = slice(h * HEAD_DIM, (h + 1) * HEAD_DIM)
            return lax.dot_general(q_ref[:, cols], k_ref[:, cols], nt, preferred_element_type=F32) + bias

        s_next = scores(0)
        for h in range(B_HEADS):
            cols = slice(h * HEAD_DIM, (h + 1) * HEAD_DIM)
            s = s_next
            if h + 1 < B_HEADS:
                s_next = scores(h + 1)
            m_prev = m_ref[h]
            m_new = jnp.maximum(m_prev, jnp.max(s, axis=1, keepdims=True))
            alpha = jnp.exp2(m_prev - m_new)
            p = jnp.exp2(s - jnp.concatenate([m_new] * rep, axis=1))
            l_ref[h] = alpha * l_ref[h] + jnp.sum(p, axis=1, keepdims=True)
            acc_ref[:, cols] = alpha * acc_ref[:, cols] + jnp.dot(p.astype(BF16), v_ref[:, cols],
                                                                  preferred_element_type=F32)
            m_ref[h] = m_new

    @pl.when(kb == last)
    def _():
        for h in range(B_HEADS):
            cols = slice(h * HEAD_DIM, (h + 1) * HEAD_DIM)
            o_ref[:, cols] = (acc_ref[:, cols] / l_ref[h]).astype(o_ref.dtype)


def _dsa_attention(z_b, bias5):
    b, s, _ = z_b.shape
    tk = SEL_TK
    tq = _pick(s, 512)
    sub = tq // SEL_TQ

    def last(i):
        return (i * tq + tq - 1) // tk

    return pl.pallas_call(
        functools.partial(_dsa_kernel, tq=tq, tk=tk),
        grid=(b, s // tq, s // tk),
        in_specs=[
            pl.BlockSpec((None, tq, B_WIDTH), lambda bb, i, kb: (bb, i, COL_BQ)),
            pl.BlockSpec((None, tk, B_WIDTH), lambda bb, i, kb: (bb, jnp.minimum(kb, last(i)), COL_BK)),
            pl.BlockSpec((None, tk, B_WIDTH), lambda bb, i, kb: (bb, jnp.minimum(kb, last(i)), COL_BV)),
            pl.BlockSpec((None, sub, None, SEL_TQ, tk), lambda bb, i, kb: (bb, i, jnp.minimum(kb, last(i)), 0, 0)),
        ],
        out_specs=pl.BlockSpec((None, tq, B_WIDTH), lambda bb, i, kb: (bb, i, 0)),
        out_shape=jax.ShapeDtypeStruct((b, s, B_WIDTH), BF16),
        scratch_shapes=[
            pltpu.VMEM((B_HEADS, tq, LANES), F32),
            pltpu.VMEM((B_HEADS, tq, LANES), F32),
            pltpu.VMEM((tq, B_WIDTH), F32),
        ],
        compiler_params=_params("parallel", "parallel", "arbitrary"),
        name="dsa_attention",
    )(z_b, z_b, z_b, bias5)


def _to_token_order(dst_ref, first, src_ref):
    r, per, width = src_ref.shape
    for c in range(width // LANES):
        cols = slice(c * LANES, (c + 1) * LANES)
        if r == 1:
            dst_ref[first + c] = src_ref[0, :, cols]
        else:
            for p in range(r):
                dst_ref[first + c, pl.ds(p, per, stride=r), :] = src_ref[p, :, cols]


def _merge_kernel(o1_ref, o2_ref, o3_ref, l1_ref, l2_ref, l3_ref, ob_ref, h_ref,
                  wga_ref, wgb_ref, bga_ref, bgb_ref, wpa_ref, wpb_ref, out_ref, oa_ref, ot_ref, lt_ref):
    @pl.when(pl.program_id(1) == 0)
    def _():
        nh = A_HEADS_PER_GROUP
        for g, (o_ref, l_ref) in enumerate(((o1_ref, l1_ref), (o2_ref, l2_ref), (o3_ref, l3_ref))):
            _to_token_order(ot_ref, g * nh, o_ref)
            _to_token_order(lt_ref, g, l_ref)
        l1, l2, l3 = lt_ref[0], lt_ref[1], lt_ref[2]
        mx = jnp.maximum(jnp.maximum(l1, l2), l3)
        e1, e2, e3 = jnp.exp(l1 - mx), jnp.exp(l2 - mx), jnp.exp(l3 - mx)
        tot = e1 + e2 + e3
        w1, w2, w3 = e1 / tot, e2 / tot, e3 / tot
        for hh in range(nh):
            oa = (w1[:, hh:hh + 1] * ot_ref[hh] + w2[:, hh:hh + 1] * ot_ref[nh + hh]
                  + w3[:, hh:hh + 1] * ot_ref[2 * nh + hh])
            oa_ref[:, hh * HEAD_DIM:(hh + 1) * HEAD_DIM] = oa.astype(oa_ref.dtype)

    h = h_ref[...]
    ga = jax.nn.sigmoid(jnp.dot(h, wga_ref[...], preferred_element_type=F32) + bga_ref[...])
    gb = jax.nn.sigmoid(jnp.dot(h, wgb_ref[...], preferred_element_type=F32) + bgb_ref[...])
    pa = jnp.dot(oa_ref[...], wpa_ref[...], preferred_element_type=F32)
    pb = jnp.dot(ob_ref[...], wpb_ref[...], preferred_element_type=F32)
    out_ref[...] = (ga * pa + gb * pb).astype(out_ref.dtype)


def _merge(outs, lses, o_b, h, w_gate, b_gate, w_proj_a, w_proj_b, seq):
    n, d = h.shape
    tm = _pick(seq, 512)
    tpb = seq // tm
    tn = _pick(d, COL_TILE)
    nj = d // tn
    row = lambda width: pl.BlockSpec((tm, width), lambda i, j: (i, 0))

    def streams(arr):
        r, width = arr.shape[1], arr.shape[3]
        return pl.BlockSpec((None, r, tm // r, width), lambda i, j: (i // tpb, 0, i % tpb, 0))

    return pl.pallas_call(
        _merge_kernel,
        grid=(n // tm, nj),
        in_specs=[
            *[streams(a) for a in outs], *[streams(a) for a in lses],
            row(B_WIDTH), row(d),
            pl.BlockSpec((d, tn), lambda i, j: (0, j)),
            pl.BlockSpec((d, tn), lambda i, j: (0, nj + j)),
            pl.BlockSpec((1, tn), lambda i, j: (0, j)),
            pl.BlockSpec((1, tn), lambda i, j: (0, nj + j)),
            pl.BlockSpec((A_GROUP_WIDTH, tn), lambda i, j: (0, j)),
            pl.BlockSpec((B_WIDTH, tn), lambda i, j: (0, j)),
        ],
        out_specs=pl.BlockSpec((tm, tn), lambda i, j: (i, j)),
        out_shape=jax.ShapeDtypeStruct((n, d), BF16),
        scratch_shapes=[
            pltpu.VMEM((tm, A_GROUP_WIDTH), BF16),
            pltpu.VMEM((len(outs) * A_HEADS_PER_GROUP, tm, HEAD_DIM), F32),
            pltpu.VMEM((len(lses), tm, LANES), F32),
        ],
        compiler_params=_params("parallel", "arbitrary"),
        name="gated_merge",
    )(*outs, *lses, o_b, h, w_gate, w_gate, b_gate, b_gate, w_proj_a, w_proj_b)


def _outproj_kernel(mg_ref, w_ref, x_ref, g_ref, sc_ref, sh_ref, xo_ref, ho_ref):
    mix = jnp.dot(mg_ref[...], w_ref[...], preferred_element_type=F32)
    x = x_ref[...] + g_ref[...] * mix
    xo_ref[...] = x
    ho_ref[...] = (_rms(x, x.shape[-1]) * (1.0 + sc_ref[...]) + sh_ref[...]).astype(ho_ref.dtype)


def _outproj(merged, w_out, x, mod, seq):
    n, d = x.shape
    tm = _pick(seq, 256)
    tpb = seq // tm
    row = pl.BlockSpec((tm, d), lambda i: (i, 0))
    return pl.pallas_call(
        _outproj_kernel,
        grid=(n // tm,),
        in_specs=[row, pl.BlockSpec((d, d), lambda i: (0, 0)), row,
                  _mod_spec(d, 2, tpb), _mod_spec(d, 4, tpb), _mod_spec(d, 3, tpb)],
        out_specs=[row, row],
        out_shape=[jax.ShapeDtypeStruct((n, d), F32), jax.ShapeDtypeStruct((n, d), BF16)],
        compiler_params=_params("parallel"),
        name="outproj",
    )(merged, w_out, x, mod, mod, mod)


def _ffn_kernel(h_ref, wu_ref, wd_ref, x_ref, g_ref, o_ref, acc_ref):
    c = pl.program_id(1)

    @pl.when(c == 0)
    def _():
        acc_ref[...] = jnp.zeros(acc_ref.shape, F32)

    u = jnp.maximum(jnp.dot(h_ref[...], wu_ref[...], preferred_element_type=F32), 0.0)
    acc_ref[...] += jnp.dot((u * u).astype(BF16), wd_ref[...], preferred_element_type=F32)

    @pl.when(c == pl.num_programs(1) - 1)
    def _():
        o_ref[...] = x_ref[...] + g_ref[...] * acc_ref[...]


def _ffn(h2, w_up, w_down, x, mod, seq):
    n, d = x.shape
    hidden = w_up.shape[1]
    tm = _pick(seq, 512)
    tc = _pick(hidden, 512)
    tpb = seq // tm
    row = pl.BlockSpec((tm, d), lambda i, c: (i, 0))
    return pl.pallas_call(
        _ffn_kernel,
        grid=(n // tm, hidden // tc),
        in_specs=[row, pl.BlockSpec((d, tc), lambda i, c: (0, c)), pl.BlockSpec((tc, d), lambda i, c: (c, 0)),
                  row, _mod_spec(d, 5, tpb)],
        out_specs=row,
        out_shape=jax.ShapeDtypeStruct((n, d), F32),
        scratch_shapes=[pltpu.VMEM((tm, d), F32)],
        compiler_params=_params("parallel", "arbitrary"),
        name="ffn",
    )(h2, w_up, w_down, x, mod)


def _pack_in_weights(w_in, a_q_gain, a_k_gain, b_q_gain, b_k_gain, idx_k_gain):
    d = w_in.shape[0]
    sizes = (A_WIDTH, A_WIDTH, A_WIDTH, B_WIDTH, B_WIDTH, B_WIDTH, IDX_WIDTH, IDX_DIM, IDX_HEADS)
    parts, off = [], 0
    for sz in sizes:
        parts.append(w_in[:, off:off + sz])
        off += sz
    aq, ak, av, bq, bk, bv, iq, ik, iw = parts
    w_b = jnp.concatenate([bq, bk, iq, bv], axis=1).astype(BF16)
    w_idx = jnp.concatenate([ik, iw, jnp.zeros((d, LANES - IDX_DIM - IDX_HEADS), w_in.dtype)], axis=1).astype(BF16)
    ones = lambda width: jnp.ones((width,), F32)
    b_gain_cols = jnp.concatenate([jnp.tile(b_q_gain * DSA_Q_SCALE, B_HEADS), jnp.tile(b_k_gain, B_HEADS),
                                   ones(IDX_WIDTH + B_WIDTH)]).reshape(1, -1)
    idx_gain_row = jnp.concatenate([idx_k_gain, ones(LANES - IDX_DIM)]).reshape(1, LANES)
    w_groups = []
    for g in range(len(A_GROUPS)):
        sl = slice(g * A_GROUP_WIDTH, (g + 1) * A_GROUP_WIDTH)
        w_groups.append(jnp.concatenate([aq[:, sl], ak[:, sl], av[:, sl]], axis=1).astype(BF16))
    a_gain_cols = jnp.concatenate([
        jnp.tile(a_q_gain, A_HEADS_PER_GROUP), jnp.tile(a_k_gain, A_HEADS_PER_GROUP), ones(A_GROUP_WIDTH),
    ]).reshape(1, A_PACK_WIDTH)
    return w_b, b_gain_cols, w_idx, idx_gain_row, w_groups, a_gain_cols


def kernel(x, c, positions, w_ada, b_ada, w_in, a_q_gain, a_k_gain, b_q_gain, b_k_gain, idx_k_gain,
           w_gate, b_gate, w_proj_a, w_proj_b, w_out, w_up, w_down):
    b, s, d = x.shape
    depth = w_ada.shape[0]
    n = b * s
    topk = min(IDX_TOPK, s // 4)
    assert s % (SEL_TK * SEL_GROUP) == 0 and SEL_TK // 2 >= topk and d % COL_TILE == 0 and s // LANES < 2 ** 15

    tabs = _rope_tables(positions)
    c128, s128, c64, s64 = tabs
    mods = _ada(c, w_ada, b_ada)
    xf = x.reshape(n, d)

    for l in range(depth):
        mod = mods[l]
        w_b, b_gain_cols, w_idx, idx_gain_row, w_groups, a_gain_cols = _pack_in_weights(
            w_in[l], a_q_gain[l], a_k_gain[l], b_q_gain[l], b_k_gain[l], idx_k_gain[l])

        h = _normmod(xf, mod, s, 1, 0)
        z_b = _bproj(h, w_b, b_gain_cols, tabs, B_EPILOGUES, "proj_b").reshape(b, s, -1)
        ik, iw = _idxproj(h, w_idx, idx_gain_row, c64, s64)

        a_outs, a_lses = [], []
        for g, (window, dilation) in enumerate(A_GROUPS):
            qkv = _aproj(h, w_groups[g], a_gain_cols, tabs, b, s, dilation)
            o, lse = _dilated(qkv, window)
            a_outs.append(o)
            a_lses.append(lse)

        bias5 = _select(z_b, iw, ik.reshape(b, s, IDX_DIM), topk)
        o_b = _dsa_attention(z_b, bias5).reshape(n, B_WIDTH)

        merged = _merge(a_outs, a_lses, o_b, h, w_gate[l].astype(BF16), b_gate[l].reshape(1, 2 * d),
                        w_proj_a[l].astype(BF16), w_proj_b[l].astype(BF16), s)
        xf, h2 = _outproj(merged, w_out[l].astype(BF16), xf, mod, s)
        xf = _ffn(h2, w_up[l].astype(BF16), w_down[l].astype(BF16), xf, mod, s)

    return xf.reshape(b, s, d)
```

```python
import functools
import math

import jax
import jax.numpy as jnp
from jax import lax
from jax.experimental import pallas as pl
from jax.experimental.pallas import tpu as pltpu

F32 = jnp.float32
BF16 = jnp.bfloat16

HEAD_DIM = 128
LANES = 128
A_GROUPS = ((128, 1), (512, 4), (2048, 16))
A_HEADS_PER_GROUP = 4
A_GROUP_WIDTH = A_HEADS_PER_GROUP * HEAD_DIM
A_WIDTH = len(A_GROUPS) * A_GROUP_WIDTH
B_HEADS = 8
B_WIDTH = B_HEADS * HEAD_DIM
IDX_HEADS = 16
IDX_DIM = 64
IDX_WIDTH = IDX_HEADS * IDX_DIM
IDX_TOPK = 256
ROPE_THETA = 10000.0
EPS = 1e-6
N_MOD = 6
NEG = -1e30
LOG2_E = 1.4426950408889634
DSA_Q_SCALE = HEAD_DIM ** -0.5 * LOG2_E

COL_TILE = 512
MXU_COLS = 256
EPI_ROPE64, EPI_QK, EPI_PLAIN = 0, 1, 2
B_EPILOGUES = ((EPI_QK,) * (2 * B_WIDTH // MXU_COLS) + (EPI_ROPE64,) * (IDX_WIDTH // MXU_COLS)
               + (EPI_PLAIN,) * (B_WIDTH // MXU_COLS))
COL_BQ, COL_BK, COL_IQ, COL_BV = 0, 1, 2, 3
assert IDX_WIDTH == B_WIDTH
A_EPILOGUES = (EPI_QK,) * (2 * A_GROUP_WIDTH // MXU_COLS) + (EPI_PLAIN,) * (A_GROUP_WIDTH // MXU_COLS)
A_PACK_WIDTH = 3 * A_GROUP_WIDTH

SEL_TQ = 128
SEL_TK = 512
SEL_GROUP = 4
SEL_UNTESTED_STEPS = (10, 9)
VMEM_LIMIT = 52 * 1024 * 1024


def _params(*sem):
    return pltpu.CompilerParams(dimension_semantics=sem, vmem_limit_bytes=VMEM_LIMIT)


def _pick(n, pref):
    t = pref
    while n % t:
        t //= 2
    return t


def _rms(x, width):
    return x * lax.rsqrt(jnp.sum(x * x, axis=-1, keepdims=True) * (1.0 / width) + EPS)


def _swap_half64(y):
    lane = lax.broadcasted_iota(jnp.int32, y.shape, 1)
    return jnp.where((lane & 63) < 32, pltpu.roll(y, 96, 1), pltpu.roll(y, 32, 1))


def _rope_tables_kernel(pos_ref, f128_ref, g128_ref, f64_ref, g64_ref, c128_ref, s128_ref, c64_ref, s64_ref):
    pos = pos_ref[...]
    a = pos * f128_ref[...]
    c128_ref[...] = jnp.cos(a)
    s128_ref[...] = jnp.sin(a) * g128_ref[...]
    a = pos * f64_ref[...]
    c64_ref[...] = jnp.cos(a)
    s64_ref[...] = jnp.sin(a) * g64_ref[...]


def _rope_tables(positions):
    n = positions.size
    pos = positions.reshape(n, 1).astype(F32)

    def freq(d):
        half = d // 2
        inv = jnp.power(ROPE_THETA, -jnp.arange(half, dtype=F32) * 2.0 / d)
        f = jnp.tile(jnp.concatenate([inv, inv]), LANES // d)
        g = jnp.tile(jnp.concatenate([-jnp.ones((half,), F32), jnp.ones((half,), F32)]), LANES // d)
        return f.reshape(1, LANES), g.reshape(1, LANES)

    f128, g128 = freq(HEAD_DIM)
    f64, g64 = freq(IDX_DIM)
    tm = _pick(n, 1024)
    row = pl.BlockSpec((1, LANES), lambda i: (0, 0))
    tab = pl.BlockSpec((tm, LANES), lambda i: (i, 0))
    return pl.pallas_call(
        _rope_tables_kernel,
        grid=(n // tm,),
        in_specs=[pl.BlockSpec((tm, 1), lambda i: (i, 0)), row, row, row, row],
        out_specs=[tab, tab, tab, tab],
        out_shape=[jax.ShapeDtypeStruct((n, LANES), F32)] * 4,
        compiler_params=_params("parallel"),
        name="rope_tables",
    )(pos, f128, g128, f64, g64)


def _ada_kernel(c_ref, w_ref, b_ref, o_ref):
    c = c_ref[...]
    act = (c * jax.nn.sigmoid(c)).astype(BF16)
    o_ref[...] = jnp.dot(act, w_ref[...].astype(BF16), preferred_element_type=F32) + b_ref[...]


def _ada(c, w_ada, b_ada):
    depth, d, n6 = w_ada.shape
    b = c.shape[0]
    rows = 8
    c_pad = jnp.zeros((rows, d), F32).at[:b].set(c)
    tn = _pick(n6, 1024)
    out = pl.pallas_call(
        _ada_kernel,
        grid=(depth, n6 // tn),
        in_specs=[
            pl.BlockSpec((rows, d), lambda l, j: (0, 0)),
            pl.BlockSpec((None, d, tn), lambda l, j: (l, 0, j)),
            pl.BlockSpec((None, 1, tn), lambda l, j: (l, 0, j)),
        ],
        out_specs=pl.BlockSpec((None, rows, tn), lambda l, j: (l, 0, j)),
        out_shape=jax.ShapeDtypeStruct((depth, rows, n6), F32),
        compiler_params=_params("parallel", "parallel"),
        name="adaln",
    )(c_pad, w_ada, b_ada.reshape(depth, 1, n6))
    return out[:, :b].reshape(depth, b, N_MOD, 1, d)


def _mod_spec(d, which, tiles_per_batch):
    return pl.BlockSpec((None, None, 1, d), lambda i, *_: (i // tiles_per_batch, which, 0, 0))


def _normmod_kernel(x_ref, sc_ref, sh_ref, o_ref):
    x = x_ref[...]
    y = _rms(x, x.shape[-1])
    o_ref[...] = (y * (1.0 + sc_ref[...]) + sh_ref[...]).astype(o_ref.dtype)


def _normmod(x, mod, seq, which_scale, which_shift):
    n, d = x.shape
    tm = _pick(seq, 512)
    tpb = seq // tm
    return pl.pallas_call(
        _normmod_kernel,
        grid=(n // tm,),
        in_specs=[pl.BlockSpec((tm, d), lambda i: (i, 0)), _mod_spec(d, which_scale, tpb), _mod_spec(d, which_shift, tpb)],
        out_specs=pl.BlockSpec((tm, d), lambda i: (i, 0)),
        out_shape=jax.ShapeDtypeStruct((n, d), BF16),
        compiler_params=_params("parallel"),
        name="normmod",
    )(x, mod, mod)


def _proj_kernel(h_ref, w_ref, g_ref, c128_ref, s128_ref, c64_ref, s64_ref, o_ref, z_ref, *, epilogues, streams):
    h = h_ref[...]
    per = h.shape[0] // streams
    for t, kind in enumerate(epilogues):
        z = jnp.dot(h, w_ref[:, t * MXU_COLS:(t + 1) * MXU_COLS], preferred_element_type=F32)
        for c in range(MXU_COLS // LANES):
            slab = t * (MXU_COLS // LANES) + c
            cols = slice(slab * LANES, (slab + 1) * LANES)
            y = z[:, c * LANES:(c + 1) * LANES]
            if kind == EPI_QK:
                y = _rms(y, HEAD_DIM) * g_ref[:, cols]
                y = y * c128_ref[...] + pltpu.roll(y, HEAD_DIM // 2, 1) * s128_ref[...]
            elif kind == EPI_ROPE64:
                y = y * c64_ref[...] + _swap_half64(y) * s64_ref[...]
            if streams == 1:
                o_ref[..., cols] = y.astype(o_ref.dtype).reshape(o_ref.shape[:-1] + (LANES,))
            else:
                z_ref[slab] = y
                for p in range(streams):
                    o_ref[p, :, cols] = z_ref[slab, pl.ds(p, per, stride=streams), :].astype(o_ref.dtype)


def _proj_call(h, w, gain_cols, tabs, epilogues, out_spec, out_shape, tm, streams, name):
    d, width = w.shape
    assert width == len(epilogues) * MXU_COLS
    tab = pl.BlockSpec((tm, LANES), lambda i: (i, 0))
    slabs = width // LANES if streams > 1 else 1
    return pl.pallas_call(
        functools.partial(_proj_kernel, epilogues=epilogues, streams=streams),
        grid=(h.shape[0] // tm,),
        in_specs=[
            pl.BlockSpec((tm, d), lambda i: (i, 0)),
            pl.BlockSpec((d, width), lambda i: (0, 0), pipeline_mode=pl.Buffered(1)),
            pl.BlockSpec((1, width), lambda i: (0, 0)),
            tab, tab, tab, tab,
        ],
        out_specs=out_spec,
        out_shape=out_shape,
        scratch_shapes=[pltpu.VMEM((slabs, tm, LANES), F32)],
        compiler_params=_params("parallel"),
        name=name,
    )(h, w, gain_cols, *tabs)


def _bproj(h, w, gain_cols, tabs, epilogues, name):
    n = h.shape[0]
    tm = _pick(n, 512)
    width = w.shape[1]
    return _proj_call(h, w, gain_cols, tabs, epilogues, pl.BlockSpec((tm, width), lambda i: (i, 0)),
                      jax.ShapeDtypeStruct((n, width), BF16), tm, 1, name)


def _aproj(h, w_group, gain_cols, tabs, batch, seq, dilation):
    r = dilation
    tm = _pick(seq, 512)
    tpb = seq // tm
    assert tm % (r * 16) == 0
    return _proj_call(h, w_group, gain_cols, tabs, A_EPILOGUES,
                      pl.BlockSpec((None, r, tm // r, A_PACK_WIDTH), lambda i: (i // tpb, 0, i % tpb, 0)),
                      jax.ShapeDtypeStruct((batch, r, seq // r, A_PACK_WIDTH), BF16), tm, r, f"aproj_r{r}")


def _idxproj_kernel(h_ref, w_ref, g_ref, c64_ref, s64_ref, ik_ref, iw_ref):
    z = jnp.dot(h_ref[...], w_ref[...], preferred_element_type=F32)
    lane = lax.broadcasted_iota(jnp.int32, z.shape, 1)
    is_k = lane < IDX_DIM
    zk = jnp.where(is_k, z, 0.0)
    y = _rms(zk, IDX_DIM) * g_ref[...]
    y = y * c64_ref[...] + _swap_half64(y) * s64_ref[...]
    ik_ref[...] = y[:, :IDX_DIM].astype(ik_ref.dtype)
    iw_ref[...] = z.T[IDX_DIM:IDX_DIM + IDX_HEADS, :] * (IDX_HEADS ** -0.5 * IDX_DIM ** -0.5)


def _idxproj(h, w_idx, gain_row, c64, s64):
    n, d = h.shape
    tm = _pick(n, 512)
    tab = pl.BlockSpec((tm, LANES), lambda i: (i, 0))
    return pl.pallas_call(
        _idxproj_kernel,
        grid=(n // tm,),
        in_specs=[
            pl.BlockSpec((tm, d), lambda i: (i, 0)),
            pl.BlockSpec((d, LANES), lambda i: (0, 0)),
            pl.BlockSpec((1, LANES), lambda i: (0, 0)),
            tab, tab,
        ],
        out_specs=[pl.BlockSpec((tm, IDX_DIM), lambda i: (i, 0)), pl.BlockSpec((IDX_HEADS, tm), lambda i: (0, i))],
        out_shape=[jax.ShapeDtypeStruct((n, IDX_DIM), BF16), jax.ShapeDtypeStruct((IDX_HEADS, n), F32)],
        compiler_params=_params("parallel"),
        name="idxproj",
    )(h, w_idx, gain_row, c64, s64)


def _dilated_kernel(q_ref, kc_ref, kp_ref, vc_ref, vp_ref, o_ref, lse_ref, *, tq):
    i = pl.program_id(2)
    blk = LANES
    scale = HEAD_DIM ** -0.5
    nkeys = blk + tq
    row = lax.broadcasted_iota(jnp.int32, (tq, nkeys), 0)
    col = lax.broadcasted_iota(jnp.int32, (tq, nkeys), 1)
    dist = row + blk - col
    band = jnp.where(dist >= 0, jnp.where(dist <= blk, 0.0, -jnp.inf), -jnp.inf)
    first = jnp.where(col >= blk, 0.0, -jnp.inf)
    bias = band + jnp.where(i > 0, 0.0, first)
    lane = lax.broadcasted_iota(jnp.int32, (tq, LANES), 1)
    nt = (((1,), (1,)), ((), ()))

    def scores(hh):
        cols = slice(hh * HEAD_DIM, (hh + 1) * HEAD_DIM)
        keys = jnp.concatenate([kp_ref[:, cols], kc_ref[:, cols]], axis=0)
        return lax.dot_general(q_ref[:, cols], keys, nt, preferred_element_type=F32) * scale + bias

    lse_tile = jnp.zeros((tq, LANES), F32)
    s_next = scores(0)
    for hh in range(A_HEADS_PER_GROUP):
        cols = slice(hh * HEAD_DIM, (hh + 1) * HEAD_DIM)
        s = s_next
        if hh + 1 < A_HEADS_PER_GROUP:
            s_next = scores(hh + 1)
        m = jnp.max(s, axis=1, keepdims=True)
        e = jnp.exp(s - m)
        den = jnp.sum(e, axis=1, keepdims=True)
        values = jnp.concatenate([vp_ref[:, cols], vc_ref[:, cols]], axis=0)
        acc = jnp.dot(e.astype(BF16), values, preferred_element_type=F32)
        o_ref[:, cols] = acc / den
        lse_tile = jnp.where(lane == hh, m + jnp.log(den), lse_tile)
    lse_ref[...] = lse_tile


def _dilated(qkv, window):
    b, r, m, _ = qkv.shape
    assert window // r == LANES and m % LANES == 0
    tq = _pick(m, 512)
    nsub = tq // LANES

    def cur(tile):
        return pl.BlockSpec((None, None, tq, COL_TILE), lambda bb, p, i: (bb, p, i, tile))

    def prev(tile):
        return pl.BlockSpec((None, None, LANES, COL_TILE),
                            lambda bb, p, i: (bb, p, jnp.maximum(i * nsub - 1, 0), tile))

    return pl.pallas_call(
        functools.partial(_dilated_kernel, tq=tq),
        grid=(b, r, m // tq),
        in_specs=[cur(0), cur(1), prev(1), cur(2), prev(2)],
        out_specs=[
            pl.BlockSpec((None, None, tq, A_GROUP_WIDTH), lambda bb, p, i: (bb, p, i, 0)),
            pl.BlockSpec((None, None, tq, LANES), lambda bb, p, i: (bb, p, i, 0)),
        ],
        out_shape=[jax.ShapeDtypeStruct((b, r, m, A_GROUP_WIDTH), F32), jax.ShapeDtypeStruct((b, r, m, LANES), F32)],
        compiler_params=_params("parallel", "parallel", "parallel"),
        name=f"dilated_r{r}",
    )(qkv, qkv, qkv, qkv, qkv)


def _key_to_float(key):
    bits = jnp.where(key >= 0, key, key ^ 0x7FFFFFFF)
    return lax.bitcast_convert_type(bits, F32)


def _float_to_key(x):
    bits = lax.bitcast_convert_type(x, jnp.int32)
    return jnp.where(bits >= 0, bits, bits ^ 0x7FFFFFFF)


PACK16 = 16


def _select_kernel(iq_ref, wt_ref, k_ref, bias_ref, qt_ref, sc_ref, hi_ref, lo_ref, gm_ref, lg_ref, *, topk):
    i = pl.program_id(1)
    tq, tk = SEL_TQ, SEL_TK
    nk = sc_ref.shape[0]
    nkb = (i * tq + tq + tk - 1) // tk
    slabs = tk // PACK16

    q_t = iq_ref[...].astype(F32).T
    for h in range(IDX_HEADS):
        qt_ref[:, h * tq:(h + 1) * tq] = q_t[h * IDX_DIM:(h + 1) * IDX_DIM, :].astype(qt_ref.dtype)

    kpos = lax.broadcasted_iota(jnp.int32, (tk, tq), 0)
    qpos = i * tq + lax.broadcasted_iota(jnp.int32, (tk, tq), 1)

    def rows(x):
        return jnp.concatenate([x] * slabs, axis=0)

    def _store_digits(kb, scores):
        key = _float_to_key(scores)
        hi_ref[kb] = (key >> 16).astype(jnp.int16)
        lo_ref[kb] = ((key & 0xFFFF) - 2 ** 15).astype(jnp.int16)

    nfull = (i * tq) // tk

    def logits_into(slot, kb):
        keys = k_ref[pl.ds(pl.multiple_of(kb * tk, tk), tk), :]
        lg_ref[slot] = jnp.dot(keys, qt_ref[...], preferred_element_type=F32)

    def score_block(slot, kb, diagonal):
        acc = jnp.zeros((tk, tq), F32)
        for h in range(IDX_HEADS):
            acc = acc + wt_ref[h:h + 1, :] * jnp.maximum(lg_ref[slot, :, h * tq:(h + 1) * tq], 0.0)
        masked = jnp.where(kpos + kb * tk <= qpos, acc, -jnp.inf) if diagonal else acc
        sc_ref[kb] = masked
        _store_digits(kb, masked)
        gm_ref[...] = jnp.maximum(gm_ref[...], jnp.maximum(masked[:tk // 2], masked[tk // 2:]))

    def score_pair(g, carry, diagonal):
        kb = 2 * g
        logits_into(1, kb + 1)
        score_block(0, kb, diagonal)
        logits_into(0, jnp.minimum(kb + 2, nk - 1))
        score_block(1, kb + 1, diagonal)
        return carry

    gm_ref[...] = jnp.full(gm_ref.shape, -jnp.inf, F32)
    logits_into(0, 0)
    lax.fori_loop(0, nfull // 2, functools.partial(score_pair, diagonal=False), 0)
    lax.fori_loop(nfull // 2, (nkb + 1) // 2, functools.partial(score_pair, diagonal=True), 0)

    def reps(x):
        return jnp.broadcast_to(x, (PACK16, tq))

    bound_lo = reps(_float_to_key(jnp.min(gm_ref[...], axis=0, keepdims=True)) >> 16)
    bound_hi = reps(_float_to_key(jnp.max(gm_ref[...], axis=0, keepdims=True)) >> 16) + 1

    ngr = (nkb + SEL_GROUP - 1) // SEL_GROUP
    nkp = ngr * SEL_GROUP

    def pad_block(kb, carry):
        neg_inf = jnp.full((tk, tq), -jnp.inf, F32)
        sc_ref[kb] = neg_inf
        _store_digits(kb, neg_inf)
        return carry

    lax.fori_loop(nkb, nkp, pad_block, 0)

    def bisect16(ref, need, lo0, hi0, high_digit, untested_steps):
        def count_ge(t):
            t16 = t.astype(jnp.int16)

            def body(g, accs):
                accs = list(accs)
                for j in range(SEL_GROUP):
                    for r in range(slabs):
                        blk = ref[g * SEL_GROUP + j, r * PACK16:(r + 1) * PACK16, :]
                        hit = jnp.where(blk >= t16, jnp.int16(1), jnp.int16(0))
                        accs[r % len(accs)] = accs[r % len(accs)] + hit
                return tuple(accs)

            zero = jnp.zeros((PACK16, tq), jnp.int16)
            accs = lax.fori_loop(0, ngr, body, (zero,) * 4)
            acc = (accs[0] + accs[1]) + (accs[2] + accs[3])
            cnt = jnp.sum(acc.astype(jnp.int32), axis=0, keepdims=True)
            return jnp.broadcast_to(cnt, (PACK16, tq))

        def open_brackets(carry):
            _, lo, hi, _, _ = carry
            return jnp.max(hi - lo) > 1

        def step(carry):
            it, lo, hi, below, above = carry
            mid = (lo + hi) >> 1
            if high_digit:
                log_lo = jnp.log(below.astype(F32))
                frac = (log_lo - math.log(need - 0.5)) / (log_lo - jnp.log(jnp.maximum(above.astype(F32), 0.5)))
                frac = jnp.minimum(jnp.maximum(frac, 0.05), 0.95)
                v_lo, v_hi = _key_to_float((lo << 16) | 0xFFFF), _key_to_float(hi << 16)
                by_value = _float_to_key(v_lo + (v_hi - v_lo) * frac) >> 16
                by_value = jnp.minimum(jnp.maximum(by_value, lo + 1), hi - 1)
                mid = jnp.where(it % 4 == 3, mid, by_value)
            mid = jnp.where(hi - lo > 1, mid, lo)
            cnt = count_ge(mid)
            ge = cnt >= need
            lo, hi = jnp.where(ge, mid, lo), jnp.where(ge, hi, mid)
            below, above = jnp.where(ge, cnt, below), jnp.where(ge, above, cnt)
            if not high_digit:
                hi = jnp.where(cnt == need, mid + 1, hi)
            return it + 1, lo, hi, below, above

        everything = jnp.zeros((PACK16, tq), jnp.int32) + nkp * tk
        carry = (jnp.int32(0), lo0, hi0, everything, jnp.zeros((PACK16, tq), jnp.int32))
        carry = lax.fori_loop(0, untested_steps, lambda _, c: step(c), carry)
        _, lo, _, _, above = lax.while_loop(open_brackets, step, carry)
        return lo, above

    key_hi, above = bisect16(hi_ref, topk, bound_lo, bound_hi, True, SEL_UNTESTED_STEPS[0])
    key_hi16 = key_hi.astype(jnp.int16)

    def low_digits(kb, carry):
        for r in range(slabs):
            sl = slice(r * PACK16, (r + 1) * PACK16)
            lo_ref[kb, sl, :] = jnp.where(hi_ref[kb, sl, :] == key_hi16, lo_ref[kb, sl, :], jnp.int16(-(2 ** 15)))
        return carry

    lax.fori_loop(0, nkp, low_digits, 0)
    digit_lo = jnp.full((PACK16, tq), -(2 ** 15), jnp.int32)
    digit_hi = jnp.full((PACK16, tq), 2 ** 15, jnp.int32)
    key_lo, _ = bisect16(lo_ref, topk - above, digit_lo, digit_hi, False, SEL_UNTESTED_STEPS[1])
    thr_rows = rows(_key_to_float((key_hi << 16) | (key_lo + 2 ** 15)))
    sub = 8

    def rows8(x):
        return jnp.concatenate([x] * (tk // sub), axis=0)

    def key_sum(x):
        parts = [x[r * sub:(r + 1) * sub] for r in range(tk // sub)]
        while len(parts) > 1:
            parts = [a + b for a, b in zip(parts[::2], parts[1::2])]
        return parts[0]

    def per_query(partial):
        return jnp.broadcast_to(jnp.sum(partial, axis=0, keepdims=True), (sub, tq))

    def store_mask(kb, picked, diagonal=True):
        if diagonal:
            picked = jnp.where(kpos + kb * tk <= qpos, picked, NEG)
        bias_ref[kb] = picked.astype(bias_ref.dtype).T

    def write_block(kb, n_ge, diagonal):
        hit = sc_ref[kb] >= thr_rows
        store_mask(kb, jnp.where(hit, 0.0, NEG), diagonal)
        return n_ge + key_sum(jnp.where(hit, 1.0, 0.0))

    n_ge = lax.fori_loop(0, nfull, functools.partial(write_block, diagonal=False), jnp.zeros((sub, tq), F32))
    n_ge = per_query(lax.fori_loop(nfull, nkb, functools.partial(write_block, diagonal=True), n_ge))

    @pl.when(jnp.max(n_ge) > topk)
    def _():
        def count(indicator):
            def body(kb, acc):
                return acc + key_sum(indicator(kb, sc_ref[kb]))
            return per_query(lax.fori_loop(0, nkb, body, jnp.zeros((sub, tq), F32)))

        n_gt = count(lambda kb, x: jnp.where(x > thr_rows, 1.0, 0.0))
        need_eq = topk - n_gt

        def step(_, carry):
            lo, hi = carry
            mid = (lo + hi) >> 1
            mid_rows = rows8(mid)
            tied_upto = count(lambda kb, x: jnp.where(
                x == thr_rows, jnp.where(kpos + kb * tk <= mid_rows, 1.0, 0.0), 0.0))
            ok = tied_upto >= need_eq
            return jnp.where(ok, lo, mid), jnp.where(ok, mid, hi)

        lo0 = jnp.full((sub, tq), -1, jnp.int32)
        hi0 = jnp.zeros((sub, tq), jnp.int32) + (nkb * tk - 1)
        _, last_tied = lax.fori_loop(0, (nk * tk).bit_length(), step, (lo0, hi0))
        last_rows = rows8(last_tied)

        def rewrite_block(kb, carry):
            x = sc_ref[kb]
            tied = jnp.where(kpos + kb * tk <= last_rows, 0.0, NEG)
            store_mask(kb, jnp.where(x > thr_rows, 0.0, jnp.where(x == thr_rows, tied, NEG)))
            return carry

        lax.fori_loop(0, nkb, rewrite_block, 0)

    def fill_block(kb, carry):
        bias_ref[kb] = jnp.full((tq, tk), NEG, bias_ref.dtype)
        return carry

    lax.fori_loop(nkb, nk, fill_block, 0)


def _select(z_iv, iw_t, ik, topk):
    b, s, _ = z_iv.shape
    tq, tk = SEL_TQ, SEL_TK
    nq, nk = s // tq, s // tk
    return pl.pallas_call(
        functools.partial(_select_kernel, topk=topk),
        grid=(b, nq),
        in_specs=[
            pl.BlockSpec((None, tq, IDX_WIDTH), lambda bb, i: (bb, i, COL_IQ)),
            pl.BlockSpec((IDX_HEADS, tq), lambda bb, i: (0, bb * nq + i)),
            pl.BlockSpec((None, s, IDX_DIM), lambda bb, i: (bb, 0, 0)),
        ],
        out_specs=pl.BlockSpec((None, None, nk, tq, tk), lambda bb, i: (bb, i, 0, 0, 0)),
        out_shape=jax.ShapeDtypeStruct((b, nq, nk, tq, tk), BF16),
        scratch_shapes=[
            pltpu.VMEM((IDX_DIM, IDX_HEADS * tq), BF16),
            pltpu.VMEM((nk, tk, tq), F32),
            pltpu.VMEM((nk, tk, tq), jnp.int16),
            pltpu.VMEM((nk, tk, tq), jnp.int16),
            pltpu.VMEM((tk // 2, tq), F32),
            pltpu.VMEM((2, tk, IDX_HEADS * tq), F32),
        ],
        compiler_params=_params("parallel", "parallel"),
        name="dsa_select",
    )(z_iv, iw_t, ik)


def _dsa_kernel(q_ref, k_ref, v_ref, b_ref, o_ref, m_ref, l_ref, acc_ref, *, tq, tk):
    i = pl.program_id(1)
    kb = pl.program_id(2)
    last = (i * tq + tq - 1) // tk
    nt = (((1,), (1,)), ((), ()))

    @pl.when(kb == 0)
    def _():
        m_ref[...] = jnp.full(m_ref.shape, NEG, F32)
        l_ref[...] = jnp.zeros(l_ref.shape, F32)
        acc_ref[...] = jnp.zeros(acc_ref.shape, F32)

    @pl.when(kb <= last)
    def _():
        bias = b_ref[...].reshape(tq, tk).astype(F32)
        rep = tk // LANES

        def scores(h):
            cols = slice(h * HEAD_DIM, (h + 1) * HEAD_DIM)
            return lax.dot_general(q_ref[:, cols], k_ref[:, cols], nt, preferred_element_type=F32) + bias

        s_next = scores(0)
        for h in range(B_HEADS):
            cols = slice(h * HEAD_DIM, (h + 1) * HEAD_DIM)
            s = s_next
            if h + 1 < B_HEADS:
                s_next = scores(h + 1)
            m_prev = m_ref[h]
            m_new = jnp.maximum(m_prev, jnp.max(s, axis=1, keepdims=True))
            alpha = jnp.exp2(m_prev - m_new)
            p = jnp.exp2(s - jnp.concatenate([m_new] * rep, axis=1))
            l_ref[h] = alpha * l_ref[h] + jnp.sum(p, axis=1, keepdims=True)
            acc_ref[:, cols] = alpha * acc_ref[:, cols] + jnp.dot(p.astype(BF16), v_ref[:, cols],
                                                                  preferred_element_type=F32)
            m_ref[h] = m_new

    @pl.when(kb == last)
    def _():
        for h in range(B_HEADS):
            cols = slice(h * HEAD_DIM, (h + 1) * HEAD_DIM)
            o_ref[:, cols] = (acc_ref[:, cols] / l_ref[h]).astype(o_ref.dtype)


def _dsa_attention(z_b, bias5):
    b, s, _ = z_b.shape
    tk = SEL_TK
    tq = _pick(s, 512)
    sub = tq // SEL_TQ

    def last(i):
        return (i * tq + tq - 1) // tk

    return pl.pallas_call(
        functools.partial(_dsa_kernel, tq=tq, tk=tk),
        grid=(b, s // tq, s // tk),
        in_specs=[
            pl.BlockSpec((None, tq, B_WIDTH), lambda bb, i, kb: (bb, i, COL_BQ)),
            pl.BlockSpec((None, tk, B_WIDTH), lambda bb, i, kb: (bb, jnp.minimum(kb, last(i)), COL_BK)),
            pl.BlockSpec((None, tk, B_WIDTH), lambda bb, i, kb: (bb, jnp.minimum(kb, last(i)), COL_BV)),
            pl.BlockSpec((None, sub, None, SEL_TQ, tk), lambda bb, i, kb: (bb, i, jnp.minimum(kb, last(i)), 0, 0)),
        ],
        out_specs=pl.BlockSpec((None, tq, B_WIDTH), lambda bb, i, kb: (bb, i, 0)),
        out_shape=jax.ShapeDtypeStruct((b, s, B_WIDTH), BF16),
        scratch_shapes=[
            pltpu.VMEM((B_HEADS, tq, LANES), F32),
            pltpu.VMEM((B_HEADS, tq, LANES), F32),
            pltpu.VMEM((tq, B_WIDTH), F32),
        ],
        compiler_params=_params("parallel", "parallel", "arbitrary"),
        name="dsa_attention",
    )(z_b, z_b, z_b, bias5)


def _to_token_order(dst_ref, first, src_ref):
    r, per, width = src_ref.shape
    for c in range(width // LANES):
        cols = slice(c * LANES, (c + 1) * LANES)
        if r == 1:
            dst_ref[first + c] = src_ref[0, :, cols]
        else:
            for p in range(r):
                dst_ref[first + c, pl.ds(p, per, stride=r), :] = src_ref[p, :, cols]


def _merge_kernel(o1_ref, o2_ref, o3_ref, l1_ref, l2_ref, l3_ref, ob_ref, h_ref,
                  wga_ref, wgb_ref, bga_ref, bgb_ref, wpa_ref, wpb_ref, out_ref, oa_ref, ot_ref, lt_ref):
    @pl.when(pl.program_id(1) == 0)
    def _():
        nh = A_HEADS_PER_GROUP
        for g, (o_ref, l_ref) in enumerate(((o1_ref, l1_ref), (o2_ref, l2_ref), (o3_ref, l3_ref))):
            _to_token_order(ot_ref, g * nh, o_ref)
            _to_token_order(lt_ref, g, l_ref)
        l1, l2, l3 = lt_ref[0], lt_ref[1], lt_ref[2]
        mx = jnp.maximum(jnp.maximum(l1, l2), l3)
        e1, e2, e3 = jnp.exp(l1 - mx), jnp.exp(l2 - mx), jnp.exp(l3 - mx)
        tot = e1 + e2 + e3
        w1, w2, w3 = e1 / tot, e2 / tot, e3 / tot
        for hh in range(nh):
            oa = (w1[:, hh:hh + 1] * ot_ref[hh] + w2[:, hh:hh + 1] * ot_ref[nh + hh]
                  + w3[:, hh:hh + 1] * ot_ref[2 * nh + hh])
            oa_ref[:, hh * HEAD_DIM:(hh + 1) * HEAD_DIM] = oa.astype(oa_ref.dtype)

    h = h_ref[...]
    ga = jax.nn.sigmoid(jnp.dot(h, wga_ref[...], preferred_element_type=F32) + bga_ref[...])
    gb = jax.nn.sigmoid(jnp.dot(h, wgb_ref[...], preferred_element_type=F32) + bgb_ref[...])
    pa = jnp.dot(oa_ref[...], wpa_ref[...], preferred_element_type=F32)
    pb = jnp.dot(ob_ref[...], wpb_ref[...], preferred_element_type=F32)
    out_ref[...] = (ga * pa + gb * pb).astype(out_ref.dtype)


def _merge(outs, lses, o_b, h, w_gate, b_gate, w_proj_a, w_proj_b, seq):
    n, d = h.shape
    tm = _pick(seq, 512)
    tpb = seq // tm
    tn = _pick(d, COL_TILE)
    nj = d // tn
    row = lambda width: pl.BlockSpec((tm, width), lambda i, j: (i, 0))

    def streams(arr):
        r, width = arr.shape[1], arr.shape[3]
        return pl.BlockSpec((None, r, tm // r, width), lambda i, j: (i // tpb, 0, i % tpb, 0))

    return pl.pallas_call(
        _merge_kernel,
        grid=(n // tm, nj),
        in_specs=[
            *[streams(a) for a in outs], *[streams(a) for a in lses],
            row(B_WIDTH), row(d),
            pl.BlockSpec((d, tn), lambda i, j: (0, j)),
            pl.BlockSpec((d, tn), lambda i, j: (0, nj + j)),
            pl.BlockSpec((1, tn), lambda i, j: (0, j)),
            pl.BlockSpec((1, tn), lambda i, j: (0, nj + j)),
            pl.BlockSpec((A_GROUP_WIDTH, tn), lambda i, j: (0, j)),
            pl.BlockSpec((B_WIDTH, tn), lambda i, j: (0, j)),
        ],
        out_specs=pl.BlockSpec((tm, tn), lambda i, j: (i, j)),
        out_shape=jax.ShapeDtypeStruct((n, d), BF16),
        scratch_shapes=[
            pltpu.VMEM((tm, A_GROUP_WIDTH), BF16),
            pltpu.VMEM((len(outs) * A_HEADS_PER_GROUP, tm, HEAD_DIM), F32),
            pltpu.VMEM((len(lses), tm, LANES), F32),
        ],
        compiler_params=_params("parallel", "arbitrary"),
        name="gated_merge",
    )(*outs, *lses, o_b, h, w_gate, w_gate, b_gate, b_gate, w_proj_a, w_proj_b)


def _outproj_kernel(mg_ref, w_ref, x_ref, g_ref, sc_ref, sh_ref, xo_ref, ho_ref):
    mix = jnp.dot(mg_ref[...], w_ref[...], preferred_element_type=F32)
    x = x_ref[...] + g_ref[...] * mix
    xo_ref[...] = x
    ho_ref[...] = (_rms(x, x.shape[-1]) * (1.0 + sc_ref[...]) + sh_ref[...]).astype(ho_ref.dtype)


def _outproj(merged, w_out, x, mod, seq):
    n, d = x.shape
    tm = _pick(seq, 256)
    tpb = seq // tm
    row = pl.BlockSpec((tm, d), lambda i: (i, 0))
    return pl.pallas_call(
        _outproj_kernel,
        grid=(n // tm,),
        in_specs=[row, pl.BlockSpec((d, d), lambda i: (0, 0)), row,
                  _mod_spec(d, 2, tpb), _mod_spec(d, 4, tpb), _mod_spec(d, 3, tpb)],
        out_specs=[row, row],
        out_shape=[jax.ShapeDtypeStruct((n, d), F32), jax.ShapeDtypeStruct((n, d), BF16)],
        compiler_params=_params("parallel"),
        name="outproj",
    )(merged, w_out, x, mod, mod, mod)


def _ffn_kernel(h_ref, wu_ref, wd_ref, x_ref, g_ref, sc_ref, sh_ref, o_ref, *rest):
    acc_ref = rest[-1]
    c = pl.program_id(1)

    @pl.when(c == 0)
    def _():
        acc_ref[...] = jnp.zeros(acc_ref.shape, F32)

    u = jnp.maximum(jnp.dot(h_ref[...], wu_ref[...], preferred_element_type=F32), 0.0)
    acc_ref[...] += jnp.dot((u * u).astype(BF16), wd_ref[...], preferred_element_type=F32)

    @pl.when(c == pl.num_programs(1) - 1)
    def _():
        x = x_ref[...] + g_ref[...] * acc_ref[...]
        o_ref[...] = x
        if len(rest) == 2:
            rest[0][...] = (_rms(x, x.shape[-1]) * (1.0 + sc_ref[...]) + sh_ref[...]).astype(rest[0].dtype)


def _ffn(h2, w_up, w_down, x, mod, seq, next_mod):
    n, d = x.shape
    hidden = w_up.shape[1]
    tm = _pick(seq, 512)
    tc = _pick(hidden, 512)
    tpb = seq // tm
    row = pl.BlockSpec((tm, d), lambda i, c: (i, 0))
    emit_next = next_mod is not None
    norm_mod = next_mod if emit_next else mod
    out = pl.pallas_call(
        _ffn_kernel,
        grid=(n // tm, hidden // tc),
        in_specs=[row, pl.BlockSpec((d, tc), lambda i, c: (0, c)), pl.BlockSpec((tc, d), lambda i, c: (c, 0)),
                  row, _mod_spec(d, 5, tpb), _mod_spec(d, 1, tpb), _mod_spec(d, 0, tpb)],
        out_specs=[row, row] if emit_next else [row],
        out_shape=[jax.ShapeDtypeStruct((n, d), F32)] + ([jax.ShapeDtypeStruct((n, d), BF16)] if emit_next else []),
        scratch_shapes=[pltpu.VMEM((tm, d), F32)],
        compiler_params=_params("parallel", "arbitrary"),
        name="ffn",
    )(h2, w_up, w_down, x, mod, norm_mod, norm_mod)
    return (out[0], out[1]) if emit_next else (out[0], None)


def _pack_in_weights(w_in, a_q_gain, a_k_gain, b_q_gain, b_k_gain, idx_k_gain):
    d = w_in.shape[0]
    sizes = (A_WIDTH, A_WIDTH, A_WIDTH, B_WIDTH, B_WIDTH, B_WIDTH, IDX_WIDTH, IDX_DIM, IDX_HEADS)
    parts, off = [], 0
    for sz in sizes:
        parts.append(w_in[:, off:off + sz])
        off += sz
    aq, ak, av, bq, bk, bv, iq, ik, iw = parts
    w_b = jnp.concatenate([bq, bk, iq, bv], axis=1).astype(BF16)
    w_idx = jnp.concatenate([ik, iw, jnp.zeros((d, LANES - IDX_DIM - IDX_HEADS), w_in.dtype)], axis=1).astype(BF16)
    ones = lambda width: jnp.ones((width,), F32)
    b_gain_cols = jnp.concatenate([jnp.tile(b_q_gain * DSA_Q_SCALE, B_HEADS), jnp.tile(b_k_gain, B_HEADS),
                                   ones(IDX_WIDTH + B_WIDTH)]).reshape(1, -1)
    idx_gain_row = jnp.concatenate([idx_k_gain, ones(LANES - IDX_DIM)]).reshape(1, LANES)
    w_groups = []
    for g in range(len(A_GROUPS)):
        sl = slice(g * A_GROUP_WIDTH, (g + 1) * A_GROUP_WIDTH)
        w_groups.append(jnp.concatenate([aq[:, sl], ak[:, sl], av[:, sl]], axis=1).astype(BF16))
    a_gain_cols = jnp.concatenate([
        jnp.tile(a_q_gain, A_HEADS_PER_GROUP), jnp.tile(a_k_gain, A_HEADS_PER_GROUP), ones(A_GROUP_WIDTH),
    ]).reshape(1, A_PACK_WIDTH)
    return w_b, b_gain_cols, w_idx, idx_gain_row, w_groups, a_gain_cols


def kernel(x, c, positions, w_ada, b_ada, w_in, a_q_gain, a_k_gain, b_q_gain, b_k_gain, idx_k_gain,
           w_gate, b_gate, w_proj_a, w_proj_b, w_out, w_up, w_down):
    b, s, d = x.shape
    depth = w_ada.shape[0]
    n = b * s
    topk = min(IDX_TOPK, s // 4)
    assert s % (SEL_TK * SEL_GROUP) == 0 and SEL_TK // 2 >= topk and d % COL_TILE == 0 and s // LANES < 2 ** 15

    tabs = _rope_tables(positions)
    c128, s128, c64, s64 = tabs
    mods = _ada(c, w_ada, b_ada)
    xf = x.reshape(n, d)

    h = _normmod(xf, mods[0], s, 1, 0)
    for l in range(depth):
        mod = mods[l]
        w_b, b_gain_cols, w_idx, idx_gain_row, w_groups, a_gain_cols = _pack_in_weights(
            w_in[l], a_q_gain[l], a_k_gain[l], b_q_gain[l], b_k_gain[l], idx_k_gain[l])

        z_b = _bproj(h, w_b, b_gain_cols, tabs, B_EPILOGUES, "proj_b").reshape(b, s, -1)
        ik, iw = _idxproj(h, w_idx, idx_gain_row, c64, s64)

        a_outs, a_lses = [], []
        for g, (window, dilation) in enumerate(A_GROUPS):
            qkv = _aproj(h, w_groups[g], a_gain_cols, tabs, b, s, dilation)
            o, lse = _dilated(qkv, window)
            a_outs.append(o)
            a_lses.append(lse)

        bias5 = _select(z_b, iw, ik.reshape(b, s, IDX_DIM), topk)
        o_b = _dsa_attention(z_b, bias5).reshape(n, B_WIDTH)

        merged = _merge(a_outs, a_lses, o_b, h, w_gate[l].astype(BF16), b_gate[l].reshape(1, 2 * d),
                        w_proj_a[l].astype(BF16), w_proj_b[l].astype(BF16), s)
        xf, h2 = _outproj(merged, w_out[l].astype(BF16), xf, mod, s)
        xf, h = _ffn(h2, w_up[l].astype(BF16), w_down[l].astype(BF16), xf, mod, s,
                     mods[l + 1] if l + 1 < depth else None)

    return xf.reshape(b, s, d)
```
